```python
import math
import jax, jax.numpy as jnp
from jax import lax
import numpy as np

D_MODEL = 1024
BATCH = 2
SEQ = 16384
DEPTH = 2
DEC_BATCH = 16
DEC_SEQ = 16
PAST_LEN = 4096

CHUNK = 64
Q_BLOCK = 128
N_A = (DEPTH + 1) // 2
N_B = DEPTH // 2
DA_HEADS = 8
DA_HEAD_DIM = 64
DA_V_DIM = 2 * DA_HEAD_DIM
DA_WIDTH = DA_HEADS * DA_V_DIM
MLA_HEADS = 16
MLA_Q_LORA = 768
MLA_KV_LORA = 256
MLA_NOPE = 64
MLA_ROPE = 32
MLA_V = 64
ROPE_THETA = 10000.0
FFN_DENSE = 2816
N_EXPERTS = 8
TOP_K = 2
FFN_EXPERT = 3584
NORM_EPS = 1e-6
NEG_INF = -1e30

kernel_name = "hybrid_diffattn_mla_stream_step"


def _rms_norm(x, g):
    xf = x.astype(jnp.float32)
    y = xf * lax.rsqrt(jnp.mean(xf * xf, axis=-1, keepdims=True) + NORM_EPS)
    return (y * g.astype(jnp.float32)).astype(x.dtype)


def _rope(x, pos):
    half = x.shape[-1] // 2
    freqs = jnp.power(ROPE_THETA, -jnp.arange(half, dtype=jnp.float32) * 2.0 / x.shape[-1])
    ang = pos.astype(jnp.float32)[:, None] * freqs[None, :]
    cos = jnp.cos(ang)[:, None, :]
    sin = jnp.sin(ang)[:, None, :]
    xf = x.astype(jnp.float32)
    x1, x2 = xf[..., :half], xf[..., half:]
    return jnp.concatenate([x1 * cos - x2 * sin, x2 * cos + x1 * sin], axis=-1).astype(x.dtype)


def _chunk_mask(q_pos, k_pos):
    return (k_pos[None, :] // CHUNK) <= (q_pos[:, None] // CHUNK)


def _over_query_blocks(fn, q, q_pos):
    B, T = q.shape[0], q.shape[1]
    if T <= Q_BLOCK:
        return fn(q, q_pos)
    nb = T // Q_BLOCK
    qb = jnp.moveaxis(q.reshape((B, nb, Q_BLOCK) + q.shape[2:]), 1, 0)
    pb = q_pos.reshape(nb, Q_BLOCK)
    out = lax.map(lambda a: fn(a[0], a[1]), (qb, pb))
    out = jnp.moveaxis(out, 0, 1)
    return out.reshape((B, T) + out.shape[3:])


def _diff_mixer(h, pos0, past_k, past_v, w_qkv, lam, subln, w_o, lam_init):
    B, T, _ = h.shape
    qkv = h @ w_qkv
    q = qkv[..., :DA_WIDTH].reshape(B, T, DA_HEADS, DA_V_DIM)
    k = qkv[..., DA_WIDTH:2 * DA_WIDTH].reshape(B, T, DA_HEADS, DA_V_DIM)
    v = qkv[..., 2 * DA_WIDTH:].reshape(B, T, DA_HEADS, DA_V_DIM)
    if past_k is None:
        k_all, v_all = k, v
    else:
        k_all = jnp.concatenate([past_k.astype(k.dtype), k], axis=1)
        v_all = jnp.concatenate([past_v.astype(v.dtype), v], axis=1)
    n_keys = k_all.shape[1]
    q_pos = pos0 + jnp.arange(T, dtype=jnp.int32)
    k_pos = jnp.arange(n_keys, dtype=jnp.int32)
    lf = lam.astype(jnp.float32)
    lam_full = jnp.exp(jnp.sum(lf[0] * lf[1])) - jnp.exp(jnp.sum(lf[2] * lf[3])) + lam_init
    slopes = jnp.exp2(-8.0 * jnp.arange(1, DA_HEADS + 1, dtype=jnp.float32) / DA_HEADS)
    k1, k2 = k_all[..., :DA_HEAD_DIM], k_all[..., DA_HEAD_DIM:]
    scale = DA_HEAD_DIM ** -0.5

    def block(qb, pb):
        s1 = jnp.einsum('bqhd,bkhd->bhqk', qb[..., :DA_HEAD_DIM], k1, preferred_element_type=jnp.float32)
        s2 = jnp.einsum('bqhd,bkhd->bhqk', qb[..., DA_HEAD_DIM:], k2, preferred_element_type=jnp.float32)
        dist = jnp.abs(pb[:, None] - k_pos[None, :]).astype(jnp.float32)
        bias = jnp.where(_chunk_mask(pb, k_pos)[None], -slopes[:, None, None] * dist[None], NEG_INF)
        p1 = jax.nn.softmax(s1 * scale + bias[None], axis=-1)
        p2 = jax.nn.softmax(s2 * scale + bias[None], axis=-1)
        a = (p1 - lam_full * p2).astype(v_all.dtype)
        return jnp.einsum('bhqk,bkhe->bqhe', a, v_all)

    o = _over_query_blocks(block, q, q_pos)
    o = _rms_norm(o, subln) * (1.0 - lam_init)
    y = o.reshape(B, T, DA_WIDTH) @ w_o
    return y, k, v


def _mla_mixer(h, pos0, past_c, past_pe, w_a, g_q, g_kv, w_uq, w_ukv, w_o):
    B, T, _ = h.shape
    a = h @ w_a
    c_q = _rms_norm(a[..., :MLA_Q_LORA], g_q)
    c_kv = _rms_norm(a[..., MLA_Q_LORA:MLA_Q_LORA + MLA_KV_LORA], g_kv)
    pe_raw = a[..., MLA_Q_LORA + MLA_KV_LORA:]
    q_pos = pos0 + jnp.arange(T, dtype=jnp.int32)
    q = (c_q @ w_uq).reshape(B, T, MLA_HEADS, MLA_NOPE + MLA_ROPE)
    q = jnp.concatenate([q[..., :MLA_NOPE], _rope(q[..., MLA_NOPE:], q_pos)], axis=-1)
    k_pe = _rope(pe_raw[:, :, None, :], q_pos)[:, :, 0, :]
    if past_c is None:
        c_all, pe_all = c_kv, k_pe
    else:
        c_all = jnp.concatenate([past_c.astype(c_kv.dtype), c_kv], axis=1)
        pe_all = jnp.concatenate([past_pe.astype(k_pe.dtype), k_pe], axis=1)
    n_keys = c_all.shape[1]
    k_pos = jnp.arange(n_keys, dtype=jnp.int32)
    kv = (c_all @ w_ukv).reshape(B, n_keys, MLA_HEADS, MLA_NOPE + MLA_V)
    k_nope, v = kv[..., :MLA_NOPE], kv[..., MLA_NOPE:]
    scale = (MLA_NOPE + MLA_ROPE) ** -0.5

    def block(qb, pb):
        s = jnp.einsum('bqhd,bkhd->bhqk', qb[..., :MLA_NOPE], k_nope, preferred_element_type=jnp.float32)
        s = s + jnp.einsum('bqhr,bkr->bhqk', qb[..., MLA_NOPE:], pe_all, preferred_element_type=jnp.float32)
        s = jnp.where(_chunk_mask(pb, k_pos)[None, None], s * scale, NEG_INF)
        p = jax.nn.softmax(s, axis=-1).astype(v.dtype)
        return jnp.einsum('bhqk,bkhe->bqhe', p, v)

    o = _over_query_blocks(block, q, q_pos)
    y = o.reshape(B, T, MLA_HEADS * MLA_V) @ w_o
    return y, c_kv, k_pe


def _swiglu(h, w_gu, w_down):
    gu = h @ w_gu
    f = w_down.shape[0]
    return (jax.nn.silu(gu[..., :f]) * gu[..., f:]) @ w_down


def _moe(h, router, w_gu, w_down):
    logits = jnp.einsum('btd,de->bte', h, router, preferred_element_type=jnp.float32)
    top_v, top_i = lax.top_k(logits, TOP_K)
    gates = jax.nn.softmax(top_v, axis=-1)
    comb = jnp.sum(jax.nn.one_hot(top_i, N_EXPERTS, dtype=jnp.float32) * gates[..., None], axis=-2)
    comb = comb.astype(h.dtype)
    y = jnp.zeros_like(h)
    for e in range(N_EXPERTS):
        y = y + comb[..., e:e + 1] * _swiglu(h, w_gu[e], w_down[e])
    return y


def _trunk(x, pos0, cache_diff_k, cache_diff_v, cache_mla_ckv, cache_mla_kpe,
           norm_mix, norm_ffn, norm_final, diff_w_qkv, diff_lambda, diff_subln, diff_w_o,
           mla_w_a, mla_norm_q, mla_norm_kv, mla_w_uq, mla_w_ukv, mla_w_o,
           ffn_w_gu, ffn_w_down, moe_router, moe_w_gu, moe_w_down):
    dk, dv, mc, mp = [], [], [], []
    for i in range(DEPTH):
        j = i // 2
        h = _rms_norm(x, norm_mix[i])
        if i % 2 == 0:
            pk = None if cache_diff_k is None else cache_diff_k[j]
            pv = None if cache_diff_v is None else cache_diff_v[j]
            lam_init = 0.8 - 0.6 * math.exp(-0.3 * i)
            out, nk, nv = _diff_mixer(h, pos0, pk, pv, diff_w_qkv[j], diff_lambda[j], diff_subln[j],
                                      diff_w_o[j], lam_init)
            dk.append(nk)
            dv.append(nv)
        else:
            pc = None if cache_mla_ckv is None else cache_mla_ckv[j]
            pp = None if cache_mla_kpe is None else cache_mla_kpe[j]
            out, nc, npe = _mla_mixer(h, pos0, pc, pp, mla_w_a[j], mla_norm_q[j], mla_norm_kv[j],
                                      mla_w_uq[j], mla_w_ukv[j], mla_w_o[j])
            mc.append(nc)
            mp.append(npe)
        x = x + out
        h = _rms_norm(x, norm_ffn[i])
        if i % 2 == 0:
            x = x + _swiglu(h, ffn_w_gu[j], ffn_w_down[j])
        else:
            x = x + _moe(h, moe_router[j], moe_w_gu[j], moe_w_down[j])
    y = _rms_norm(x, norm_final)
    return y, jnp.stack(dk, 0), jnp.stack(dv, 0), jnp.stack(mc, 0), jnp.stack(mp, 0)


def setup_inputs(seed: int = 0) -> dict:
    key = jax.random.key(seed)
    ks = jax.random.split(key, 32)
    f32 = jnp.float32

    def w(k, shape, fan_in):
        return jax.random.normal(k, shape, f32) * (fan_in ** -0.5)

    def gain(k, shape):
        return 1.0 + 0.01 * jax.random.normal(k, shape, f32)

    return {
        "x_prompt": jax.random.normal(ks[0], (BATCH, SEQ, D_MODEL), f32),
        "x_sample": jax.random.normal(ks[1], (DEC_BATCH, DEC_SEQ, D_MODEL), f32),
        "cache_diff_k": jax.random.normal(ks[2], (N_A, DEC_BATCH, PAST_LEN, DA_HEADS, 2 * DA_HEAD_DIM), f32),
        "cache_diff_v": jax.random.normal(ks[3], (N_A, DEC_BATCH, PAST_LEN, DA_HEADS, DA_V_DIM), f32),
        "cache_mla_ckv": jax.random.normal(ks[4], (N_B, DEC_BATCH, PAST_LEN, MLA_KV_LORA), f32),
        "cache_mla_kpe": jax.random.normal(ks[5], (N_B, DEC_BATCH, PAST_LEN, MLA_ROPE), f32),
        "norm_mix": gain(ks[6], (DEPTH, D_MODEL)),
        "norm_ffn": gain(ks[7], (DEPTH, D_MODEL)),
        "norm_final": gain(ks[8], (D_MODEL,)),
        "diff_w_qkv": w(ks[9], (N_A, D_MODEL, 3 * DA_WIDTH), D_MODEL),
        "diff_lambda": 0.1 * jax.random.normal(ks[10], (N_A, 4, DA_HEAD_DIM), f32),
        "diff_subln": gain(ks[11], (N_A, DA_V_DIM)),
        "diff_w_o": w(ks[12], (N_A, DA_WIDTH, D_MODEL), DA_WIDTH),
        "mla_w_a": w(ks[13], (N_B, D_MODEL, MLA_Q_LORA + MLA_KV_LORA + MLA_ROPE), D_MODEL),
        "mla_norm_q": gain(ks[14], (N_B, MLA_Q_LORA)),
        "mla_norm_kv": gain(ks[15], (N_B, MLA_KV_LORA)),
        "mla_w_uq": w(ks[16], (N_B, MLA_Q_LORA, MLA_HEADS * (MLA_NOPE + MLA_ROPE)), MLA_Q_LORA),
        "mla_w_ukv": w(ks[17], (N_B, MLA_KV_LORA, MLA_HEADS * (MLA_NOPE + MLA_V)), MLA_KV_LORA),
        "mla_w_o": w(ks[18], (N_B, MLA_HEADS * MLA_V, D_MODEL), MLA_HEADS * MLA_V),
        "ffn_w_gu": w(ks[19], (N_A, D_MODEL, 2 * FFN_DENSE), D_MODEL),
        "ffn_w_down": w(ks[20], (N_A, FFN_DENSE, D_MODEL), FFN_DENSE),
        "moe_router": w(ks[21], (N_B, D_MODEL, N_EXPERTS), D_MODEL),
        "moe_w_gu": w(ks[22], (N_B, N_EXPERTS, D_MODEL, 2 * FFN_EXPERT), D_MODEL),
        "moe_w_down": w(ks[23], (N_B, N_EXPERTS, FFN_EXPERT, D_MODEL), FFN_EXPERT),
    }


def reference(x_prompt, x_sample, cache_diff_k, cache_diff_v, cache_mla_ckv, cache_mla_kpe,
              norm_mix, norm_ffn, norm_final, diff_w_qkv, diff_lambda, diff_subln, diff_w_o,
              mla_w_a, mla_norm_q, mla_norm_kv, mla_w_uq, mla_w_ukv, mla_w_o,
              ffn_w_gu, ffn_w_down, moe_router, moe_w_gu, moe_w_down):
    y_prompt, diff_k_prompt, diff_v_prompt, mla_ckv_prompt, mla_kpe_prompt = _trunk(
        x_prompt, 0, None, None, None, None,
        norm_mix, norm_ffn, norm_final, diff_w_qkv, diff_lambda, diff_subln, diff_w_o,
        mla_w_a, mla_norm_q, mla_norm_kv, mla_w_uq, mla_w_ukv, mla_w_o,
        ffn_w_gu, ffn_w_down, moe_router, moe_w_gu, moe_w_down)
    past_len = cache_diff_k.shape[2]
    y_sample, diff_k_sample, diff_v_sample, mla_ckv_sample, mla_kpe_sample = _trunk(
        x_sample, past_len, cache_diff_k, cache_diff_v, cache_mla_ckv, cache_mla_kpe,
        norm_mix, norm_ffn, norm_final, diff_w_qkv, diff_lambda, diff_subln, diff_w_o,
        mla_w_a, mla_norm_q, mla_norm_kv, mla_w_uq, mla_w_ukv, mla_w_o,
        ffn_w_gu, ffn_w_down, moe_router, moe_w_gu, moe_w_down)
    return (y_prompt, y_sample, diff_k_prompt, diff_v_prompt, mla_ckv_prompt, mla_kpe_prompt,
            diff_k_sample, diff_v_sample, mla_ckv_sample, mla_kpe_sample)
```

```python
import functools
import math

import jax
import jax.numpy as jnp
from jax import lax
from jax.experimental import pallas as pl
from jax.experimental.pallas import tpu as pltpu

NORM_EPS = 1e-6
NEG_INF = -1e30
CHUNK = 64
ROPE_THETA = 10000.0
TOP_K = 2
LANES = 128
VMEM_LIMIT_BYTES = 56 * 2**20
BF16 = jnp.bfloat16
F32 = jnp.float32


def _params(*sem):
    return pltpu.CompilerParams(dimension_semantics=sem, vmem_limit_bytes=VMEM_LIMIT_BYTES)


def _const_spec(shape):
    nd = len(shape)
    return pl.BlockSpec(shape, lambda *_: (0,) * nd, pipeline_mode=pl.Buffered(1))


def _rms(x, g):
    ms = jnp.mean(x * x, axis=-1, keepdims=True)
    return x * lax.rsqrt(ms + NORM_EPS) * g


def _dot(a, b):
    return jnp.dot(a, b, preferred_element_type=F32)


def _dot_nt(a, b):
    return lax.dot_general(a, b, (((1,), (1,)), ((), ())), preferred_element_type=F32)


def _row_tile(t, cap):
    tm = min(t, cap)
    assert t % tm == 0
    return tm


def _qkv_kernel(x_ref, g_ref, w_ref, q_ref, k_ref, v_ref, kb_ref, vt_ref, *, width, scale):
    h = _rms(x_ref[0], g_ref[...]).astype(BF16)
    qkv = _dot(h, w_ref[...])
    q_ref[0] = (qkv[:, :width] * scale).astype(BF16)
    k = qkv[:, width:2 * width]
    v = qkv[:, 2 * width:]
    k_ref[0] = k
    v_ref[0] = v
    kb_ref[0] = k.astype(BF16)
    vt_ref[0] = v.T.astype(BF16)


def _qkv_proj(x, g, w_bf, scale):
    bx, tx, d = x.shape
    width = w_bf.shape[1] // 3
    tm = _row_tile(tx, 512)
    row = lambda n: pl.BlockSpec((1, tm, n), lambda b, t: (b, t, 0))
    return pl.pallas_call(
        functools.partial(_qkv_kernel, width=width, scale=scale),
        grid=(bx, tx // tm),
        in_specs=[row(d), _const_spec((1, d)), _const_spec(w_bf.shape)],
        out_specs=[row(width), row(width), row(width), row(width),
                   pl.BlockSpec((1, width, tm), lambda b, t: (b, 0, t))],
        out_shape=[jax.ShapeDtypeStruct((bx, tx, width), BF16),
                   jax.ShapeDtypeStruct((bx, tx, width), F32),
                   jax.ShapeDtypeStruct((bx, tx, width), F32),
                   jax.ShapeDtypeStruct((bx, tx, width), BF16),
                   jax.ShapeDtypeStruct((bx, width, tx), BF16)],
        compiler_params=_params("parallel", "parallel"),
        name="qkv_proj",
    )(x, g, w_bf)


def _softmax_tile_update(z, shift, vt_blk, m_ref, l_ref, acc_ref, rows):
    m_old = m_ref[...]
    m_new = jnp.maximum(m_old, jnp.max(z, axis=0, keepdims=True) + shift)
    alpha = jnp.exp(m_old - m_new)
    p = jnp.exp(z - (m_new - shift))
    l_ref[...] = alpha * l_ref[...] + jnp.sum(p, axis=0, keepdims=True)
    acc_ref[rows, :] = alpha * acc_ref[rows, :] + _dot(vt_blk, p.astype(BF16))
    m_ref[...] = m_new


def _lambda_full(lam_ref, lam_init):
    lf = lam_ref[...]
    s01 = jnp.sum(lf[0:1, :] * lf[1:2, :], axis=-1, keepdims=True)
    s23 = jnp.sum(lf[2:3, :] * lf[3:4, :], axis=-1, keepdims=True)
    return jnp.exp(s01) - jnp.exp(s23) + lam_init


def _diff_flash_kernel(slopes_ref, q_ref, k_ref, vt_ref, lam_ref, subln_ref, o_ref,
                       m1, l1, m2, l2, acc1, acc2, *, tq, half, lam_init):
    h = pl.program_id(1)
    i = pl.program_id(2)
    slope = slopes_ref[h]
    q = q_ref[0]
    lane = lax.broadcasted_iota(jnp.int32, q.shape, 1)
    zero = jnp.zeros_like(q)
    q_halves = (jnp.where(lane < half, q, zero), jnp.where(lane >= half, q, zero))
    stats = ((m1, l1, acc1), (m2, l2, acc2))
    kl = lax.broadcasted_iota(jnp.int32, (tq, tq), 0)
    ql = lax.broadcasted_iota(jnp.int32, (tq, tq), 1)
    all_rows = slice(None)

    for m_ref, l_ref, acc_ref in stats:
        m_ref[...] = jnp.full(m_ref.shape, NEG_INF, F32)
        l_ref[...] = jnp.zeros(l_ref.shape, F32)
        acc_ref[...] = jnp.zeros(acc_ref.shape, F32)

    def update(k_blk, vt_blk, bias, shift):
        for qh, (m_ref, l_ref, acc_ref) in zip(q_halves, stats):
            z = _dot_nt(k_blk, qh) + bias
            _softmax_tile_update(z, shift, vt_blk, m_ref, l_ref, acc_ref, all_rows)

    bias_past = slope * kl.astype(F32)

    def body(j, carry):
        off = pl.multiple_of(j * tq, tq)
        shift = slope * ((j - i) * tq).astype(F32)
        update(k_ref[0, pl.ds(off, tq), :], vt_ref[0, :, pl.ds(off, tq)], bias_past, shift)
        return carry

    lax.fori_loop(0, i, body, 0)

    off = pl.multiple_of(i * tq, tq)
    visible = (kl // CHUNK) <= (ql // CHUNK)
    bias_diag = jnp.where(visible, slope * (ql - jnp.abs(ql - kl)).astype(F32), NEG_INF)
    update(k_ref[0, pl.ds(off, tq), :], vt_ref[0, :, pl.ds(off, tq)], bias_diag, 0.0)

    lam = _lambda_full(lam_ref, lam_init)
    o_t = acc1[...] / l1[...] - lam * (acc2[...] / l2[...])
    o = _rms(o_t.T, subln_ref[...]) * (1.0 - lam_init)
    o_ref[0] = o.astype(BF16)


def _diff_attn_prompt(q, k_bf, vt_bf, lam, subln, slopes, lam_init):
    b, t, width = q.shape
    dv = subln.shape[-1]
    n_heads = width // dv
    tq = _row_tile(t, 512)
    return pl.pallas_call(
        functools.partial(_diff_flash_kernel, tq=tq, half=dv // 2, lam_init=lam_init),
        grid=(b, n_heads, t // tq),
        in_specs=[pl.BlockSpec(memory_space=pltpu.SMEM),
                  pl.BlockSpec((1, tq, dv), lambda b_, h, i: (b_, i, h)),
                  pl.BlockSpec((1, t, dv), lambda b_, h, i: (b_, 0, h)),
                  pl.BlockSpec((1, dv, t), lambda b_, h, i: (b_, h, 0)),
                  pl.BlockSpec(lam.shape, lambda b_, h, i: (0, 0)),
                  pl.BlockSpec(subln.shape, lambda b_, h, i: (0, 0))],
        out_specs=pl.BlockSpec((1, tq, dv), lambda b_, h, i: (b_, i, h)),
        out_shape=jax.ShapeDtypeStruct((b, t, width), BF16),
        scratch_shapes=[pltpu.VMEM((1, tq), F32)] * 4 + [pltpu.VMEM((dv, tq), F32)] * 2,
        compiler_params=_params("parallel", "parallel", "arbitrary"),
        name="diff_attn_prompt",
    )(slopes, q, k_bf, vt_bf, lam, subln)


def _mla_flash_kernel(q_ref, k_ref, vt_ref, o_ref, m_a, l_a, m_b, l_b, acc, *, tq, dv):
    i = pl.program_id(2)
    q = q_ref[0]
    heads = ((q[:, :LANES], m_a, l_a, slice(0, dv)), (q[:, LANES:], m_b, l_b, slice(dv, 2 * dv)))
    for _, m_ref, l_ref, _ in heads:
        m_ref[...] = jnp.full(m_ref.shape, NEG_INF, F32)
        l_ref[...] = jnp.zeros(l_ref.shape, F32)
    acc[...] = jnp.zeros(acc.shape, F32)

    def update(off, mask):
        for n, (qh, m_ref, l_ref, rows) in enumerate(heads):
            z = _dot_nt(k_ref[0, pl.ds(off, tq), n * LANES:(n + 1) * LANES], qh)
            if mask is not None:
                z = jnp.where(mask, z, NEG_INF)
            _softmax_tile_update(z, 0.0, vt_ref[0, rows, pl.ds(off, tq)], m_ref, l_ref, acc, rows)

    def body(j, carry):
        update(pl.multiple_of(j * tq, tq), None)
        return carry

    lax.fori_loop(0, i, body, 0)
    kl = lax.broadcasted_iota(jnp.int32, (tq, tq), 0)
    ql = lax.broadcasted_iota(jnp.int32, (tq, tq), 1)
    update(pl.multiple_of(i * tq, tq), (kl // CHUNK) <= (ql // CHUNK))

    o_t = jnp.concatenate([acc[0:dv, :] / l_a[...], acc[dv:2 * dv, :] / l_b[...]], axis=0)
    o_ref[0] = o_t.T.astype(BF16)


def _mla_attn_prompt(qcat, kcat, vt_bf, dv):
    b, t, wq = qcat.shape
    n_pairs = wq // (2 * LANES)
    tq = _row_tile(t, 512)
    return pl.pallas_call(
        functools.partial(_mla_flash_kernel, tq=tq, dv=dv),
        grid=(b, n_pairs, t // tq),
        in_specs=[pl.BlockSpec((1, tq, 2 * LANES), lambda b_, h, i: (b_, i, h)),
                  pl.BlockSpec((1, t, 2 * LANES), lambda b_, h, i: (b_, 0, h)),
                  pl.BlockSpec((1, 2 * dv, t), lambda b_, h, i: (b_, h, 0))],
        out_specs=pl.BlockSpec((1, tq, 2 * dv), lambda b_, h, i: (b_, i, h)),
        out_shape=jax.ShapeDtypeStruct((b, t, n_pairs * 2 * dv), BF16),
        scratch_shapes=[pltpu.VMEM((1, tq), F32)] * 4 + [pltpu.VMEM((2 * dv, tq), F32)],
        compiler_params=_params("parallel", "parallel", "arbitrary"),
        name="mla_attn_prompt",
    )(qcat, kcat, vt_bf)


def _block_diag_queries(q, n_groups, group_width):
    qt = jnp.concatenate([q] * n_groups, axis=0)
    r = lax.broadcasted_iota(jnp.int32, qt.shape, 0)
    c = lax.broadcasted_iota(jnp.int32, qt.shape, 1)
    return jnp.where((c // group_width) == (r // q.shape[0]), qt, jnp.zeros_like(qt))


def _decode_update(z, v_bf, m_ref, l_ref, acc_ref):
    m_old = m_ref[...]
    m_new = jnp.maximum(m_old, jnp.max(z, axis=-1, keepdims=True))
    alpha = jnp.exp(m_old - m_new)
    p = jnp.exp(z - m_new)
    l_ref[...] = alpha * l_ref[...] + jnp.sum(p, axis=-1, keepdims=True)
    acc_ref[...] = alpha * acc_ref[...] + _dot(p.astype(BF16), v_bf)
    m_ref[...] = m_new


def _decode_positions(n_rows, n_keys, nq, past, key_start, n_valid):
    r = lax.broadcasted_iota(jnp.int32, (n_rows, n_keys), 0)
    c = lax.broadcasted_iota(jnp.int32, (n_rows, n_keys), 1)
    q_pos = past + r % nq
    k_pos = key_start + c
    visible = ((k_pos // CHUNK) <= (q_pos // CHUNK)) & (c < n_valid)
    return q_pos, k_pos, visible


def _pad_rows(x, n):
    return jnp.concatenate([x, jnp.zeros((n - x.shape[0], x.shape[1]), x.dtype)], axis=0)


def _diff_decode_kernel(slope_ref, q_ref, ck_ref, cv_ref, kn_ref, vn_ref, lam_ref, subln_ref,
                        o_ref, qbd, m, l, acc, *, past, tk, nq, dv, lam_init):
    j = pl.program_id(1)
    n_rows = qbd.shape[0]

    @pl.when(j == 0)
    def _():
        qbd[...] = _block_diag_queries(q_ref[0], n_rows // nq, dv // 2)
        m[...] = jnp.full(m.shape, NEG_INF, F32)
        l[...] = jnp.zeros(l.shape, F32)
        acc[...] = jnp.zeros(acc.shape, F32)

    def step(k_f32, v_f32, key_start, n_valid):
        n_keys = k_f32.shape[0]
        s = _dot_nt(qbd[...], k_f32.astype(BF16))
        q_pos, k_pos, visible = _decode_positions(n_rows, n_keys, nq, past, key_start, n_valid)
        dist = jnp.abs(q_pos - k_pos).astype(F32)
        z = jnp.where(visible, s - slope_ref[...] * dist, NEG_INF)
        _decode_update(z, v_f32.astype(BF16), m, l, acc)

    step(ck_ref[0], cv_ref[0], j * tk, tk)

    @pl.when(j == pl.num_programs(1) - 1)
    def _():
        step(_pad_rows(kn_ref[0], LANES), _pad_rows(vn_ref[0], LANES), past, nq)
        lam = _lambda_full(lam_ref, lam_init)
        for hh in range(n_rows // (2 * nq)):
            r0 = hh * 2 * nq
            cols = slice(hh * dv, (hh + 1) * dv)
            o1 = acc[r0:r0 + nq, cols] / l[r0:r0 + nq, :]
            o2 = acc[r0 + nq:r0 + 2 * nq, cols] / l[r0 + nq:r0 + 2 * nq, :]
            o = _rms(o1 - lam * o2, subln_ref[...]) * (1.0 - lam_init)
            o_ref[0, :, cols] = o.astype(BF16)


def _diff_attn_decode(q, cache_k, cache_v, k_new, v_new, lam, subln, slope_rows, lam_init):
    db, nq, width = q.shape
    past = cache_k.shape[1]
    dv = subln.shape[-1]
    n_rows = (width // dv) * 2 * nq
    tk = _row_tile(past, 512)
    stream = lambda n: pl.BlockSpec((1, nq, n), lambda b, j: (b, 0, 0))
    cache = pl.BlockSpec((1, tk, width), lambda b, j: (b, j, 0))
    return pl.pallas_call(
        functools.partial(_diff_decode_kernel, past=past, tk=tk, nq=nq, dv=dv, lam_init=lam_init),
        grid=(db, past // tk),
        in_specs=[_const_spec(slope_rows.shape), stream(width), cache, cache, stream(width),
                  stream(width), _const_spec(lam.shape), _const_spec(subln.shape)],
        out_specs=stream(width),
        out_shape=jax.ShapeDtypeStruct((db, nq, width), BF16),
        scratch_shapes=[pltpu.VMEM((n_rows, width), BF16), pltpu.VMEM((n_rows, 1), F32),
                        pltpu.VMEM((n_rows, 1), F32), pltpu.VMEM((n_rows, width), F32)],
        compiler_params=_params("parallel", "arbitrary"),
        name="diff_attn_decode",
    )(slope_rows, q, cache_k, cache_v, k_new, v_new, lam, subln)


def _mla_decode_kernel(q_ref, cc_ref, cpe_ref, cn_ref, pen_ref, wuk_ref, wuv_ref, o_ref,
                       qbd, m, l, acc, *, past, tk, nq, dv):
    j = pl.program_id(1)
    n_rows = qbd.shape[0]
    n_heads = n_rows // nq

    @pl.when(j == 0)
    def _():
        qbd[...] = _block_diag_queries(q_ref[0], n_heads, LANES)
        m[...] = jnp.full(m.shape, NEG_INF, F32)
        l[...] = jnp.zeros(l.shape, F32)
        acc[...] = jnp.zeros(acc.shape, F32)

    def step(c_f32, pe_f32, key_start, n_valid):
        n_keys = c_f32.shape[0]
        c_bf = c_f32.astype(BF16)
        kcat = (_dot(c_bf, wuk_ref[...]) + jnp.concatenate([pe_f32] * n_heads, axis=1)).astype(BF16)
        v_bf = _dot(c_bf, wuv_ref[...]).astype(BF16)
        s = _dot_nt(qbd[...], kcat)
        _, _, visible = _decode_positions(n_rows, n_keys, nq, past, key_start, n_valid)
        _decode_update(jnp.where(visible, s, NEG_INF), v_bf, m, l, acc)

    step(cc_ref[0], cpe_ref[0], j * tk, tk)

    @pl.when(j == pl.num_programs(1) - 1)
    def _():
        step(_pad_rows(cn_ref[0], LANES), _pad_rows(pen_ref[0], LANES), past, nq)
        lane = lax.broadcasted_iota(jnp.int32, (nq, 2 * dv), 1)
        for pair in range(n_heads // 2):
            r0 = pair * 2 * nq
            cols = slice(pair * 2 * dv, (pair + 1) * 2 * dv)
            even = acc[r0:r0 + nq, cols] / l[r0:r0 + nq, :]
            odd = acc[r0 + nq:r0 + 2 * nq, cols] / l[r0 + nq:r0 + 2 * nq, :]
            o_ref[0, :, cols] = jnp.where(lane < dv, even, odd).astype(BF16)


def _mla_attn_decode(qcat, cache_c, cache_pe, c_new, pe_new, wuk_bf, wuv_bf):
    db, nq, wq = qcat.shape
    past, lora = cache_c.shape[1], cache_c.shape[2]
    n_heads = wq // LANES
    wv = wuv_bf.shape[1]
    dv = wv // n_heads
    n_rows = n_heads * nq
    tk = _row_tile(past, 512)
    stream = lambda n: pl.BlockSpec((1, nq, n), lambda b, j: (b, 0, 0))
    return pl.pallas_call(
        functools.partial(_mla_decode_kernel, past=past, tk=tk, nq=nq, dv=dv),
        grid=(db, past // tk),
        in_specs=[stream(wq),
                  pl.BlockSpec((1, tk, lora), lambda b, j: (b, j, 0)),
                  pl.BlockSpec((1, tk, LANES), lambda b, j: (b, j, 0)),
                  stream(lora), stream(LANES),
                  _const_spec(wuk_bf.shape), _const_spec(wuv_bf.shape)],
        out_specs=stream(wv),
        out_shape=jax.ShapeDtypeStruct((db, nq, wv), BF16),
        scratch_shapes=[pltpu.VMEM((n_rows, wq), BF16), pltpu.VMEM((n_rows, 1), F32),
                        pltpu.VMEM((n_rows, 1), F32), pltpu.VMEM((n_rows, wv), F32)],
        compiler_params=_params("parallel", "arbitrary"),
        name="mla_attn_decode",
    )(qcat, cache_c, cache_pe, c_new, pe_new, wuk_bf, wuv_bf)


def _silu(x):
    return x * jax.nn.sigmoid(x)


def _out_ffn_kernel(x_ref, o_ref, wo_ref, g_ref, wgu_ref, wd_ref, out_ref, *, f, n_chunks):
    x1 = x_ref[...] + _dot(o_ref[...], wo_ref[...])
    h = _rms(x1, g_ref[...]).astype(BF16)
    fc = f // n_chunks
    acc = x1
    for c in range(n_chunks):
        gate = _dot(h, wgu_ref[:, c * fc:(c + 1) * fc])
        up = _dot(h, wgu_ref[:, f + c * fc:f + (c + 1) * fc])
        acc = acc + _dot((_silu(gate) * up).astype(BF16), wd_ref[c * fc:(c + 1) * fc, :])
    out_ref[...] = acc


def _out_ffn(x, o, wo_bf, g, wgu_bf, wd_bf):
    n, d = x.shape
    f = wd_bf.shape[0]
    tm = _row_tile(n, 512)
    n_chunks = 2 if f % (2 * LANES) == 0 else 1
    row = lambda w: pl.BlockSpec((tm, w), lambda t: (t, 0))
    return pl.pallas_call(
        functools.partial(_out_ffn_kernel, f=f, n_chunks=n_chunks),
        grid=(n // tm,),
        in_specs=[row(d), row(o.shape[1]), _const_spec(wo_bf.shape), _const_spec((1, d)),
                  _const_spec(wgu_bf.shape), _const_spec(wd_bf.shape)],
        out_specs=row(d),
        out_shape=jax.ShapeDtypeStruct((n, d), F32),
        compiler_params=_params("parallel"),
        name="attn_out_ffn",
    )(x, o, wo_bf, g, wgu_bf, wd_bf)


def _mla_proj_kernel(x_ref, g_ref, waq_ref, wakv_ref, wpe_ref, gq_ref, gkv_ref, wuq_ref, wuqs_ref,
                     wuk_ref, wuv_ref, c0_ref, s0_ref, c64_ref, s64_ref,
                     ckv_ref, kpe_ref, pe64_ref, q_ref, k_ref, vt_ref, *, rope, scale):
    h = _rms(x_ref[0], g_ref[...]).astype(BF16)
    c_q = _rms(_dot(h, waq_ref[...]), gq_ref[...]).astype(BF16)
    c_kv = _rms(_dot(h, wakv_ref[...]), gkv_ref[...])
    ckv_ref[0] = c_kv
    c_kv_bf = c_kv.astype(BF16)
    pe = _dot(h, wpe_ref[...])
    kpe0 = pe[:, 0:LANES] * c0_ref[...] + pe[:, LANES:2 * LANES] * s0_ref[...]
    kpe_ref[0] = kpe0[:, :rope]
    c64 = c64_ref[...]
    s64 = s64_ref[...]
    pe64 = pe[:, 2 * LANES:3 * LANES] * c64 + pe[:, 3 * LANES:] * s64
    pe64_ref[0] = pe64
    q = _dot(c_q, wuq_ref[...])
    q_sw = _dot(c_q, wuqs_ref[...])
    k = _dot(c_kv_bf, wuk_ref[...])
    for hh in range(q.shape[1] // LANES):
        cols = slice(hh * LANES, (hh + 1) * LANES)
        q_ref[0, :, cols] = ((q[:, cols] * c64 + q_sw[:, cols] * s64) * scale).astype(BF16)
        k_ref[0, :, cols] = (k[:, cols] + pe64).astype(BF16)
    vt_ref[0] = _dot(c_kv_bf, wuv_ref[...]).T.astype(BF16)


def _mla_proj(x, g, w, tabs, rope, scale):
    bx, tx, d = x.shape
    tm = _row_tile(tx, 256)
    lora = w["wakv"].shape[1]
    wq, wv = w["wuq"].shape[1], w["wuv"].shape[1]
    row = lambda n: pl.BlockSpec((1, tm, n), lambda b, t: (b, t, 0))
    tab = pl.BlockSpec((tm, LANES), lambda b, t: (t, 0))
    weights = [w[k] for k in ("waq", "wakv", "wpe", "gq", "gkv", "wuq", "wuqs", "wuk", "wuv")]
    return pl.pallas_call(
        functools.partial(_mla_proj_kernel, rope=rope, scale=scale),
        grid=(bx, tx // tm),
        in_specs=[row(d), _const_spec((1, d))] + [_const_spec(a.shape) for a in weights] + [tab] * 4,
        out_specs=[row(lora), row(rope), row(LANES), row(wq), row(wq),
                   pl.BlockSpec((1, wv, tm), lambda b, t: (b, 0, t))],
        out_shape=[jax.ShapeDtypeStruct((bx, tx, lora), F32),
                   jax.ShapeDtypeStruct((bx, tx, rope), F32),
                   jax.ShapeDtypeStruct((bx, tx, LANES), F32),
                   jax.ShapeDtypeStruct((bx, tx, wq), BF16),
                   jax.ShapeDtypeStruct((bx, tx, wq), BF16),
                   jax.ShapeDtypeStruct((bx, wv, tx), BF16)],
        compiler_params=_params("parallel", "parallel"),
        name="mla_proj",
    )(x, g, *weights, *tabs)


def _moe_kernel(x_ref, o_ref, wo_ref, g_ref, router_ref, wg_ref, wu_ref, wd_ref, gfin_ref, out_ref,
                x3_scr, h_scr, comb_scr, acc_scr, *, n_exp, n_fc):
    s = pl.program_id(1)
    lane = lax.broadcasted_iota(jnp.int32, comb_scr.shape, 1).astype(F32)

    @pl.when(s == 0)
    def _():
        x3 = x_ref[...] + _dot(o_ref[...], wo_ref[...])
        h = _rms(x3, g_ref[...])
        logits = jnp.dot(h, router_ref[...], preferred_element_type=F32,
                         precision=lax.Precision.HIGHEST)
        logits = jnp.where(lane < n_exp, logits, -jnp.inf)
        v1 = jnp.max(logits, axis=-1, keepdims=True)
        i1 = jnp.min(jnp.where(logits == v1, lane, float(LANES)), axis=-1, keepdims=True)
        rest = jnp.where(lane == i1, -jnp.inf, logits)
        v2 = jnp.max(rest, axis=-1, keepdims=True)
        i2 = jnp.min(jnp.where(rest == v2, lane, float(LANES)), axis=-1, keepdims=True)
        e2 = jnp.exp(v2 - v1)
        denom = 1.0 + e2
        comb_scr[...] = jnp.where(lane == i1, 1.0 / denom, 0.0) + jnp.where(lane == i2, e2 / denom, 0.0)
        x3_scr[...] = x3
        h_scr[...] = h.astype(BF16)
        acc_scr[...] = jnp.zeros(acc_scr.shape, F32)

    h = h_scr[...]
    act = (_silu(_dot(h, wg_ref[0])) * _dot(h, wu_ref[0])).astype(BF16)
    y = _dot(act, wd_ref[0])
    expert = (s // n_fc).astype(F32)
    gate = jnp.sum(jnp.where(lane == expert, comb_scr[...], 0.0), axis=-1, keepdims=True)
    acc_scr[...] += gate * y

    @pl.when(s == pl.num_programs(1) - 1)
    def _():
        out_ref[...] = _rms(x3_scr[...] + acc_scr[...], gfin_ref[...])


def _out_moe(x, o, wo_bf, g, router_pad, n_exp, wgu_bf, wd_bf, g_final):
    n, d = x.shape
    f = wd_bf.shape[1]
    tm = _row_tile(n, 512)
    fc = 512 if f % 512 == 0 else f
    n_fc = f // fc
    row = lambda w: pl.BlockSpec((tm, w), lambda t, s: (t, 0))
    return pl.pallas_call(
        functools.partial(_moe_kernel, n_exp=n_exp, n_fc=n_fc),
        grid=(n // tm, n_exp * n_fc),
        in_specs=[row(d), row(o.shape[1]), _const_spec(wo_bf.shape), _const_spec((1, d)),
                  _const_spec(router_pad.shape),
                  pl.BlockSpec((1, d, fc), lambda t, s: (s // n_fc, 0, s % n_fc)),
                  pl.BlockSpec((1, d, fc), lambda t, s: (s // n_fc, 0, n_fc + s % n_fc)),
                  pl.BlockSpec((1, fc, d), lambda t, s: (s // n_fc, s % n_fc, 0)),
                  _const_spec((1, d))],
        out_specs=row(d),
        out_shape=jax.ShapeDtypeStruct((n, d), F32),
        scratch_shapes=[pltpu.VMEM((tm, d), F32), pltpu.VMEM((tm, d), BF16),
                        pltpu.VMEM((tm, LANES), F32), pltpu.VMEM((tm, d), F32)],
        compiler_params=_params("parallel", "arbitrary"),
        name="attn_out_moe",
    )(x, o, wo_bf, g, router_pad, wgu_bf, wgu_bf, wd_bf, g_final)


def _lane_block(x, start):
    return jnp.pad(x, ((0, 0), (start, LANES - start - x.shape[1])))


def _swap_halves(x):
    half = x.shape[-1] // 2
    return jnp.concatenate([x[..., half:], x[..., :half]], axis=-1)


def _prep_mla_weights(w_a, g_q, g_kv, w_uq, w_ukv, q_lora, kv_lora, n_heads, nope, rope):
    w_pe = w_a[:, q_lora + kv_lora:]
    w_pe_sw = _swap_halves(w_pe)
    wpe = jnp.concatenate([_lane_block(w_pe, 0), _lane_block(w_pe_sw, 0),
                           _lane_block(w_pe, nope), _lane_block(w_pe_sw, nope)], axis=1)
    uq = w_uq.reshape(q_lora, n_heads, nope + rope)
    uq_nope, uq_pe = uq[..., :nope], uq[..., nope:]
    tail = jnp.zeros((q_lora, n_heads, LANES - nope - rope), w_uq.dtype)
    wuq = jnp.concatenate([uq_nope, uq_pe, tail], axis=-1).reshape(q_lora, n_heads * LANES)
    wuqs = jnp.concatenate([jnp.zeros_like(uq_nope), _swap_halves(uq_pe), tail],
                           axis=-1).reshape(q_lora, n_heads * LANES)
    ukv = w_ukv.reshape(kv_lora, n_heads, -1)
    uk = ukv[..., :nope]
    wuk = jnp.concatenate([uk, jnp.zeros((kv_lora, n_heads, LANES - nope), uk.dtype)],
                          axis=-1).reshape(kv_lora, n_heads * LANES)
    wuv = ukv[..., nope:].reshape(kv_lora, -1)
    bf = lambda a: a.astype(BF16)
    return {"waq": bf(w_a[:, :q_lora]), "wakv": bf(w_a[:, q_lora:q_lora + kv_lora]), "wpe": bf(wpe),
            "gq": g_q[None, :], "gkv": g_kv[None, :], "wuq": bf(wuq), "wuqs": bf(wuqs),
            "wuk": bf(wuk), "wuv": bf(wuv)}


def _rope_tables(pos, nope, rope):
    half = rope // 2
    freqs = jnp.power(ROPE_THETA, -jnp.arange(half, dtype=F32) * 2.0 / rope)
    ang = pos.astype(F32)[:, None] * freqs[None, :]
    cos = jnp.concatenate([jnp.cos(ang)] * 2, axis=1)
    sin = jnp.concatenate([-jnp.sin(ang), jnp.sin(ang)], axis=1)
    c64 = _lane_block(cos, nope) + _lane_block(jnp.ones((pos.shape[0], nope), F32), 0)
    return _lane_block(cos, 0), _lane_block(sin, 0), c64, _lane_block(sin, nope)


def kernel(x_prompt, x_sample, cache_diff_k, cache_diff_v, cache_mla_ckv, cache_mla_kpe, norm_mix, norm_ffn, norm_final, diff_w_qkv, diff_lambda, diff_subln, diff_w_o, mla_w_a, mla_norm_q, mla_norm_kv, mla_w_uq, mla_w_ukv, mla_w_o, ffn_w_gu, ffn_w_down, moe_router, moe_w_gu, moe_w_down):
    b, t, d = x_prompt.shape
    db, nq, _ = x_sample.shape
    past = cache_diff_k.shape[2]
    n_da_heads, dv = cache_diff_k.shape[3], diff_subln.shape[-1]
    da_width = n_da_heads * dv
    q_lora, kv_lora = mla_norm_q.shape[-1], mla_norm_kv.shape[-1]
    rope = mla_w_a.shape[-1] - q_lora - kv_lora
    mla_width = mla_w_o.shape[1]
    n_exp = moe_router.shape[-1]
    assert norm_mix.shape[0] == 2, "one differential-attention layer followed by one MLA layer"

    uq_cols, ukv_cols = mla_w_uq.shape[-1], mla_w_ukv.shape[-1]
    n_mla_heads = (uq_cols - (ukv_cols - mla_width)) // rope
    nope = uq_cols // n_mla_heads - rope
    mla_v = mla_width // n_mla_heads
    assert n_mla_heads * (nope + mla_v) == ukv_cols and n_mla_heads % 2 == 0
    assert nope + rope <= LANES and 2 * mla_v == LANES

    bf = lambda a: a.astype(BF16)
    lam_init = 0.8 - 0.6 * math.exp(-0.3 * 0)
    da_scale = (dv // 2) ** -0.5
    mla_scale = (nope + rope) ** -0.5
    slopes = jnp.exp2(-8.0 * jnp.arange(1, n_da_heads + 1, dtype=F32) / n_da_heads)
    slope_rows = jnp.repeat(slopes, 2 * nq)[:, None]

    w_qkv, w_o0 = bf(diff_w_qkv[0]), bf(diff_w_o[0])
    w_gu0, w_d0 = bf(ffn_w_gu[0]), bf(ffn_w_down[0])
    mla_w = _prep_mla_weights(mla_w_a[0], mla_norm_q[0], mla_norm_kv[0], mla_w_uq[0], mla_w_ukv[0],
                              q_lora, kv_lora, n_mla_heads, nope, rope)
    w_o1 = bf(mla_w_o[0])
    router_pad = jnp.pad(moe_router[0], ((0, 0), (0, LANES - n_exp)))
    moe_gu, moe_d = bf(moe_w_gu[0]), bf(moe_w_down[0])
    g_mix, g_ffn, g_fin = norm_mix[:, None, :], norm_ffn[:, None, :], norm_final[None, :]
    lam, subln = diff_lambda[0], diff_subln[0][None, :]

    def trunk(x, pos, attn0, attn1):
        bx, tx, _ = x.shape
        x2d = x.reshape(bx * tx, d)
        q, k, v, k_bf, vt_bf = _qkv_proj(x, g_mix[0], w_qkv, da_scale)
        o = attn0(q, k, v, k_bf, vt_bf)
        x2 = _out_ffn(x2d, o.reshape(bx * tx, da_width), w_o0, g_ffn[0], w_gu0, w_d0)
        ckv, kpe, pe64, qcat, kcat, vt1 = _mla_proj(x2.reshape(bx, tx, d), g_mix[1], mla_w,
                                                   _rope_tables(pos, nope, rope), rope, mla_scale)
        o = attn1(ckv, pe64, qcat, kcat, vt1)
        y = _out_moe(x2, o.reshape(bx * tx, mla_width), w_o1, g_ffn[1], router_pad, n_exp,
                     moe_gu, moe_d, g_fin)
        return y, k, v, ckv, kpe

    y_p, k_p, v_p, ckv_p, kpe_p = trunk(
        x_prompt, jnp.arange(t, dtype=jnp.int32),
        lambda q, k, v, k_bf, vt_bf: _diff_attn_prompt(q, k_bf, vt_bf, lam, subln, slopes, lam_init),
        lambda ckv, pe64, qcat, kcat, vt1: _mla_attn_prompt(qcat, kcat, vt1, mla_v))

    n_s = db * nq
    cache_k = cache_diff_k[0].reshape(db, past, da_width)
    cache_v = cache_diff_v[0].reshape(db, past, da_width)
    cache_pe = jnp.pad(cache_mla_kpe[0], ((0, 0), (0, 0), (nope, LANES - nope - rope)))
    pos_s = past + jnp.tile(jnp.arange(nq, dtype=jnp.int32), db)

    def attn0_s(q, k, v, k_bf, vt_bf):
        r = lambda a: a.reshape(db, nq, da_width)
        return _diff_attn_decode(r(q), cache_k, cache_v, r(k), r(v), lam, subln, slope_rows, lam_init)

    def attn1_s(ckv, pe64, qcat, kcat, vt1):
        return _mla_attn_decode(qcat.reshape(db, nq, -1), cache_mla_ckv[0], cache_pe,
                                ckv.reshape(db, nq, kv_lora), pe64.reshape(db, nq, LANES),
                                mla_w["wuk"], mla_w["wuv"])

    y_s, k_s, v_s, ckv_s, kpe_s = trunk(x_sample.reshape(1, n_s, d), pos_s, attn0_s, attn1_s)

    return (y_p.reshape(b, t, d), y_s.reshape(db, nq, d),
            k_p.reshape(1, b, t, n_da_heads, dv), v_p.reshape(1, b, t, n_da_heads, dv),
            ckv_p.reshape(1, b, t, kv_lora), kpe_p.reshape(1, b, t, rope),
            k_s.reshape(1, db, nq, n_da_heads, dv), v_s.reshape(1, db, nq, n_da_heads, dv),
            ckv_s.reshape(1, db, nq, kv_lora), kpe_s.reshape(1, db, nq, rope))
```

```python
import functools
import math

import jax
import jax.numpy as jnp
from jax import lax
from jax.experimental import pallas as pl
from jax.experimental.pallas import tpu as pltpu

NORM_EPS = 1e-6
NEG_INF = -1e30
CHUNK = 64
ROPE_THETA = 10000.0
LOG2E = math.log2(math.e)
LANES = 128
MXU_DIM = 256
BF16_ROWS = 16
VMEM_LIMIT_BYTES = 56 * 2**20
BF16 = jnp.bfloat16
F32 = jnp.float32


def _params(*sem):
    return pltpu.CompilerParams(dimension_semantics=sem, vmem_limit_bytes=VMEM_LIMIT_BYTES)


def _const_spec(shape):
    nd = len(shape)
    return pl.BlockSpec(shape, lambda *_: (0,) * nd, pipeline_mode=pl.Buffered(1))


def _rms(x, g):
    ms = jnp.mean(x * x, axis=-1, keepdims=True)
    return x * lax.rsqrt(ms + NORM_EPS) * g


def _dot(a, b):
    return jnp.dot(a, b, preferred_element_type=F32)


def _dot_nt(a, b):
    return lax.dot_general(a, b, (((1,), (1,)), ((), ())), preferred_element_type=F32)


def _row_tile(t, cap):
    tm = min(t, cap)
    assert t % tm == 0
    return tm


def _ones_row_block(n_cols):
    sub = lax.broadcasted_iota(jnp.int32, (BF16_ROWS, n_cols), 0)
    return jnp.where(sub == 0, 1.0, 0.0).astype(BF16)


def _qkv_kernel(x_ref, g_ref, w_ref, q_ref, k_ref, v_ref, kx_ref, vtx_ref, *, width, dv, scale):
    tm = x_ref.shape[1]
    h = _rms(x_ref[0], g_ref[...]).astype(BF16)
    qkv = _dot(h, w_ref[...])
    q_ref[0] = (qkv[:, :width] * scale).astype(BF16)
    k = qkv[:, width:2 * width]
    v = qkv[:, 2 * width:]
    k_ref[0] = k
    v_ref[0] = v
    k_bf = k.astype(BF16)
    vt = v.T.astype(BF16)
    pos = lax.broadcasted_iota(jnp.int32, (tm, LANES), 0) + pl.program_id(1) * tm
    lane = lax.broadcasted_iota(jnp.int32, (tm, LANES), 1)
    pos_feat = jnp.where(lane < 3, pos // LANES, jnp.where(lane < 6, pos % LANES, 0))
    pos_feat = pos_feat.astype(F32).astype(BF16)
    ones = _ones_row_block(tm)
    for hh in range(width // dv):
        kx_ref[0, :, 2 * dv * hh:2 * dv * hh + dv] = k_bf[:, hh * dv:(hh + 1) * dv]
        kx_ref[0, :, 2 * dv * hh + dv:2 * dv * (hh + 1)] = pos_feat
        r0 = hh * (dv + BF16_ROWS)
        vtx_ref[0, r0:r0 + dv, :] = vt[hh * dv:(hh + 1) * dv, :]
        vtx_ref[0, r0 + dv:r0 + dv + BF16_ROWS, :] = ones


def _qkv_proj(x, g, w_bf, dv, scale):
    bx, tx, d = x.shape
    width = w_bf.shape[1] // 3
    n_heads = width // dv
    assert dv == LANES and tx <= LANES * MXU_DIM
    tm = _row_tile(tx, 512)
    row = lambda n: pl.BlockSpec((1, tm, n), lambda b, t: (b, t, 0))
    vrows = n_heads * (dv + BF16_ROWS)
    return pl.pallas_call(
        functools.partial(_qkv_kernel, width=width, dv=dv, scale=scale),
        grid=(bx, tx // tm),
        in_specs=[row(d), _const_spec((1, d)), _const_spec(w_bf.shape)],
        out_specs=[row(width), row(width), row(width), row(2 * width),
                   pl.BlockSpec((1, vrows, tm), lambda b, t: (b, 0, t))],
        out_shape=[jax.ShapeDtypeStruct((bx, tx, width), BF16),
                   jax.ShapeDtypeStruct((bx, tx, width), F32),
                   jax.ShapeDtypeStruct((bx, tx, width), F32),
                   jax.ShapeDtypeStruct((bx, tx, 2 * width), BF16),
                   jax.ShapeDtypeStruct((bx, vrows, tx), BF16)],
        compiler_params=_params("parallel", "parallel"),
        name="qkv_proj",
    )(x, g, w_bf)


def _flash_streams(i, tq, key_op, query_op, value_op, diag_tile, z_bufs, cm_bufs, m, acc):
    n_streams = 2

    def produce(blk, z, cm, extra):
        off = pl.multiple_of(blk * tq, tq)
        for s in range(n_streams):
            k_blk = key_op(s, off)
            for n in range(tq // MXU_DIM):
                cols = slice(n * MXU_DIM, (n + 1) * MXU_DIM)
                scores = _dot_nt(k_blk, query_op(s, cols))
                if extra is not None:
                    scores = scores + extra[:, cols]
                z[s, :, cols] = scores
                cm[s, :, cols] = jnp.max(scores, axis=0, keepdims=True)

    def consume(blk, z, cm):
        off = pl.multiple_of(blk * tq, tq)
        for s in range(n_streams):
            m_old = m[s]
            m_new = jnp.maximum(m_old, cm[s])
            p = jnp.exp2(z[s] - m_new).astype(BF16)
            acc[s] = jnp.exp2(m_old - m_new) * acc[s] + _dot(value_op(s, off), p)
            m[s] = m_new

    m[...] = jnp.full(m.shape, NEG_INF, F32)
    acc[...] = jnp.zeros(acc.shape, F32)
    produce(i, z_bufs[0], cm_bufs[0], diag_tile)

    def body(u, carry):
        prev = jnp.where(u == 0, i, u - 1)
        for parity in range(2):
            @pl.when(u % 2 == parity)
            def _():
                produce(u, z_bufs[1 - parity], cm_bufs[1 - parity], None)
                consume(prev, z_bufs[parity], cm_bufs[parity])
        return carry

    lax.fori_loop(0, i, body, 0)
    last = jnp.maximum(i - 1, 0)
    for parity in range(2):
        @pl.when(i % 2 == parity)
        def _():
            consume(last, z_bufs[parity], cm_bufs[parity])


def _local_positions(tq):
    kl = lax.broadcasted_iota(jnp.int32, (tq, tq), 0)
    ql = lax.broadcasted_iota(jnp.int32, (tq, tq), 1)
    return kl, ql, (kl // CHUNK) <= (ql // CHUNK)


def _lambda_full(lam_ref, lam_init):
    lf = lam_ref[...]
    s01 = jnp.sum(lf[0:1, :] * lf[1:2, :], axis=-1, keepdims=True)
    s23 = jnp.sum(lf[2:3, :] * lf[3:4, :], axis=-1, keepdims=True)
    return jnp.exp(s01) - jnp.exp(s23) + lam_init


def _diff_flash_kernel(slope_ref, q_ref, kx_ref, vtx_ref, lam_ref, subln_ref, o_ref,
                       qx, z_a, z_b, cm_a, cm_b, m, acc, *, tq, dv, lam_init):
    h = pl.program_id(1)
    i = pl.program_id(2)
    s_hi, s_mid, s_lo, slope2 = (slope_ref[4 * h + n] for n in range(4))
    q = q_ref[0]
    lane = lax.broadcasted_iota(jnp.int32, q.shape, 1)
    zero = jnp.zeros_like(q)
    slope_feat = jnp.zeros(q.shape, F32)
    for n, val in enumerate((LANES * s_hi, LANES * s_mid, LANES * s_lo, s_hi, s_mid, s_lo)):
        slope_feat = jnp.where(lane == n, val, slope_feat)
    slope_feat = slope_feat.astype(BF16)
    qx[0, :, 0:dv] = jnp.where(lane < dv // 2, q, zero)
    qx[1, :, 0:dv] = jnp.where(lane >= dv // 2, q, zero)
    qx[0, :, dv:2 * dv] = slope_feat
    qx[1, :, dv:2 * dv] = slope_feat

    kl, ql, visible = _local_positions(tq)
    diag_tile = jnp.where(visible, (-2.0 * slope2) * jnp.maximum(kl - ql, 0).astype(F32), NEG_INF)

    _flash_streams(i, tq,
                   key_op=lambda s, off: kx_ref[0, pl.ds(off, tq), :],
                   query_op=lambda s, cols: qx[s, cols, :],
                   value_op=lambda s, off: vtx_ref[0, :, pl.ds(off, tq)],
                   diag_tile=diag_tile, z_bufs=(z_a, z_b), cm_bufs=(cm_a, cm_b), m=m, acc=acc)

    lam = _lambda_full(lam_ref, lam_init)
    o_t = (acc[0, 0:dv, :] / acc[0, dv:dv + 1, :]
           - lam * (acc[1, 0:dv, :] / acc[1, dv:dv + 1, :]))
    o = _rms(o_t.T, subln_ref[...]) * (1.0 - lam_init)
    o_ref[0] = o.astype(BF16)


def _flash_scratch(tq, q_feat, acc_rows):
    return ([pltpu.VMEM((2, tq, q_feat), BF16)] if q_feat else []) + [
        pltpu.VMEM((2, tq, tq), F32), pltpu.VMEM((2, tq, tq), F32),
        pltpu.VMEM((2, 1, tq), F32), pltpu.VMEM((2, 1, tq), F32),
        pltpu.VMEM((2, 1, tq), F32), pltpu.VMEM((2, acc_rows, tq), F32)]


def _diff_attn_prompt(q, kx, vtx, lam, subln, slope_parts, lam_init):
    b, t, width = q.shape
    dv = subln.shape[-1]
    n_heads = width // dv
    tq = _row_tile(t, 512)
    assert tq % MXU_DIM == 0
    vrows = dv + BF16_ROWS
    return pl.pallas_call(
        functools.partial(_diff_flash_kernel, tq=tq, dv=dv, lam_init=lam_init),
        grid=(b, n_heads, t // tq),
        in_specs=[pl.BlockSpec(memory_space=pltpu.SMEM),
                  pl.BlockSpec((1, tq, dv), lambda b_, h, i: (b_, i, h)),
                  pl.BlockSpec((1, t, 2 * dv), lambda b_, h, i: (b_, 0, h)),
                  pl.BlockSpec((1, vrows, t), lambda b_, h, i: (b_, h, 0)),
                  pl.BlockSpec(lam.shape, lambda b_, h, i: (0, 0)),
                  pl.BlockSpec(subln.shape, lambda b_, h, i: (0, 0))],
        out_specs=pl.BlockSpec((1, tq, dv), lambda b_, h, i: (b_, i, h)),
        out_shape=jax.ShapeDtypeStruct((b, t, width), BF16),
        scratch_shapes=_flash_scratch(tq, 2 * dv, vrows),
        compiler_params=_params("parallel", "parallel", "arbitrary"),
        name="diff_attn_prompt",
    )(slope_parts, q, kx, vtx, lam, subln)


def _mla_flash_kernel(q_ref, k_ref, vtx_ref, o_ref, z_a, z_b, cm_a, cm_b, m, acc, *, tq, dv):
    i = pl.program_id(2)
    vrows = dv + BF16_ROWS
    _, _, visible = _local_positions(tq)
    _flash_streams(i, tq,
                   key_op=lambda s, off: k_ref[0, pl.ds(off, tq), s * LANES:(s + 1) * LANES],
                   query_op=lambda s, cols: q_ref[0, cols, s * LANES:(s + 1) * LANES],
                   value_op=lambda s, off: vtx_ref[0, s * vrows:(s + 1) * vrows, pl.ds(off, tq)],
                   diag_tile=jnp.where(visible, 0.0, NEG_INF),
                   z_bufs=(z_a, z_b), cm_bufs=(cm_a, cm_b), m=m, acc=acc)
    o_t = jnp.concatenate([acc[s, 0:dv, :] / acc[s, dv:dv + 1, :] for s in range(2)], axis=0)
    o_ref[0] = o_t.T.astype(BF16)


def _mla_attn_prompt(qcat, kcat, vtx, dv):
    b, t, wq = qcat.shape
    n_pairs = wq // (2 * LANES)
    tq = _row_tile(t, 512)
    assert tq % MXU_DIM == 0
    vrows = dv + BF16_ROWS
    return pl.pallas_call(
        functools.partial(_mla_flash_kernel, tq=tq, dv=dv),
        grid=(b, n_pairs, t // tq),
        in_specs=[pl.BlockSpec((1, tq, 2 * LANES), lambda b_, h, i: (b_, i, h)),
                  pl.BlockSpec((1, t, 2 * LANES), lambda b_, h, i: (b_, 0, h)),
                  pl.BlockSpec((1, 2 * vrows, t), lambda b_, h, i: (b_, h, 0))],
        out_specs=pl.BlockSpec((1, tq, 2 * dv), lambda b_, h, i: (b_, i, h)),
        out_shape=jax.ShapeDtypeStruct((b, t, n_pairs * 2 * dv), BF16),
        scratch_shapes=_flash_scratch(tq, 0, vrows),
        compiler_params=_params("parallel", "parallel", "arbitrary"),
        name="mla_attn_prompt",
    )(qcat, kcat, vtx)


def _block_diag_queries(q, n_groups, group_width):
    qt = jnp.concatenate([q] * n_groups, axis=0)
    r = lax.broadcasted_iota(jnp.int32, qt.shape, 0)
    c = lax.broadcasted_iota(jnp.int32, qt.shape, 1)
    return jnp.where((c // group_width) == (r // q.shape[0]), qt, jnp.zeros_like(qt))


def _decode_update(z, v_bf, m_ref, l_ref, acc_ref):
    m_old = m_ref[...]
    m_new = jnp.maximum(m_old, jnp.max(z, axis=-1, keepdims=True))
    alpha = jnp.exp2(m_old - m_new)
    p = jnp.exp2(z - m_new)
    l_ref[...] = alpha * l_ref[...] + jnp.sum(p, axis=-1, keepdims=True)
    acc_ref[...] = alpha * acc_ref[...] + _dot(p.astype(BF16), v_bf)
    m_ref[...] = m_new


def _decode_positions(n_rows, n_keys, nq, past, key_start, n_valid):
    r = lax.broadcasted_iota(jnp.int32, (n_rows, n_keys), 0)
    c = lax.broadcasted_iota(jnp.int32, (n_rows, n_keys), 1)
    q_pos = past + r % nq
    k_pos = key_start + c
    visible = ((k_pos // CHUNK) <= (q_pos // CHUNK)) & (c < n_valid)
    return q_pos, k_pos, visible


def _pad_rows(x, n):
    return jnp.concatenate([x, jnp.zeros((n - x.shape[0], x.shape[1]), x.dtype)], axis=0)


def _diff_decode_kernel(slope_ref, q_ref, ck_ref, cv_ref, kn_ref, vn_ref, lam_ref, subln_ref,
                        o_ref, qbd, m, l, acc, *, past, tk, nq, dv, lam_init):
    j = pl.program_id(1)
    n_rows = qbd.shape[0]

    @pl.when(j == 0)
    def _():
        qbd[...] = _block_diag_queries(q_ref[0], n_rows // nq, dv // 2)
        m[...] = jnp.full(m.shape, NEG_INF, F32)
        l[...] = jnp.zeros(l.shape, F32)
        acc[...] = jnp.zeros(acc.shape, F32)

    def step(k_f32, v_f32, key_start, n_valid):
        n_keys = k_f32.shape[0]
        s = _dot_nt(qbd[...], k_f32.astype(BF16))
        q_pos, k_pos, visible = _decode_positions(n_rows, n_keys, nq, past, key_start, n_valid)
        dist = jnp.abs(q_pos - k_pos).astype(F32)
        z = jnp.where(visible, s - slope_ref[...] * dist, NEG_INF)
        _decode_update(z, v_f32.astype(BF16), m, l, acc)

    step(ck_ref[0], cv_ref[0], j * tk, tk)

    @pl.when(j == pl.num_programs(1) - 1)
    def _():
        step(_pad_rows(kn_ref[0], LANES), _pad_rows(vn_ref[0], LANES), past, nq)
        lam = _lambda_full(lam_ref, lam_init)
        for hh in range(n_rows // (2 * nq)):
            r0 = hh * 2 * nq
            cols = slice(hh * dv, (hh + 1) * dv)
            o1 = acc[r0:r0 + nq, cols] / l[r0:r0 + nq, :]
            o2 = acc[r0 + nq:r0 + 2 * nq, cols] / l[r0 + nq:r0 + 2 * nq, :]
            o = _rms(o1 - lam * o2, subln_ref[...]) * (1.0 - lam_init)
            o_ref[0, :, cols] = o.astype(BF16)


def _diff_attn_decode(q, cache_k, cache_v, k_new, v_new, lam, subln, slope_rows, lam_init):
    db, nq, width = q.shape
    past = cache_k.shape[1]
    dv = subln.shape[-1]
    n_rows = (width // dv) * 2 * nq
    tk = _row_tile(past, 512)
    stream = lambda n: pl.BlockSpec((1, nq, n), lambda b, j: (b, 0, 0))
    cache = pl.BlockSpec((1, tk, width), lambda b, j: (b, j, 0))
    return pl.pallas_call(
        functools.partial(_diff_decode_kernel, past=past, tk=tk, nq=nq, dv=dv, lam_init=lam_init),
        grid=(db, past // tk),
        in_specs=[_const_spec(slope_rows.shape), stream(width), cache, cache, stream(width),
                  stream(width), _const_spec(lam.shape), _const_spec(subln.shape)],
        out_specs=stream(width),
        out_shape=jax.ShapeDtypeStruct((db, nq, width), BF16),
        scratch_shapes=[pltpu.VMEM((n_rows, width), BF16), pltpu.VMEM((n_rows, 1), F32),
                        pltpu.VMEM((n_rows, 1), F32), pltpu.VMEM((n_rows, width), F32)],
        compiler_params=_params("parallel", "arbitrary"),
        name="diff_attn_decode",
    )(slope_rows, q, cache_k, cache_v, k_new, v_new, lam, subln)


def _mla_decode_kernel(q_ref, cc_ref, cpe_ref, cn_ref, pen_ref, wuk_ref, wuv_ref, o_ref,
                       qbd, m, l, acc, *, past, tk, nq, dv):
    j = pl.program_id(1)
    n_rows = qbd.shape[0]
    n_heads = n_rows // nq

    @pl.when(j == 0)
    def _():
        qbd[...] = _block_diag_queries(q_ref[0], n_heads, LANES)
        m[...] = jnp.full(m.shape, NEG_INF, F32)
        l[...] = jnp.zeros(l.shape, F32)
        acc[...] = jnp.zeros(acc.shape, F32)

    def step(c_f32, pe_f32, key_start, n_valid):
        n_keys = c_f32.shape[0]
        c_bf = c_f32.astype(BF16)
        kcat = (_dot(c_bf, wuk_ref[...]) + jnp.concatenate([pe_f32] * n_heads, axis=1)).astype(BF16)
        v_bf = _dot(c_bf, wuv_ref[...]).astype(BF16)
        s = _dot_nt(qbd[...], kcat)
        _, _, visible = _decode_positions(n_rows, n_keys, nq, past, key_start, n_valid)
        _decode_update(jnp.where(visible, s, NEG_INF), v_bf, m, l, acc)

    step(cc_ref[0], cpe_ref[0], j * tk, tk)

    @pl.when(j == pl.num_programs(1) - 1)
    def _():
        step(_pad_rows(cn_ref[0], LANES), _pad_rows(pen_ref[0], LANES), past, nq)
        lane = lax.broadcasted_iota(jnp.int32, (nq, 2 * dv), 1)
        for pair in range(n_heads // 2):
            r0 = pair * 2 * nq
            cols = slice(pair * 2 * dv, (pair + 1) * 2 * dv)
            even = acc[r0:r0 + nq, cols] / l[r0:r0 + nq, :]
            odd = acc[r0 + nq:r0 + 2 * nq, cols] / l[r0 + nq:r0 + 2 * nq, :]
            o_ref[0, :, cols] = jnp.where(lane < dv, even, odd).astype(BF16)


def _mla_attn_decode(qcat, cache_c, cache_pe, c_new, pe_new, wuk_bf, wuv_bf):
    db, nq, wq = qcat.shape
    past, lora = cache_c.shape[1], cache_c.shape[2]
    n_heads = wq // LANES
    wv = wuv_bf.shape[1]
    dv = wv // n_heads
    n_rows = n_heads * nq
    tk = _row_tile(past, 512)
    stream = lambda n: pl.BlockSpec((1, nq, n), lambda b, j: (b, 0, 0))
    return pl.pallas_call(
        functools.partial(_mla_decode_kernel, past=past, tk=tk, nq=nq, dv=dv),
        grid=(db, past // tk),
        in_specs=[stream(wq),
                  pl.BlockSpec((1, tk, lora), lambda b, j: (b, j, 0)),
                  pl.BlockSpec((1, tk, LANES), lambda b, j: (b, j, 0)),
                  stream(lora), stream(LANES),
                  _const_spec(wuk_bf.shape), _const_spec(wuv_bf.shape)],
        out_specs=stream(wv),
        out_shape=jax.ShapeDtypeStruct((db, nq, wv), BF16),
        scratch_shapes=[pltpu.VMEM((n_rows, wq), BF16), pltpu.VMEM((n_rows, 1), F32),
                        pltpu.VMEM((n_rows, 1), F32), pltpu.VMEM((n_rows, wv), F32)],
        compiler_params=_params("parallel", "arbitrary"),
        name="mla_attn_decode",
    )(qcat, cache_c, cache_pe, c_new, pe_new, wuk_bf, wuv_bf)


def _silu(x):
    return x * jax.nn.sigmoid(x)


def _out_ffn_kernel(x_ref, o_ref, wo_ref, g_ref, wgu_ref, wd_ref, out_ref, *, f, n_chunks):
    x1 = x_ref[...] + _dot(o_ref[...], wo_ref[...])
    h = _rms(x1, g_ref[...]).astype(BF16)
    fc = f // n_chunks
    acc = x1
    for c in range(n_chunks):
        gate = _dot(h, wgu_ref[:, c * fc:(c + 1) * fc])
        up = _dot(h, wgu_ref[:, f + c * fc:f + (c + 1) * fc])
        acc = acc + _dot((_silu(gate) * up).astype(BF16), wd_ref[c * fc:(c + 1) * fc, :])
    out_ref[...] = acc


def _out_ffn(x, o, wo_bf, g, wgu_bf, wd_bf):
    n, d = x.shape
    f = wd_bf.shape[0]
    tm = _row_tile(n, 512)
    n_chunks = 2 if f % (2 * LANES) == 0 else 1
    row = lambda w: pl.BlockSpec((tm, w), lambda t: (t, 0))
    return pl.pallas_call(
        functools.partial(_out_ffn_kernel, f=f, n_chunks=n_chunks),
        grid=(n // tm,),
        in_specs=[row(d), row(o.shape[1]), _const_spec(wo_bf.shape), _const_spec((1, d)),
                  _const_spec(wgu_bf.shape), _const_spec(wd_bf.shape)],
        out_specs=row(d),
        out_shape=jax.ShapeDtypeStruct((n, d), F32),
        compiler_params=_params("parallel"),
        name="attn_out_ffn",
    )(x, o, wo_bf, g, wgu_bf, wd_bf)


def _mla_proj_kernel(x_ref, g_ref, waq_ref, wakv_ref, wpe_ref, gq_ref, gkv_ref, wuq_ref, wuqs_ref,
                     wuk_ref, wuv_ref, c0_ref, s0_ref, c64_ref, s64_ref,
                     ckv_ref, kpe_ref, pe64_ref, q_ref, k_ref, vtx_ref, *, rope, dv, scale):
    tm = x_ref.shape[1]
    h = _rms(x_ref[0], g_ref[...]).astype(BF16)
    c_q = _rms(_dot(h, waq_ref[...]), gq_ref[...]).astype(BF16)
    c_kv = _rms(_dot(h, wakv_ref[...]), gkv_ref[...])
    ckv_ref[0] = c_kv
    c_kv_bf = c_kv.astype(BF16)
    pe = _dot(h, wpe_ref[...])
    kpe0 = pe[:, 0:LANES] * c0_ref[...] + pe[:, LANES:2 * LANES] * s0_ref[...]
    kpe_ref[0] = kpe0[:, :rope]
    c64 = c64_ref[...]
    s64 = s64_ref[...]
    pe64 = pe[:, 2 * LANES:3 * LANES] * c64 + pe[:, 3 * LANES:] * s64
    pe64_ref[0] = pe64
    q = _dot(c_q, wuq_ref[...])
    q_sw = _dot(c_q, wuqs_ref[...])
    k = _dot(c_kv_bf, wuk_ref[...])
    for hh in range(q.shape[1] // LANES):
        cols = slice(hh * LANES, (hh + 1) * LANES)
        q_ref[0, :, cols] = ((q[:, cols] * c64 + q_sw[:, cols] * s64) * scale).astype(BF16)
        k_ref[0, :, cols] = (k[:, cols] + pe64).astype(BF16)
    vt = _dot(c_kv_bf, wuv_ref[...]).T.astype(BF16)
    ones = _ones_row_block(tm)
    for hh in range(vt.shape[0] // dv):
        r0 = hh * (dv + BF16_ROWS)
        vtx_ref[0, r0:r0 + dv, :] = vt[hh * dv:(hh + 1) * dv, :]
        vtx_ref[0, r0 + dv:r0 + dv + BF16_ROWS, :] = ones


def _mla_proj(x, g, w, tabs, rope, dv, scale):
    bx, tx, d = x.shape
    tm = _row_tile(tx, 256)
    lora = w["wakv"].shape[1]
    wq, wv = w["wuq"].shape[1], w["wuv"].shape[1]
    vrows = (wv // dv) * (dv + BF16_ROWS)
    row = lambda n: pl.BlockSpec((1, tm, n), lambda b, t: (b, t, 0))
    tab = pl.BlockSpec((tm, LANES), lambda b, t: (t, 0))
    weights = [w[k] for k in ("waq", "wakv", "wpe", "gq", "gkv", "wuq", "wuqs", "wuk", "wuv")]
    return pl.pallas_call(
        functools.partial(_mla_proj_kernel, rope=rope, dv=dv, scale=scale),
        grid=(bx, tx // tm),
        in_specs=[row(d), _const_spec((1, d))] + [_const_spec(a.shape) for a in weights] + [tab] * 4,
        out_specs=[row(lora), row(rope), row(LANES), row(wq), row(wq),
                   pl.BlockSpec((1, vrows, tm), lambda b, t: (b, 0, t))],
        out_shape=[jax.ShapeDtypeStruct((bx, tx, lora), F32),
                   jax.ShapeDtypeStruct((bx, tx, rope), F32),
                   jax.ShapeDtypeStruct((bx, tx, LANES), F32),
                   jax.ShapeDtypeStruct((bx, tx, wq), BF16),
                   jax.ShapeDtypeStruct((bx, tx, wq), BF16),
                   jax.ShapeDtypeStruct((bx, vrows, tx), BF16)],
        compiler_params=_params("parallel", "parallel"),
        name="mla_proj",
    )(x, g, *weights, *tabs)


def _moe_kernel(x_ref, o_ref, wo_ref, g_ref, router_ref, wg_ref, wu_ref, wd_ref, gfin_ref, out_ref,
                x3_scr, h_scr, comb_scr, acc_scr, *, n_exp, n_fc):
    s = pl.program_id(1)
    lane = lax.broadcasted_iota(jnp.int32, comb_scr.shape, 1).astype(F32)

    @pl.when(s == 0)
    def _():
        x3 = x_ref[...] + _dot(o_ref[...], wo_ref[...])
        h = _rms(x3, g_ref[...])
        logits = jnp.dot(h, router_ref[...], preferred_element_type=F32,
                         precision=lax.Precision.HIGHEST)
        logits = jnp.where(lane < n_exp, logits, -jnp.inf)
        v1 = jnp.max(logits, axis=-1, keepdims=True)
        i1 = jnp.min(jnp.where(logits == v1, lane, float(LANES)), axis=-1, keepdims=True)
        rest = jnp.where(lane == i1, -jnp.inf, logits)
        v2 = jnp.max(rest, axis=-1, keepdims=True)
        i2 = jnp.min(jnp.where(rest == v2, lane, float(LANES)), axis=-1, keepdims=True)
        e2 = jnp.exp(v2 - v1)
        denom = 1.0 + e2
        comb_scr[...] = jnp.where(lane == i1, 1.0 / denom, 0.0) + jnp.where(lane == i2, e2 / denom, 0.0)
        x3_scr[...] = x3
        h_scr[...] = h.astype(BF16)
        acc_scr[...] = jnp.zeros(acc_scr.shape, F32)

    h = h_scr[...]
    act = (_silu(_dot(h, wg_ref[0])) * _dot(h, wu_ref[0])).astype(BF16)
    y = _dot(act, wd_ref[0])
    expert = (s // n_fc).astype(F32)
    gate = jnp.sum(jnp.where(lane == expert, comb_scr[...], 0.0), axis=-1, keepdims=True)
    acc_scr[...] += gate * y

    @pl.when(s == pl.num_programs(1) - 1)
    def _():
        out_ref[...] = _rms(x3_scr[...] + acc_scr[...], gfin_ref[...])


def _out_moe(x, o, wo_bf, g, router_pad, n_exp, wgu_bf, wd_bf, g_final):
    n, d = x.shape
    f = wd_bf.shape[1]
    tm = _row_tile(n, 512)
    fc = 512 if f % 512 == 0 else f
    n_fc = f // fc
    row = lambda w: pl.BlockSpec((tm, w), lambda t, s: (t, 0))
    return pl.pallas_call(
        functools.partial(_moe_kernel, n_exp=n_exp, n_fc=n_fc),
        grid=(n // tm, n_exp * n_fc),
        in_specs=[row(d), row(o.shape[1]), _const_spec(wo_bf.shape), _const_spec((1, d)),
                  _const_spec(router_pad.shape),
                  pl.BlockSpec((1, d, fc), lambda t, s: (s // n_fc, 0, s % n_fc)),
                  pl.BlockSpec((1, d, fc), lambda t, s: (s // n_fc, 0, n_fc + s % n_fc)),
                  pl.BlockSpec((1, fc, d), lambda t, s: (s // n_fc, s % n_fc, 0)),
                  _const_spec((1, d))],
        out_specs=row(d),
        out_shape=jax.ShapeDtypeStruct((n, d), F32),
        scratch_shapes=[pltpu.VMEM((tm, d), F32), pltpu.VMEM((tm, d), BF16),
                        pltpu.VMEM((tm, LANES), F32), pltpu.VMEM((tm, d), F32)],
        compiler_params=_params("parallel", "arbitrary"),
        name="attn_out_moe",
    )(x, o, wo_bf, g, router_pad, wgu_bf, wgu_bf, wd_bf, g_final)


def _lane_block(x, start):
    return jnp.pad(x, ((0, 0), (start, LANES - start - x.shape[1])))


def _swap_halves(x):
    half = x.shape[-1] // 2
    return jnp.concatenate([x[..., half:], x[..., :half]], axis=-1)


def _prep_mla_weights(w_a, g_q, g_kv, w_uq, w_ukv, q_lora, kv_lora, n_heads, nope, rope):
    w_pe = w_a[:, q_lora + kv_lora:]
    w_pe_sw = _swap_halves(w_pe)
    wpe = jnp.concatenate([_lane_block(w_pe, 0), _lane_block(w_pe_sw, 0),
                           _lane_block(w_pe, nope), _lane_block(w_pe_sw, nope)], axis=1)
    uq = w_uq.reshape(q_lora, n_heads, nope + rope)
    uq_nope, uq_pe = uq[..., :nope], uq[..., nope:]
    tail = jnp.zeros((q_lora, n_heads, LANES - nope - rope), w_uq.dtype)
    wuq = jnp.concatenate([uq_nope, uq_pe, tail], axis=-1).reshape(q_lora, n_heads * LANES)
    wuqs = jnp.concatenate([jnp.zeros_like(uq_nope), _swap_halves(uq_pe), tail],
                           axis=-1).reshape(q_lora, n_heads * LANES)
    ukv = w_ukv.reshape(kv_lora, n_heads, -1)
    uk = ukv[..., :nope]
    wuk = jnp.concatenate([uk, jnp.zeros((kv_lora, n_heads, LANES - nope), uk.dtype)],
                          axis=-1).reshape(kv_lora, n_heads * LANES)
    wuv = ukv[..., nope:].reshape(kv_lora, -1)
    bf = lambda a: a.astype(BF16)
    return {"waq": bf(w_a[:, :q_lora]), "wakv": bf(w_a[:, q_lora:q_lora + kv_lora]), "wpe": bf(wpe),
            "gq": g_q[None, :], "gkv": g_kv[None, :], "wuq": bf(wuq), "wuqs": bf(wuqs),
            "wuk": bf(wuk), "wuv": bf(wuv)}


def _rope_tables(pos, nope, rope):
    half = rope // 2
    freqs = jnp.power(ROPE_THETA, -jnp.arange(half, dtype=F32) * 2.0 / rope)
    ang = pos.astype(F32)[:, None] * freqs[None, :]
    cos = jnp.concatenate([jnp.cos(ang)] * 2, axis=1)
    sin = jnp.concatenate([-jnp.sin(ang), jnp.sin(ang)], axis=1)
    c64 = _lane_block(cos, nope) + _lane_block(jnp.ones((pos.shape[0], nope), F32), 0)
    return _lane_block(cos, 0), _lane_block(sin, 0), c64, _lane_block(sin, nope)


def _bf16_split3(x):
    hi = x.astype(BF16).astype(F32)
    mid = (x - hi).astype(BF16).astype(F32)
    lo = (x - hi - mid).astype(BF16).astype(F32)
    return hi, mid, lo


def kernel(x_prompt, x_sample, cache_diff_k, cache_diff_v, cache_mla_ckv, cache_mla_kpe, norm_mix, norm_ffn, norm_final, diff_w_qkv, diff_lambda, diff_subln, diff_w_o, mla_w_a, mla_norm_q, mla_norm_kv, mla_w_uq, mla_w_ukv, mla_w_o, ffn_w_gu, ffn_w_down, moe_router, moe_w_gu, moe_w_down):
    b, t, d = x_prompt.shape
    db, nq, _ = x_sample.shape
    past = cache_diff_k.shape[2]
    n_da_heads, dv = cache_diff_k.shape[3], diff_subln.shape[-1]
    da_width = n_da_heads * dv
    q_lora, kv_lora = mla_norm_q.shape[-1], mla_norm_kv.shape[-1]
    rope = mla_w_a.shape[-1] - q_lora - kv_lora
    mla_width = mla_w_o.shape[1]
    n_exp = moe_router.shape[-1]
    assert norm_mix.shape[0] == 2, "one differential-attention layer followed by one MLA layer"

    uq_cols, ukv_cols = mla_w_uq.shape[-1], mla_w_ukv.shape[-1]
    n_mla_heads = (uq_cols - (ukv_cols - mla_width)) // rope
    nope = uq_cols // n_mla_heads - rope
    mla_v = mla_width // n_mla_heads
    assert n_mla_heads * (nope + mla_v) == ukv_cols and n_mla_heads % 2 == 0
    assert nope + rope <= LANES and 2 * mla_v == LANES

    bf = lambda a: a.astype(BF16)
    lam_init = 0.8 - 0.6 * math.exp(-0.3 * 0)
    da_scale = (dv // 2) ** -0.5 * LOG2E
    mla_scale = (nope + rope) ** -0.5 * LOG2E
    slopes2 = jnp.exp2(-8.0 * jnp.arange(1, n_da_heads + 1, dtype=F32) / n_da_heads) * LOG2E
    slope_parts = jnp.stack(_bf16_split3(slopes2) + (slopes2,), axis=1).reshape(-1)
    slope_rows = jnp.repeat(slopes2, 2 * nq)[:, None]

    w_qkv, w_o0 = bf(diff_w_qkv[0]), bf(diff_w_o[0])
    w_gu0, w_d0 = bf(ffn_w_gu[0]), bf(ffn_w_down[0])
    mla_w = _prep_mla_weights(mla_w_a[0], mla_norm_q[0], mla_norm_kv[0], mla_w_uq[0], mla_w_ukv[0],
                              q_lora, kv_lora, n_mla_heads, nope, rope)
    w_o1 = bf(mla_w_o[0])
    router_pad = jnp.pad(moe_router[0], ((0, 0), (0, LANES - n_exp)))
    moe_gu, moe_d = bf(moe_w_gu[0]), bf(moe_w_down[0])
    g_mix, g_ffn, g_fin = norm_mix[:, None, :], norm_ffn[:, None, :], norm_final[None, :]
    lam, subln = diff_lambda[0], diff_subln[0][None, :]

    def trunk(x, pos, attn0, attn1):
        bx, tx, _ = x.shape
        x2d = x.reshape(bx * tx, d)
        q, k, v, kx, vtx = _qkv_proj(x, g_mix[0], w_qkv, dv, da_scale)
        o = attn0(q, k, v, kx, vtx)
        x2 = _out_ffn(x2d, o.reshape(bx * tx, da_width), w_o0, g_ffn[0], w_gu0, w_d0)
        ckv, kpe, pe64, qcat, kcat, vtx1 = _mla_proj(x2.reshape(bx, tx, d), g_mix[1], mla_w,
                                                    _rope_tables(pos, nope, rope), rope, mla_v,
                                                    mla_scale)
        o = attn1(ckv, pe64, qcat, kcat, vtx1)
        y = _out_moe(x2, o.reshape(bx * tx, mla_width), w_o1, g_ffn[1], router_pad, n_exp,
                     moe_gu, moe_d, g_fin)
        return y, k, v, ckv, kpe

    y_p, k_p, v_p, ckv_p, kpe_p = trunk(
        x_prompt, jnp.arange(t, dtype=jnp.int32),
        lambda q, k, v, kx, vtx: _diff_attn_prompt(q, kx, vtx, lam, subln, slope_parts, lam_init),
        lambda ckv, pe64, qcat, kcat, vtx1: _mla_attn_prompt(qcat, kcat, vtx1, mla_v))

    n_s = db * nq
    cache_k = cache_diff_k[0].reshape(db, past, da_width)
    cache_v = cache_diff_v[0].reshape(db, past, da_width)
    cache_pe = jnp.pad(cache_mla_kpe[0], ((0, 0), (0, 0), (nope, LANES - nope - rope)))
    pos_s = past + jnp.tile(jnp.arange(nq, dtype=jnp.int32), db)

    def attn0_s(q, k, v, kx, vtx):
        r = lambda a: a.reshape(db, nq, da_width)
        return _diff_attn_decode(r(q), cache_k, cache_v, r(k), r(v), lam, subln, slope_rows, lam_init)

    def attn1_s(ckv, pe64, qcat, kcat, vtx1):
        return _mla_attn_decode(qcat.reshape(db, nq, -1), cache_mla_ckv[0], cache_pe,
                                ckv.reshape(db, nq, kv_lora), pe64.reshape(db, nq, LANES),
                                mla_w["wuk"], mla_w["wuv"])

    y_s, k_s, v_s, ckv_s, kpe_s = trunk(x_sample.reshape(1, n_s, d), pos_s, attn0_s, attn1_s)

    return (y_p.reshape(b, t, d), y_s.reshape(db, nq, d),
            k_p.reshape(1, b, t, n_da_heads, dv), v_p.reshape(1, b, t, n_da_heads, dv),
            ckv_p.reshape(1, b, t, kv_lora), kpe_p.reshape(1, b, t, rope),
            k_s.reshape(1, db, nq, n_da_heads, dv), v_s.reshape(1, db, nq, n_da_heads, dv),
            ckv_s.reshape(1, db, nq, kv_lora), kpe_s.reshape(1, db, nq, rope))
```

```python
import functools
import math

import jax
import jax.numpy as jnp
from jax import lax
from jax.experimental import pallas as pl
from jax.experimental.pallas import tpu as pltpu

NORM_EPS = 1e-6
NEG_INF = -1e30
CHUNK = 64
ROPE_THETA = 10000.0
LOG2E = math.log2(math.e)
LANES = 128
MXU_DIM = 256
BF16_ROWS = 16
VMEM_LIMIT_BYTES = 56 * 2**20
BF16 = jnp.bfloat16
F32 = jnp.float32


def _params(*sem):
    return pltpu.CompilerParams(dimension_semantics=sem, vmem_limit_bytes=VMEM_LIMIT_BYTES)


def _const_spec(shape):
    nd = len(shape)
    return pl.BlockSpec(shape, lambda *_: (0,) * nd, pipeline_mode=pl.Buffered(1))


def _rms(x, g):
    ms = jnp.mean(x * x, axis=-1, keepdims=True)
    return x * lax.rsqrt(ms + NORM_EPS) * g


def _dot(a, b):
    return jnp.dot(a, b, preferred_element_type=F32)


def _dot_nt(a, b):
    return lax.dot_general(a, b, (((1,), (1,)), ((), ())), preferred_element_type=F32)


def _row_tile(t, cap):
    tm = min(t, cap)
    assert t % tm == 0
    return tm


def _ones_row_block(n_cols):
    sub = lax.broadcasted_iota(jnp.int32, (BF16_ROWS, n_cols), 0)
    return jnp.where(sub == 0, 1.0, 0.0).astype(BF16)


def _qkv_kernel(x_ref, g_ref, w_ref, q_ref, k_ref, v_ref, kx_ref, vtx_ref, *, width, dv, scale):
    tm = x_ref.shape[1]
    h = _rms(x_ref[0], g_ref[...]).astype(BF16)
    qkv = _dot(h, w_ref[...])
    q_ref[0] = (qkv[:, :width] * scale).astype(BF16)
    k = qkv[:, width:2 * width]
    v = qkv[:, 2 * width:]
    k_ref[0] = k
    v_ref[0] = v
    k_bf = k.astype(BF16)
    vt = v.T.astype(BF16)
    pos = lax.broadcasted_iota(jnp.int32, (tm, LANES), 0) + pl.program_id(1) * tm
    lane = lax.broadcasted_iota(jnp.int32, (tm, LANES), 1)
    pos_feat = jnp.where(lane < 3, pos // LANES, jnp.where(lane < 6, pos % LANES, 0))
    pos_feat = pos_feat.astype(F32).astype(BF16)
    ones = _ones_row_block(tm)
    for hh in range(width // dv):
        kx_ref[0, :, 2 * dv * hh:2 * dv * hh + dv] = k_bf[:, hh * dv:(hh + 1) * dv]
        kx_ref[0, :, 2 * dv * hh + dv:2 * dv * (hh + 1)] = pos_feat
        r0 = hh * (dv + BF16_ROWS)
        vtx_ref[0, r0:r0 + dv, :] = vt[hh * dv:(hh + 1) * dv, :]
        vtx_ref[0, r0 + dv:r0 + dv + BF16_ROWS, :] = ones


def _qkv_proj(x, g, w_bf, dv, scale):
    bx, tx, d = x.shape
    width = w_bf.shape[1] // 3
    n_heads = width // dv
    assert dv == LANES and tx <= LANES * MXU_DIM
    tm = _row_tile(tx, 512)
    row = lambda n: pl.BlockSpec((1, tm, n), lambda b, t: (b, t, 0))
    vrows = n_heads * (dv + BF16_ROWS)
    return pl.pallas_call(
        functools.partial(_qkv_kernel, width=width, dv=dv, scale=scale),
        grid=(bx, tx // tm),
        in_specs=[row(d), _const_spec((1, d)), _const_spec(w_bf.shape)],
        out_specs=[row(width), row(width), row(width), row(2 * width),
                   pl.BlockSpec((1, vrows, tm), lambda b, t: (b, 0, t))],
        out_shape=[jax.ShapeDtypeStruct((bx, tx, width), BF16),
                   jax.ShapeDtypeStruct((bx, tx, width), F32),
                   jax.ShapeDtypeStruct((bx, tx, width), F32),
                   jax.ShapeDtypeStruct((bx, tx, 2 * width), BF16),
                   jax.ShapeDtypeStruct((bx, vrows, tx), BF16)],
        compiler_params=_params("parallel", "parallel"),
        name="qkv_proj",
    )(x, g, w_bf)


def _flash_streams(i, tq, key_op, query_op, value_op, diag_tile, z_bufs, cm_bufs, m, acc):
    n_streams = 2

    def produce(blk, z, cm, extra):
        off = pl.multiple_of(blk * tq, tq)
        for s in range(n_streams):
            k_blk = key_op(s, off)
            for n in range(tq // MXU_DIM):
                cols = slice(n * MXU_DIM, (n + 1) * MXU_DIM)
                scores = _dot_nt(k_blk, query_op(s, cols))
                if extra is not None:
                    scores = scores + extra[:, cols]
                z[s, :, cols] = scores
                cm[s, :, cols] = jnp.max(scores, axis=0, keepdims=True)

    def consume(blk, z, cm):
        off = pl.multiple_of(blk * tq, tq)
        for s in range(n_streams):
            m_old = m[s]
            m_new = jnp.maximum(m_old, cm[s])
            p = jnp.exp2(z[s] - m_new).astype(BF16)
            acc[s] = jnp.exp2(m_old - m_new) * acc[s] + _dot(value_op(s, off), p)
            m[s] = m_new

    m[...] = jnp.full(m.shape, NEG_INF, F32)
    acc[...] = jnp.zeros(acc.shape, F32)
    produce(i, z_bufs[0], cm_bufs[0], diag_tile)

    def body(u, carry):
        prev = jnp.where(u == 0, i, u - 1)
        for parity in range(2):
            @pl.when(u % 2 == parity)
            def _():
                produce(u, z_bufs[1 - parity], cm_bufs[1 - parity], None)
                consume(prev, z_bufs[parity], cm_bufs[parity])
        return carry

    lax.fori_loop(0, i, body, 0)
    last = jnp.maximum(i - 1, 0)
    for parity in range(2):
        @pl.when(i % 2 == parity)
        def _():
            consume(last, z_bufs[parity], cm_bufs[parity])


def _local_positions(tq):
    kl = lax.broadcasted_iota(jnp.int32, (tq, tq), 0)
    ql = lax.broadcasted_iota(jnp.int32, (tq, tq), 1)
    return kl, ql, (kl // CHUNK) <= (ql // CHUNK)


def _lambda_full(lam_ref, lam_init):
    lf = lam_ref[...]
    s01 = jnp.sum(lf[0:1, :] * lf[1:2, :], axis=-1, keepdims=True)
    s23 = jnp.sum(lf[2:3, :] * lf[3:4, :], axis=-1, keepdims=True)
    return jnp.exp(s01) - jnp.exp(s23) + lam_init


def _diff_flash_kernel(slope_ref, q_ref, kx_ref, vtx_ref, lam_ref, subln_ref, o_ref,
                       qx, z_a, z_b, cm_a, cm_b, m, acc, *, tq, dv, lam_init):
    h = pl.program_id(1)
    i = pl.program_id(2)
    s_hi, s_mid, s_lo, slope2 = (slope_ref[4 * h + n] for n in range(4))
    q = q_ref[0]
    lane = lax.broadcasted_iota(jnp.int32, q.shape, 1)
    zero = jnp.zeros_like(q)
    slope_feat = jnp.zeros(q.shape, F32)
    for n, val in enumerate((LANES * s_hi, LANES * s_mid, LANES * s_lo, s_hi, s_mid, s_lo)):
        slope_feat = jnp.where(lane == n, val, slope_feat)
    slope_feat = slope_feat.astype(BF16)
    qx[0, :, 0:dv] = jnp.where(lane < dv // 2, q, zero)
    qx[1, :, 0:dv] = jnp.where(lane >= dv // 2, q, zero)
    qx[0, :, dv:2 * dv] = slope_feat
    qx[1, :, dv:2 * dv] = slope_feat

    kl, ql, visible = _local_positions(tq)
    diag_tile = jnp.where(visible, (-2.0 * slope2) * jnp.maximum(kl - ql, 0).astype(F32), NEG_INF)

    _flash_streams(i, tq,
                   key_op=lambda s, off: kx_ref[0, pl.ds(off, tq), :],
                   query_op=lambda s, cols: qx[s, cols, :],
                   value_op=lambda s, off: vtx_ref[0, :, pl.ds(off, tq)],
                   diag_tile=diag_tile, z_bufs=(z_a, z_b), cm_bufs=(cm_a, cm_b), m=m, acc=acc)

    lam = _lambda_full(lam_ref, lam_init)
    o_t = (acc[0, 0:dv, :] / acc[0, dv:dv + 1, :]
           - lam * (acc[1, 0:dv, :] / acc[1, dv:dv + 1, :]))
    o = _rms(o_t.T, subln_ref[...]) * (1.0 - lam_init)
    o_ref[0] = o.astype(BF16)


def _flash_scratch(tq, q_feat, acc_rows):
    return ([pltpu.VMEM((2, tq, q_feat), BF16)] if q_feat else []) + [
        pltpu.VMEM((2, tq, tq), F32), pltpu.VMEM((2, tq, tq), F32),
        pltpu.VMEM((2, 1, tq), F32), pltpu.VMEM((2, 1, tq), F32),
        pltpu.VMEM((2, 1, tq), F32), pltpu.VMEM((2, acc_rows, tq), F32)]


def _diff_attn_prompt(q, kx, vtx, lam, subln, slope_parts, lam_init):
    b, t, width = q.shape
    dv = subln.shape[-1]
    n_heads = width // dv
    tq = _row_tile(t, 512)
    assert tq % MXU_DIM == 0
    vrows = dv + BF16_ROWS
    return pl.pallas_call(
        functools.partial(_diff_flash_kernel, tq=tq, dv=dv, lam_init=lam_init),
        grid=(b, n_heads, t // tq),
        in_specs=[pl.BlockSpec(memory_space=pltpu.SMEM),
                  pl.BlockSpec((1, tq, dv), lambda b_, h, i: (b_, i, h)),
                  pl.BlockSpec((1, t, 2 * dv), lambda b_, h, i: (b_, 0, h)),
                  pl.BlockSpec((1, vrows, t), lambda b_, h, i: (b_, h, 0)),
                  pl.BlockSpec(lam.shape, lambda b_, h, i: (0, 0)),
                  pl.BlockSpec(subln.shape, lambda b_, h, i: (0, 0))],
        out_specs=pl.BlockSpec((1, tq, dv), lambda b_, h, i: (b_, i, h)),
        out_shape=jax.ShapeDtypeStruct((b, t, width), BF16),
        scratch_shapes=_flash_scratch(tq, 2 * dv, vrows),
        compiler_params=_params("parallel", "parallel", "arbitrary"),
        name="diff_attn_prompt",
    )(slope_parts, q, kx, vtx, lam, subln)


def _mla_flash_kernel(q_ref, k_ref, vtx_ref, o_ref, z_a, z_b, cm_a, cm_b, m, acc, *, tq, dv):
    i = pl.program_id(2)
    vrows = dv + BF16_ROWS
    _, _, visible = _local_positions(tq)
    _flash_streams(i, tq,
                   key_op=lambda s, off: k_ref[0, pl.ds(off, tq), s * LANES:(s + 1) * LANES],
                   query_op=lambda s, cols: q_ref[0, cols, s * LANES:(s + 1) * LANES],
                   value_op=lambda s, off: vtx_ref[0, s * vrows:(s + 1) * vrows, pl.ds(off, tq)],
                   diag_tile=jnp.where(visible, 0.0, NEG_INF),
                   z_bufs=(z_a, z_b), cm_bufs=(cm_a, cm_b), m=m, acc=acc)
    o_t = jnp.concatenate([acc[s, 0:dv, :] / acc[s, dv:dv + 1, :] for s in range(2)], axis=0)
    o_ref[0] = o_t.T.astype(BF16)


def _mla_attn_prompt(qcat, kcat, vtx, dv):
    b, t, wq = qcat.shape
    n_pairs = wq // (2 * LANES)
    tq = _row_tile(t, 512)
    assert tq % MXU_DIM == 0
    vrows = dv + BF16_ROWS
    return pl.pallas_call(
        functools.partial(_mla_flash_kernel, tq=tq, dv=dv),
        grid=(b, n_pairs, t // tq),
        in_specs=[pl.BlockSpec((1, tq, 2 * LANES), lambda b_, h, i: (b_, i, h)),
                  pl.BlockSpec((1, t, 2 * LANES), lambda b_, h, i: (b_, 0, h)),
                  pl.BlockSpec((1, 2 * vrows, t), lambda b_, h, i: (b_, h, 0))],
        out_specs=pl.BlockSpec((1, tq, 2 * dv), lambda b_, h, i: (b_, i, h)),
        out_shape=jax.ShapeDtypeStruct((b, t, n_pairs * 2 * dv), BF16),
        scratch_shapes=_flash_scratch(tq, 0, vrows),
        compiler_params=_params("parallel", "parallel", "arbitrary"),
        name="mla_attn_prompt",
    )(qcat, kcat, vtx)


def _block_diag_queries(q, n_groups, group_width):
    qt = jnp.concatenate([q] * n_groups, axis=0)
    r = lax.broadcasted_iota(jnp.int32, qt.shape, 0)
    c = lax.broadcasted_iota(jnp.int32, qt.shape, 1)
    return jnp.where((c // group_width) == (r // q.shape[0]), qt, jnp.zeros_like(qt))


def _decode_update(z, v_bf, m_ref, l_ref, acc_ref):
    m_old = m_ref[...]
    m_new = jnp.maximum(m_old, jnp.max(z, axis=-1, keepdims=True))
    alpha = jnp.exp2(m_old - m_new)
    p = jnp.exp2(z - m_new)
    l_ref[...] = alpha * l_ref[...] + jnp.sum(p, axis=-1, keepdims=True)
    acc_ref[...] = alpha * acc_ref[...] + _dot(p.astype(BF16), v_bf)
    m_ref[...] = m_new


def _decode_positions(n_rows, n_keys, nq, past, key_start, n_valid):
    r = lax.broadcasted_iota(jnp.int32, (n_rows, n_keys), 0)
    c = lax.broadcasted_iota(jnp.int32, (n_rows, n_keys), 1)
    q_pos = past + r % nq
    k_pos = key_start + c
    visible = ((k_pos // CHUNK) <= (q_pos // CHUNK)) & (c < n_valid)
    return q_pos, k_pos, visible


def _pad_rows(x, n):
    return jnp.concatenate([x, jnp.zeros((n - x.shape[0], x.shape[1]), x.dtype)], axis=0)


def _diff_decode_kernel(slope_ref, q_ref, ck_ref, cv_ref, kn_ref, vn_ref, lam_ref, subln_ref,
                        o_ref, qbd, m, l, acc, *, past, tk, nq, dv, lam_init):
    j = pl.program_id(1)
    n_rows = qbd.shape[0]

    @pl.when(j == 0)
    def _():
        qbd[...] = _block_diag_queries(q_ref[0], n_rows // nq, dv // 2)
        m[...] = jnp.full(m.shape, NEG_INF, F32)
        l[...] = jnp.zeros(l.shape, F32)
        acc[...] = jnp.zeros(acc.shape, F32)

    def step(k_f32, v_f32, key_start, n_valid):
        n_keys = k_f32.shape[0]
        s = _dot_nt(qbd[...], k_f32.astype(BF16))
        q_pos, k_pos, visible = _decode_positions(n_rows, n_keys, nq, past, key_start, n_valid)
        dist = jnp.abs(q_pos - k_pos).astype(F32)
        z = jnp.where(visible, s - slope_ref[...] * dist, NEG_INF)
        _decode_update(z, v_f32.astype(BF16), m, l, acc)

    step(ck_ref[0], cv_ref[0], j * tk, tk)

    @pl.when(j == pl.num_programs(1) - 1)
    def _():
        step(_pad_rows(kn_ref[0], LANES), _pad_rows(vn_ref[0], LANES), past, nq)
        lam = _lambda_full(lam_ref, lam_init)
        for hh in range(n_rows // (2 * nq)):
            r0 = hh * 2 * nq
            cols = slice(hh * dv, (hh + 1) * dv)
            o1 = acc[r0:r0 + nq, cols] / l[r0:r0 + nq, :]
            o2 = acc[r0 + nq:r0 + 2 * nq, cols] / l[r0 + nq:r0 + 2 * nq, :]
            o = _rms(o1 - lam * o2, subln_ref[...]) * (1.0 - lam_init)
            o_ref[0, :, cols] = o.astype(BF16)


def _diff_attn_decode(q, cache_k, cache_v, k_new, v_new, lam, subln, slope_rows, lam_init):
    db, nq, width = q.shape
    past = cache_k.shape[1]
    dv = subln.shape[-1]
    n_rows = (width // dv) * 2 * nq
    tk = _row_tile(past, 512)
    stream = lambda n: pl.BlockSpec((1, nq, n), lambda b, j: (b, 0, 0))
    cache = pl.BlockSpec((1, tk, width), lambda b, j: (b, j, 0))
    return pl.pallas_call(
        functools.partial(_diff_decode_kernel, past=past, tk=tk, nq=nq, dv=dv, lam_init=lam_init),
        grid=(db, past // tk),
        in_specs=[_const_spec(slope_rows.shape), stream(width), cache, cache, stream(width),
                  stream(width), _const_spec(lam.shape), _const_spec(subln.shape)],
        out_specs=stream(width),
        out_shape=jax.ShapeDtypeStruct((db, nq, width), BF16),
        scratch_shapes=[pltpu.VMEM((n_rows, width), BF16), pltpu.VMEM((n_rows, 1), F32),
                        pltpu.VMEM((n_rows, 1), F32), pltpu.VMEM((n_rows, width), F32)],
        compiler_params=_params("parallel", "arbitrary"),
        name="diff_attn_decode",
    )(slope_rows, q, cache_k, cache_v, k_new, v_new, lam, subln)


def _mla_decode_kernel(q_ref, cc_ref, cpe_ref, cn_ref, pen_ref, wuk_ref, wuv_ref, o_ref,
                       qbd, m, l, acc, *, past, tk, nq, dv):
    j = pl.program_id(1)
    n_rows = qbd.shape[0]
    n_heads = n_rows // nq

    @pl.when(j == 0)
    def _():
        qbd[...] = _block_diag_queries(q_ref[0], n_heads, LANES)
        m[...] = jnp.full(m.shape, NEG_INF, F32)
        l[...] = jnp.zeros(l.shape, F32)
        acc[...] = jnp.zeros(acc.shape, F32)

    def step(c_f32, pe_f32, key_start, n_valid):
        n_keys = c_f32.shape[0]
        c_bf = c_f32.astype(BF16)
        kcat = (_dot(c_bf, wuk_ref[...]) + jnp.concatenate([pe_f32] * n_heads, axis=1)).astype(BF16)
        v_bf = _dot(c_bf, wuv_ref[...]).astype(BF16)
        s = _dot_nt(qbd[...], kcat)
        _, _, visible = _decode_positions(n_rows, n_keys, nq, past, key_start, n_valid)
        _decode_update(jnp.where(visible, s, NEG_INF), v_bf, m, l, acc)

    step(cc_ref[0], cpe_ref[0], j * tk, tk)

    @pl.when(j == pl.num_programs(1) - 1)
    def _():
        step(_pad_rows(cn_ref[0], LANES), _pad_rows(pen_ref[0], LANES), past, nq)
        lane = lax.broadcasted_iota(jnp.int32, (nq, 2 * dv), 1)
        for pair in range(n_heads // 2):
            r0 = pair * 2 * nq
            cols = slice(pair * 2 * dv, (pair + 1) * 2 * dv)
            even = acc[r0:r0 + nq, cols] / l[r0:r0 + nq, :]
            odd = acc[r0 + nq:r0 + 2 * nq, cols] / l[r0 + nq:r0 + 2 * nq, :]
            o_ref[0, :, cols] = jnp.where(lane < dv, even, odd).astype(BF16)


def _mla_attn_decode(qcat, cache_c, cache_pe, c_new, pe_new, wuk_bf, wuv_bf):
    db, nq, wq = qcat.shape
    past, lora = cache_c.shape[1], cache_c.shape[2]
    n_heads = wq // LANES
    wv = wuv_bf.shape[1]
    dv = wv // n_heads
    n_rows = n_heads * nq
    tk = _row_tile(past, 512)
    stream = lambda n: pl.BlockSpec((1, nq, n), lambda b, j: (b, 0, 0))
    return pl.pallas_call(
        functools.partial(_mla_decode_kernel, past=past, tk=tk, nq=nq, dv=dv),
        grid=(db, past // tk),
        in_specs=[stream(wq),
                  pl.BlockSpec((1, tk, lora), lambda b, j: (b, j, 0)),
                  pl.BlockSpec((1, tk, LANES), lambda b, j: (b, j, 0)),
                  stream(lora), stream(LANES),
                  _const_spec(wuk_bf.shape), _const_spec(wuv_bf.shape)],
        out_specs=stream(wv),
        out_shape=jax.ShapeDtypeStruct((db, nq, wv), BF16),
        scratch_shapes=[pltpu.VMEM((n_rows, wq), BF16), pltpu.VMEM((n_rows, 1), F32),
                        pltpu.VMEM((n_rows, 1), F32), pltpu.VMEM((n_rows, wv), F32)],
        compiler_params=_params("parallel", "arbitrary"),
        name="mla_attn_decode",
    )(qcat, cache_c, cache_pe, c_new, pe_new, wuk_bf, wuv_bf)


def _silu(x):
    return x * jax.nn.sigmoid(x)


def _out_ffn_kernel(x_ref, o_ref, wo_ref, g_ref, wgu_ref, wd_ref, out_ref, *, f, n_chunks):
    x1 = x_ref[...] + _dot(o_ref[...], wo_ref[...])
    h = _rms(x1, g_ref[...]).astype(BF16)
    fc = f // n_chunks
    acc = x1
    for c in range(n_chunks):
        gate = _dot(h, wgu_ref[:, c * fc:(c + 1) * fc])
        up = _dot(h, wgu_ref[:, f + c * fc:f + (c + 1) * fc])
        acc = acc + _dot((_silu(gate) * up).astype(BF16), wd_ref[c * fc:(c + 1) * fc, :])
    out_ref[...] = acc


def _out_ffn(x, o, wo_bf, g, wgu_bf, wd_bf):
    n, d = x.shape
    f = wd_bf.shape[0]
    tm = _row_tile(n, 512)
    n_chunks = 2 if f % (2 * LANES) == 0 else 1
    row = lambda w: pl.BlockSpec((tm, w), lambda t: (t, 0))
    return pl.pallas_call(
        functools.partial(_out_ffn_kernel, f=f, n_chunks=n_chunks),
        grid=(n // tm,),
        in_specs=[row(d), row(o.shape[1]), _const_spec(wo_bf.shape), _const_spec((1, d)),
                  _const_spec(wgu_bf.shape), _const_spec(wd_bf.shape)],
        out_specs=row(d),
        out_shape=jax.ShapeDtypeStruct((n, d), F32),
        compiler_params=_params("parallel"),
        name="attn_out_ffn",
    )(x, o, wo_bf, g, wgu_bf, wd_bf)


def _mla_proj_kernel(x_ref, g_ref, waq_ref, wakv_ref, wpe_ref, gq_ref, gkv_ref, wuq_ref, wuqs_ref,
                     wuk_ref, wuv_ref, c0_ref, s0_ref, c64_ref, s64_ref,
                     ckv_ref, kpe_ref, pe64_ref, q_ref, k_ref, vtx_ref, *, rope, dv, scale):
    tm = x_ref.shape[1]
    h = _rms(x_ref[0], g_ref[...]).astype(BF16)
    c_q = _rms(_dot(h, waq_ref[...]), gq_ref[...]).astype(BF16)
    c_kv = _rms(_dot(h, wakv_ref[...]), gkv_ref[...])
    ckv_ref[0] = c_kv
    c_kv_bf = c_kv.astype(BF16)
    pe = _dot(h, wpe_ref[...])
    kpe0 = pe[:, 0:LANES] * c0_ref[...] + pe[:, LANES:2 * LANES] * s0_ref[...]
    kpe_ref[0] = kpe0[:, :rope]
    c64 = c64_ref[...]
    s64 = s64_ref[...]
    pe64 = pe[:, 2 * LANES:3 * LANES] * c64 + pe[:, 3 * LANES:] * s64
    pe64_ref[0] = pe64
    q = _dot(c_q, wuq_ref[...])
    q_sw = _dot(c_q, wuqs_ref[...])
    k = _dot(c_kv_bf, wuk_ref[...])
    for hh in range(q.shape[1] // LANES):
        cols = slice(hh * LANES, (hh + 1) * LANES)
        q_ref[0, :, cols] = ((q[:, cols] * c64 + q_sw[:, cols] * s64) * scale).astype(BF16)
        k_ref[0, :, cols] = (k[:, cols] + pe64).astype(BF16)
    vt = _dot(c_kv_bf, wuv_ref[...]).T.astype(BF16)
    ones = _ones_row_block(tm)
    for hh in range(vt.shape[0] // dv):
        r0 = hh * (dv + BF16_ROWS)
        vtx_ref[0, r0:r0 + dv, :] = vt[hh * dv:(hh + 1) * dv, :]
        vtx_ref[0, r0 + dv:r0 + dv + BF16_ROWS, :] = ones


def _mla_proj(x, g, w, tabs, rope, dv, scale):
    bx, tx, d = x.shape
    tm = _row_tile(tx, 256)
    lora = w["wakv"].shape[1]
    wq, wv = w["wuq"].shape[1], w["wuv"].shape[1]
    vrows = (wv // dv) * (dv + BF16_ROWS)
    row = lambda n: pl.BlockSpec((1, tm, n), lambda b, t: (b, t, 0))
    tab = pl.BlockSpec((tm, LANES), lambda b, t: (t, 0))
    weights = [w[k] for k in ("waq", "wakv", "wpe", "gq", "gkv", "wuq", "wuqs", "wuk", "wuv")]
    return pl.pallas_call(
        functools.partial(_mla_proj_kernel, rope=rope, dv=dv, scale=scale),
        grid=(bx, tx // tm),
        in_specs=[row(d), _const_spec((1, d))] + [_const_spec(a.shape) for a in weights] + [tab] * 4,
        out_specs=[row(lora), row(rope), row(LANES), row(wq), row(wq),
                   pl.BlockSpec((1, vrows, tm), lambda b, t: (b, 0, t))],
        out_shape=[jax.ShapeDtypeStruct((bx, tx, lora), F32),
                   jax.ShapeDtypeStruct((bx, tx, rope), F32),
                   jax.ShapeDtypeStruct((bx, tx, LANES), F32),
                   jax.ShapeDtypeStruct((bx, tx, wq), BF16),
                   jax.ShapeDtypeStruct((bx, tx, wq), BF16),
                   jax.ShapeDtypeStruct((bx, vrows, tx), BF16)],
        compiler_params=_params("parallel", "parallel"),
        name="mla_proj",
    )(x, g, *weights, *tabs)


def _route_kernel(x_ref, o_ref, wo_ref, g_ref, router_ref, x3_ref, h_ref, route_ref, *, n_exp):
    x3 = x_ref[...] + _dot(o_ref[...], wo_ref[...])
    h = _rms(x3, g_ref[...])
    lane = lax.broadcasted_iota(jnp.int32, route_ref.shape, 1).astype(F32)
    logits = jnp.dot(h, router_ref[...], preferred_element_type=F32,
                     precision=lax.Precision.HIGHEST)
    logits = jnp.where(lane < n_exp, logits, -jnp.inf)
    v1 = jnp.max(logits, axis=-1, keepdims=True)
    i1 = jnp.min(jnp.where(logits == v1, lane, float(LANES)), axis=-1, keepdims=True)
    rest = jnp.where(lane == i1, -jnp.inf, logits)
    v2 = jnp.max(rest, axis=-1, keepdims=True)
    i2 = jnp.min(jnp.where(rest == v2, lane, float(LANES)), axis=-1, keepdims=True)
    e2 = jnp.exp(v2 - v1)
    denom = 1.0 + e2
    route = jnp.zeros(route_ref.shape, F32)
    for n, val in enumerate((i1, i2, 1.0 / denom, e2 / denom)):
        route = jnp.where(lane == n, val, route)
    x3_ref[...] = x3
    h_ref[...] = h
    route_ref[...] = route


def _moe_route(x, o, wo_bf, g, router_pad, n_exp):
    n, d = x.shape
    tm = _row_tile(n, 512)
    row = lambda w: pl.BlockSpec((tm, w), lambda t: (t, 0))
    return pl.pallas_call(
        functools.partial(_route_kernel, n_exp=n_exp),
        grid=(n // tm,),
        in_specs=[row(d), row(o.shape[1]), _const_spec(wo_bf.shape), _const_spec((1, d)),
                  _const_spec(router_pad.shape)],
        out_specs=[row(d), row(d), row(LANES)],
        out_shape=[jax.ShapeDtypeStruct((n, d), F32), jax.ShapeDtypeStruct((n, d), F32),
                   jax.ShapeDtypeStruct((n, LANES), F32)],
        compiler_params=_params("parallel"),
        name="moe_route",
    )(x, o, wo_bf, g, router_pad)


def _gather_rows_start(idx_ref, src_hbm, dst, sem, n_rows):
    def body(t, carry):
        pltpu.make_async_copy(src_hbm.at[pl.ds(idx_ref[0, 0, t], 1)], dst.at[pl.ds(t, 1)], sem).start()
        return carry
    lax.fori_loop(0, n_rows, body, 0, unroll=8)


def _gather_rows_wait(src_hbm, dst, sem, n_rows):
    pltpu.make_async_copy(src_hbm.at[pl.ds(0, n_rows)], dst, sem).wait()


def _experts_kernel(te_ref, nu_ref, idx_ref, idx_next_ref, h_hbm, wg_ref, wu_ref, wd_ref, ys_ref,
                    hbuf, hb, sems, *, tm):
    r = pl.program_id(0)
    c = pl.program_id(1)
    n_used = nu_ref[0]
    slot = r % 2

    @pl.when((c == 0) & (r < n_used))
    def _():
        @pl.when(r == 0)
        def _():
            _gather_rows_start(idx_ref, h_hbm, hbuf.at[0], sems.at[0], tm)

        @pl.when(r + 1 < n_used)
        def _():
            _gather_rows_start(idx_next_ref, h_hbm, hbuf.at[1 - slot], sems.at[1 - slot], tm)

        _gather_rows_wait(h_hbm, hbuf.at[slot], sems.at[slot], tm)
        hb[...] = hbuf[slot].astype(BF16)

    @pl.when(r < n_used)
    def _():
        h = hb[...]
        act = (_silu(_dot(h, wg_ref[0])) * _dot(h, wu_ref[0])).astype(BF16)
        y = _dot(act, wd_ref[0])

        @pl.when(c == 0)
        def _():
            ys_ref[...] = y

        @pl.when(c > 0)
        def _():
            ys_ref[...] += y

    @pl.when((c == 0) & (r >= n_used))
    def _():
        ys_ref[...] = jnp.zeros(ys_ref.shape, F32)


def _moe_experts(h, src_tiles, tile_expert, n_used, wgu_bf, wd_bf, tm):
    n_tiles = src_tiles.shape[0]
    d = h.shape[1]
    f = wd_bf.shape[1]
    fc = 512 if f % 512 == 0 else f
    n_fc = f // fc
    chunk = lambda r, c, nu: jnp.where(r < nu[0], c, 0)
    grid_spec = pltpu.PrefetchScalarGridSpec(
        num_scalar_prefetch=2,
        grid=(n_tiles, n_fc),
        in_specs=[pl.BlockSpec((1, 1, tm), lambda r, c, te, nu: (r, 0, 0), memory_space=pltpu.SMEM),
                  pl.BlockSpec((1, 1, tm), lambda r, c, te, nu: (jnp.minimum(r + 1, n_tiles - 1), 0, 0),
                               memory_space=pltpu.SMEM),
                  pl.BlockSpec(memory_space=pl.ANY),
                  pl.BlockSpec((1, d, fc), lambda r, c, te, nu: (te[r], 0, chunk(r, c, nu))),
                  pl.BlockSpec((1, d, fc), lambda r, c, te, nu: (te[r], 0, n_fc + chunk(r, c, nu))),
                  pl.BlockSpec((1, fc, d), lambda r, c, te, nu: (te[r], chunk(r, c, nu), 0))],
        out_specs=pl.BlockSpec((tm, d), lambda r, c, te, nu: (r, 0)),
        scratch_shapes=[pltpu.VMEM((2, tm, d), F32), pltpu.VMEM((tm, d), BF16),
                        pltpu.SemaphoreType.DMA((2,))])
    return pl.pallas_call(
        functools.partial(_experts_kernel, tm=tm),
        grid_spec=grid_spec,
        out_shape=jax.ShapeDtypeStruct((n_tiles * tm, d), F32),
        compiler_params=_params("arbitrary", "arbitrary"),
        name="moe_experts",
    )(tile_expert, n_used, src_tiles, src_tiles, h, wgu_bf, wgu_bf, wd_bf)


def _combine_kernel(idx1_ref, idx2_ref, x3_ref, route_ref, ys_hbm, gfin_ref, out_ref, ybuf, sems, *, tm):
    _gather_rows_start(idx1_ref, ys_hbm, ybuf.at[0], sems.at[0], tm)
    _gather_rows_start(idx2_ref, ys_hbm, ybuf.at[1], sems.at[1], tm)
    route = route_ref[...]
    lane = lax.broadcasted_iota(jnp.int32, route.shape, 1)
    gate = lambda n: jnp.sum(jnp.where(lane == n, route, 0.0), axis=-1, keepdims=True)
    g1, g2 = gate(2), gate(3)
    _gather_rows_wait(ys_hbm, ybuf.at[0], sems.at[0], tm)
    _gather_rows_wait(ys_hbm, ybuf.at[1], sems.at[1], tm)
    out_ref[...] = _rms(x3_ref[...] + (g1 * ybuf[0] + g2 * ybuf[1]), gfin_ref[...])


def _moe_combine(x3, route, ys, slot1, slot2, g_final, tm):
    n, d = x3.shape
    idx = pl.BlockSpec((1, 1, tm), lambda t: (t, 0, 0), memory_space=pltpu.SMEM)
    row = lambda w: pl.BlockSpec((tm, w), lambda t: (t, 0))
    return pl.pallas_call(
        functools.partial(_combine_kernel, tm=tm),
        grid=(n // tm,),
        in_specs=[idx, idx, row(d), row(LANES), pl.BlockSpec(memory_space=pl.ANY), _const_spec((1, d))],
        out_specs=row(d),
        out_shape=jax.ShapeDtypeStruct((n, d), F32),
        scratch_shapes=[pltpu.VMEM((2, tm, d), F32), pltpu.SemaphoreType.DMA((2,))],
        compiler_params=_params("arbitrary"),
        name="moe_combine",
    )(slot1, slot2, x3, route, ys, g_final)


def _routing_tables(route, n_exp, tm):
    n = route.shape[0]
    experts = route[:, :2].astype(jnp.int32).reshape(-1)
    onehot = (experts[:, None] == jnp.arange(n_exp, dtype=jnp.int32)[None, :]).astype(jnp.int32)
    csum = jnp.cumsum(onehot, axis=0)
    rank = jnp.sum(csum * onehot, axis=1) - 1
    counts = csum[-1]
    padded = ((counts + tm - 1) // tm) * tm
    ends = jnp.cumsum(padded)
    starts = ends - padded
    slots = jnp.sum(starts[None, :] * onehot, axis=1) + rank
    n_tiles = (2 * n) // tm + n_exp
    src = jnp.zeros((n_tiles * tm,), jnp.int32).at[slots].set(jnp.arange(2 * n, dtype=jnp.int32) // 2)
    tile_start = jnp.arange(n_tiles, dtype=jnp.int32) * tm
    tile_expert = jnp.minimum(jnp.sum((tile_start[:, None] >= ends[None, :]).astype(jnp.int32), axis=1),
                              n_exp - 1)
    n_used = (ends[-1] // tm).reshape(1)
    slots = slots.reshape(n, 2)
    return slots[:, 0], slots[:, 1], src, tile_expert, n_used


def _out_moe(x, o, wo_bf, g, router_pad, n_exp, wgu_bf, wd_bf, g_final):
    n, d = x.shape
    tm = _row_tile(n, 512)
    x3, h, route = _moe_route(x, o, wo_bf, g, router_pad, n_exp)
    slot1, slot2, src, tile_expert, n_used = _routing_tables(route, n_exp, tm)
    ys = _moe_experts(h, src.reshape(-1, 1, tm), tile_expert, n_used, wgu_bf, wd_bf, tm)
    return _moe_combine(x3, route, ys, slot1.reshape(-1, 1, tm), slot2.reshape(-1, 1, tm), g_final, tm)


def _lane_block(x, start):
    return jnp.pad(x, ((0, 0), (start, LANES - start - x.shape[1])))


def _swap_halves(x):
    half = x.shape[-1] // 2
    return jnp.concatenate([x[..., half:], x[..., :half]], axis=-1)


def _prep_mla_weights(w_a, g_q, g_kv, w_uq, w_ukv, q_lora, kv_lora, n_heads, nope, rope):
    w_pe = w_a[:, q_lora + kv_lora:]
    w_pe_sw = _swap_halves(w_pe)
    wpe = jnp.concatenate([_lane_block(w_pe, 0), _lane_block(w_pe_sw, 0),
                           _lane_block(w_pe, nope), _lane_block(w_pe_sw, nope)], axis=1)
    uq = w_uq.reshape(q_lora, n_heads, nope + rope)
    uq_nope, uq_pe = uq[..., :nope], uq[..., nope:]
    tail = jnp.zeros((q_lora, n_heads, LANES - nope - rope), w_uq.dtype)
    wuq = jnp.concatenate([uq_nope, uq_pe, tail], axis=-1).reshape(q_lora, n_heads * LANES)
    wuqs = jnp.concatenate([jnp.zeros_like(uq_nope), _swap_halves(uq_pe), tail],
                           axis=-1).reshape(q_lora, n_heads * LANES)
    ukv = w_ukv.reshape(kv_lora, n_heads, -1)
    uk = ukv[..., :nope]
    wuk = jnp.concatenate([uk, jnp.zeros((kv_lora, n_heads, LANES - nope), uk.dtype)],
                          axis=-1).reshape(kv_lora, n_heads * LANES)
    wuv = ukv[..., nope:].reshape(kv_lora, -1)
    bf = lambda a: a.astype(BF16)
    return {"waq": bf(w_a[:, :q_lora]), "wakv": bf(w_a[:, q_lora:q_lora + kv_lora]), "wpe": bf(wpe),
            "gq": g_q[None, :], "gkv": g_kv[None, :], "wuq": bf(wuq), "wuqs": bf(wuqs),
            "wuk": bf(wuk), "wuv": bf(wuv)}


def _rope_tables(pos, nope, rope):
    half = rope // 2
    freqs = jnp.power(ROPE_THETA, -jnp.arange(half, dtype=F32) * 2.0 / rope)
    ang = pos.astype(F32)[:, None] * freqs[None, :]
    cos = jnp.concatenate([jnp.cos(ang)] * 2, axis=1)
    sin = jnp.concatenate([-jnp.sin(ang), jnp.sin(ang)], axis=1)
    c64 = _lane_block(cos, nope) + _lane_block(jnp.ones((pos.shape[0], nope), F32), 0)
    return _lane_block(cos, 0), _lane_block(sin, 0), c64, _lane_block(sin, nope)


def _bf16_split3(x):
    hi = x.astype(BF16).astype(F32)
    mid = (x - hi).astype(BF16).astype(F32)
    lo = (x - hi - mid).astype(BF16).astype(F32)
    return hi, mid, lo


def kernel(x_prompt, x_sample, cache_diff_k, cache_diff_v, cache_mla_ckv, cache_mla_kpe, norm_mix, norm_ffn, norm_final, diff_w_qkv, diff_lambda, diff_subln, diff_w_o, mla_w_a, mla_norm_q, mla_norm_kv, mla_w_uq, mla_w_ukv, mla_w_o, ffn_w_gu, ffn_w_down, moe_router, moe_w_gu, moe_w_down):
    b, t, d = x_prompt.shape
    db, nq, _ = x_sample.shape
    past = cache_diff_k.shape[2]
    n_da_heads, dv = cache_diff_k.shape[3], diff_subln.shape[-1]
    da_width = n_da_heads * dv
    q_lora, kv_lora = mla_norm_q.shape[-1], mla_norm_kv.shape[-1]
    rope = mla_w_a.shape[-1] - q_lora - kv_lora
    mla_width = mla_w_o.shape[1]
    n_exp = moe_router.shape[-1]
    assert norm_mix.shape[0] == 2, "one differential-attention layer followed by one MLA layer"

    uq_cols, ukv_cols = mla_w_uq.shape[-1], mla_w_ukv.shape[-1]
    n_mla_heads = (uq_cols - (ukv_cols - mla_width)) // rope
    nope = uq_cols // n_mla_heads - rope
    mla_v = mla_width // n_mla_heads
    assert n_mla_heads * (nope + mla_v) == ukv_cols and n_mla_heads % 2 == 0
    assert nope + rope <= LANES and 2 * mla_v == LANES

    bf = lambda a: a.astype(BF16)
    lam_init = 0.8 - 0.6 * math.exp(-0.3 * 0)
    da_scale = (dv // 2) ** -0.5 * LOG2E
    mla_scale = (nope + rope) ** -0.5 * LOG2E
    slopes2 = jnp.exp2(-8.0 * jnp.arange(1, n_da_heads + 1, dtype=F32) / n_da_heads) * LOG2E
    slope_parts = jnp.stack(_bf16_split3(slopes2) + (slopes2,), axis=1).reshape(-1)
    slope_rows = jnp.repeat(slopes2, 2 * nq)[:, None]

    w_qkv, w_o0 = bf(diff_w_qkv[0]), bf(diff_w_o[0])
    w_gu0, w_d0 = bf(ffn_w_gu[0]), bf(ffn_w_down[0])
    mla_w = _prep_mla_weights(mla_w_a[0], mla_norm_q[0], mla_norm_kv[0], mla_w_uq[0], mla_w_ukv[0],
                              q_lora, kv_lora, n_mla_heads, nope, rope)
    w_o1 = bf(mla_w_o[0])
    router_pad = jnp.pad(moe_router[0], ((0, 0), (0, LANES - n_exp)))
    moe_gu, moe_d = bf(moe_w_gu.reshape(moe_w_gu.shape[1:])), bf(moe_w_down.reshape(moe_w_down.shape[1:]))
    g_mix, g_ffn, g_fin = norm_mix[:, None, :], norm_ffn[:, None, :], norm_final[None, :]
    lam, subln = diff_lambda[0], diff_subln[0][None, :]

    def trunk(x, pos, attn0, attn1):
        bx, tx, _ = x.shape
        x2d = x.reshape(bx * tx, d)
        q, k, v, kx, vtx = _qkv_proj(x, g_mix[0], w_qkv, dv, da_scale)
        o = attn0(q, k, v, kx, vtx)
        x2 = _out_ffn(x2d, o.reshape(bx * tx, da_width), w_o0, g_ffn[0], w_gu0, w_d0)
        ckv, kpe, pe64, qcat, kcat, vtx1 = _mla_proj(x2.reshape(bx, tx, d), g_mix[1], mla_w,
                                                    _rope_tables(pos, nope, rope), rope, mla_v,
                                                    mla_scale)
        o = attn1(ckv, pe64, qcat, kcat, vtx1)
        y = _out_moe(x2, o.reshape(bx * tx, mla_width), w_o1, g_ffn[1], router_pad, n_exp,
                     moe_gu, moe_d, g_fin)
        return y, k, v, ckv, kpe

    y_p, k_p, v_p, ckv_p, kpe_p = trunk(
        x_prompt, jnp.arange(t, dtype=jnp.int32),
        lambda q, k, v, kx, vtx: _diff_attn_prompt(q, kx, vtx, lam, subln, slope_parts, lam_init),
        lambda ckv, pe64, qcat, kcat, vtx1: _mla_attn_prompt(qcat, kcat, vtx1, mla_v))

    n_s = db * nq
    cache_k = cache_diff_k[0].reshape(db, past, da_width)
    cache_v = cache_diff_v[0].reshape(db, past, da_width)
    cache_pe = jnp.pad(cache_mla_kpe[0], ((0, 0), (0, 0), (nope, LANES - nope - rope)))
    pos_s = past + jnp.tile(jnp.arange(nq, dtype=jnp.int32), db)

    def attn0_s(q, k, v, kx, vtx):
        r = lambda a: a.reshape(db, nq, da_width)
        return _diff_attn_decode(r(q), cache_k, cache_v, r(k), r(v), lam, subln, slope_rows, lam_init)

    def attn1_s(ckv, pe64, qcat, kcat, vtx1):
        return _mla_attn_decode(qcat.reshape(db, nq, -1), cache_mla_ckv[0], cache_pe,
                                ckv.reshape(db, nq, kv_lora), pe64.reshape(db, nq, LANES),
                                mla_w["wuk"], mla_w["wuv"])

    y_s, k_s, v_s, ckv_s, kpe_s = trunk(x_sample.reshape(1, n_s, d), pos_s, attn0_s, attn1_s)

    return (y_p.reshape(b, t, d), y_s.reshape(db, nq, d),
            k_p.reshape(1, b, t, n_da_heads, dv), v_p.reshape(1, b, t, n_da_heads, dv),
            ckv_p.reshape(1, b, t, kv_lora), kpe_p.reshape(1, b, t, rope),
            k_s.reshape(1, db, nq, n_da_heads, dv), v_s.reshape(1, db, nq, n_da_heads, dv),
            ckv_s.reshape(1, db, nq, kv_lora), kpe_s.reshape(1, db, nq, rope))
```

```python
import functools
import math

import jax
import jax.numpy as jnp
from jax import lax
from jax.experimental import pallas as pl
from jax.experimental.pallas import tpu as pltpu

NORM_EPS = 1e-6
NEG_INF = -1e30
CHUNK = 64
ROPE_THETA = 10000.0
LOG2E = math.log2(math.e)
LANES = 128
MXU_DIM = 256
BF16_ROWS = 16
FLASH_Q_BLOCK = 1024
VMEM_LIMIT_BYTES = 56 * 2**20
BF16 = jnp.bfloat16
F32 = jnp.float32


def _params(*sem):
    return pltpu.CompilerParams(dimension_semantics=sem, vmem_limit_bytes=VMEM_LIMIT_BYTES)


def _const_spec(shape):
    nd = len(shape)
    return pl.BlockSpec(shape, lambda *_: (0,) * nd, pipeline_mode=pl.Buffered(1))


def _rms(x, g):
    ms = jnp.mean(x * x, axis=-1, keepdims=True)
    return x * lax.rsqrt(ms + NORM_EPS) * g


def _dot(a, b):
    return jnp.dot(a, b, preferred_element_type=F32)


def _dot_nt(a, b):
    return lax.dot_general(a, b, (((1,), (1,)), ((), ())), preferred_element_type=F32)


def _row_tile(t, cap):
    tm = min(t, cap)
    assert t % tm == 0
    return tm


def _ones_row_block(n_cols):
    sub = lax.broadcasted_iota(jnp.int32, (BF16_ROWS, n_cols), 0)
    return jnp.where(sub == 0, 1.0, 0.0).astype(BF16)


def _qkv_kernel(x_ref, g_ref, w_ref, q_ref, k_ref, v_ref, kx_ref, vtx_ref, *, width, dv, scale):
    tm = x_ref.shape[1]
    h = _rms(x_ref[0], g_ref[...]).astype(BF16)
    qkv = _dot(h, w_ref[...])
    q_ref[0] = (qkv[:, :width] * scale).astype(BF16)
    k = qkv[:, width:2 * width]
    v = qkv[:, 2 * width:]
    k_ref[0] = k
    v_ref[0] = v
    k_bf = k.astype(BF16)
    vt = v.T.astype(BF16)
    pos = lax.broadcasted_iota(jnp.int32, (tm, LANES), 0) + pl.program_id(1) * tm
    lane = lax.broadcasted_iota(jnp.int32, (tm, LANES), 1)
    pos_feat = jnp.where(lane < 3, pos // LANES, jnp.where(lane < 6, pos % LANES, 0))
    pos_feat = pos_feat.astype(F32).astype(BF16)
    ones = _ones_row_block(tm)
    for hh in range(width // dv):
        kx_ref[0, :, 2 * dv * hh:2 * dv * hh + dv] = k_bf[:, hh * dv:(hh + 1) * dv]
        kx_ref[0, :, 2 * dv * hh + dv:2 * dv * (hh + 1)] = pos_feat
        r0 = hh * (dv + BF16_ROWS)
        vtx_ref[0, r0:r0 + dv, :] = vt[hh * dv:(hh + 1) * dv, :]
        vtx_ref[0, r0 + dv:r0 + dv + BF16_ROWS, :] = ones


def _qkv_proj(x, g, w_bf, dv, scale):
    bx, tx, d = x.shape
    width = w_bf.shape[1] // 3
    n_heads = width // dv
    assert dv == LANES and tx <= LANES * MXU_DIM
    tm = _row_tile(tx, 512)
    row = lambda n: pl.BlockSpec((1, tm, n), lambda b, t: (b, t, 0))
    vrows = n_heads * (dv + BF16_ROWS)
    return pl.pallas_call(
        functools.partial(_qkv_kernel, width=width, dv=dv, scale=scale),
        grid=(bx, tx // tm),
        in_specs=[row(d), _const_spec((1, d)), _const_spec(w_bf.shape)],
        out_specs=[row(width), row(width), row(width), row(2 * width),
                   pl.BlockSpec((1, vrows, tm), lambda b, t: (b, 0, t))],
        out_shape=[jax.ShapeDtypeStruct((bx, tx, width), BF16),
                   jax.ShapeDtypeStruct((bx, tx, width), F32),
                   jax.ShapeDtypeStruct((bx, tx, width), F32),
                   jax.ShapeDtypeStruct((bx, tx, 2 * width), BF16),
                   jax.ShapeDtypeStruct((bx, vrows, tx), BF16)],
        compiler_params=_params("parallel", "parallel"),
        name="qkv_proj",
    )(x, g, w_bf)


def _flash_streams(i, tq, key_op, query_op, value_op, diag_tile, z_a, z_b, cm_a, cm_b, m, acc):
    n_streams = 2
    tk = tq // 2

    def produce(blk, z, cm, extra):
        off = pl.multiple_of(blk * tk, tk)
        for s in range(n_streams):
            k_blk = key_op(s, off)
            for n in range(tq // MXU_DIM):
                cols = slice(n * MXU_DIM, (n + 1) * MXU_DIM)
                scores = _dot_nt(k_blk, query_op(s, cols))
                if extra is not None:
                    scores = scores + extra[:, cols]
                z[s, :, cols] = scores
                cm[s, :, cols] = jnp.max(scores, axis=0, keepdims=True)

    def consume(blk, z, cm):
        off = pl.multiple_of(blk * tk, tk)
        for s in range(n_streams):
            m_old = m[s]
            m_new = jnp.maximum(m_old, cm[s])
            p = jnp.exp2(z[s] - m_new).astype(BF16)
            acc[s] = jnp.exp2(m_old - m_new) * acc[s] + _dot(value_op(s, off), p)
            m[s] = m_new

    m[...] = jnp.full(m.shape, NEG_INF, F32)
    acc[...] = jnp.zeros(acc.shape, F32)
    produce(2 * i, z_a, cm_a, diag_tile(0))

    def body(u, carry):
        produce(2 * u, z_b, cm_b, None)
        consume(jnp.where(u == 0, 2 * i, 2 * u - 1), z_a, cm_a)
        produce(2 * u + 1, z_a, cm_a, None)
        consume(2 * u, z_b, cm_b)
        return carry

    lax.fori_loop(0, i, body, 0)
    produce(2 * i + 1, z_b, cm_b, diag_tile(1))
    consume(jnp.where(i == 0, 0, 2 * i - 1), z_a, cm_a)
    consume(2 * i + 1, z_b, cm_b)


def _local_positions(tq, n):
    tk = tq // 2
    kl = lax.broadcasted_iota(jnp.int32, (tk, tq), 0) + n * tk
    ql = lax.broadcasted_iota(jnp.int32, (tk, tq), 1)
    return kl, ql, (kl // CHUNK) <= (ql // CHUNK)


def _lambda_full(lam_ref, lam_init):
    lf = lam_ref[...]
    s01 = jnp.sum(lf[0:1, :] * lf[1:2, :], axis=-1, keepdims=True)
    s23 = jnp.sum(lf[2:3, :] * lf[3:4, :], axis=-1, keepdims=True)
    return jnp.exp(s01) - jnp.exp(s23) + lam_init


def _diff_flash_kernel(slope_ref, q_ref, kx_ref, vtx_ref, lam_ref, subln_ref, o_ref,
                       qx, z_a, z_b, cm_a, cm_b, m, acc, *, tq, dv, lam_init):
    h = pl.program_id(1)
    i = pl.program_id(2)
    s_hi, s_mid, s_lo, slope2 = (slope_ref[4 * h + n] for n in range(4))
    q = q_ref[0]
    lane = lax.broadcasted_iota(jnp.int32, q.shape, 1)
    zero = jnp.zeros_like(q)
    slope_feat = jnp.zeros(q.shape, F32)
    for n, val in enumerate((LANES * s_hi, LANES * s_mid, LANES * s_lo, s_hi, s_mid, s_lo)):
        slope_feat = jnp.where(lane == n, val, slope_feat)
    slope_feat = slope_feat.astype(BF16)
    qx[0, :, 0:dv] = jnp.where(lane < dv // 2, q, zero)
    qx[1, :, 0:dv] = jnp.where(lane >= dv // 2, q, zero)
    qx[0, :, dv:2 * dv] = slope_feat
    qx[1, :, dv:2 * dv] = slope_feat

    def diag_tile(n):
        kl, ql, visible = _local_positions(tq, n)
        return jnp.where(visible, (-2.0 * slope2) * jnp.maximum(kl - ql, 0).astype(F32), NEG_INF)

    tk = tq // 2
    _flash_streams(i, tq,
                   key_op=lambda s, off: kx_ref[0, pl.ds(off, tk), :],
                   query_op=lambda s, cols: qx[s, cols, :],
                   value_op=lambda s, off: vtx_ref[0, :, pl.ds(off, tk)],
                   diag_tile=diag_tile, z_a=z_a, z_b=z_b, cm_a=cm_a, cm_b=cm_b, m=m, acc=acc)

    lam = _lambda_full(lam_ref, lam_init)
    o_t = (acc[0, 0:dv, :] / acc[0, dv:dv + 1, :]
           - lam * (acc[1, 0:dv, :] / acc[1, dv:dv + 1, :]))
    o = _rms(o_t.T, subln_ref[...]) * (1.0 - lam_init)
    o_ref[0] = o.astype(BF16)


def _flash_scratch(tq, q_feat, acc_rows):
    return ([pltpu.VMEM((2, tq, q_feat), BF16)] if q_feat else []) + [
        pltpu.VMEM((2, tq // 2, tq), F32), pltpu.VMEM((2, tq // 2, tq), F32),
        pltpu.VMEM((2, 1, tq), F32), pltpu.VMEM((2, 1, tq), F32),
        pltpu.VMEM((2, 1, tq), F32), pltpu.VMEM((2, acc_rows, tq), F32)]


def _diff_attn_prompt(q, kx, vtx, lam, subln, slope_parts, lam_init):
    b, t, width = q.shape
    dv = subln.shape[-1]
    n_heads = width // dv
    tq = _row_tile(t, FLASH_Q_BLOCK)
    assert tq % (2 * MXU_DIM) == 0
    vrows = dv + BF16_ROWS
    return pl.pallas_call(
        functools.partial(_diff_flash_kernel, tq=tq, dv=dv, lam_init=lam_init),
        grid=(b, n_heads, t // tq),
        in_specs=[pl.BlockSpec(memory_space=pltpu.SMEM),
                  pl.BlockSpec((1, tq, dv), lambda b_, h, i: (b_, i, h)),
                  pl.BlockSpec((1, t, 2 * dv), lambda b_, h, i: (b_, 0, h)),
                  pl.BlockSpec((1, vrows, t), lambda b_, h, i: (b_, h, 0)),
                  pl.BlockSpec(lam.shape, lambda b_, h, i: (0, 0)),
                  pl.BlockSpec(subln.shape, lambda b_, h, i: (0, 0))],
        out_specs=pl.BlockSpec((1, tq, dv), lambda b_, h, i: (b_, i, h)),
        out_shape=jax.ShapeDtypeStruct((b, t, width), BF16),
        scratch_shapes=_flash_scratch(tq, 2 * dv, vrows),
        compiler_params=_params("parallel", "parallel", "arbitrary"),
        name="diff_attn_prompt",
    )(slope_parts, q, kx, vtx, lam, subln)


def _mla_flash_kernel(q_ref, k_ref, vtx_ref, o_ref, z_a, z_b, cm_a, cm_b, m, acc, *, tq, dv):
    i = pl.program_id(2)
    vrows = dv + BF16_ROWS
    tk = tq // 2
    _flash_streams(i, tq,
                   key_op=lambda s, off: k_ref[0, pl.ds(off, tk), s * LANES:(s + 1) * LANES],
                   query_op=lambda s, cols: q_ref[0, cols, s * LANES:(s + 1) * LANES],
                   value_op=lambda s, off: vtx_ref[0, s * vrows:(s + 1) * vrows, pl.ds(off, tk)],
                   diag_tile=lambda n: jnp.where(_local_positions(tq, n)[2], 0.0, NEG_INF),
                   z_a=z_a, z_b=z_b, cm_a=cm_a, cm_b=cm_b, m=m, acc=acc)
    o_t = jnp.concatenate([acc[s, 0:dv, :] / acc[s, dv:dv + 1, :] for s in range(2)], axis=0)
    o_ref[0] = o_t.T.astype(BF16)


def _mla_attn_prompt(qcat, kcat, vtx, dv):
    b, t, wq = qcat.shape
    n_pairs = wq // (2 * LANES)
    tq = _row_tile(t, FLASH_Q_BLOCK)
    assert tq % (2 * MXU_DIM) == 0
    vrows = dv + BF16_ROWS
    return pl.pallas_call(
        functools.partial(_mla_flash_kernel, tq=tq, dv=dv),
        grid=(b, n_pairs, t // tq),
        in_specs=[pl.BlockSpec((1, tq, 2 * LANES), lambda b_, h, i: (b_, i, h)),
                  pl.BlockSpec((1, t, 2 * LANES), lambda b_, h, i: (b_, 0, h)),
                  pl.BlockSpec((1, 2 * vrows, t), lambda b_, h, i: (b_, h, 0))],
        out_specs=pl.BlockSpec((1, tq, 2 * dv), lambda b_, h, i: (b_, i, h)),
        out_shape=jax.ShapeDtypeStruct((b, t, n_pairs * 2 * dv), BF16),
        scratch_shapes=_flash_scratch(tq, 0, vrows),
        compiler_params=_params("parallel", "parallel", "arbitrary"),
        name="mla_attn_prompt",
    )(qcat, kcat, vtx)


def _block_diag_queries(q, n_groups, group_width):
    qt = jnp.concatenate([q] * n_groups, axis=0)
    r = lax.broadcasted_iota(jnp.int32, qt.shape, 0)
    c = lax.broadcasted_iota(jnp.int32, qt.shape, 1)
    return jnp.where((c // group_width) == (r // q.shape[0]), qt, jnp.zeros_like(qt))


def _decode_update(z, v_bf, m_ref, l_ref, acc_ref):
    m_old = m_ref[...]
    m_new = jnp.maximum(m_old, jnp.max(z, axis=-1, keepdims=True))
    alpha = jnp.exp2(m_old - m_new)
    p = jnp.exp2(z - m_new)
    l_ref[...] = alpha * l_ref[...] + jnp.sum(p, axis=-1, keepdims=True)
    acc_ref[...] = alpha * acc_ref[...] + _dot(p.astype(BF16), v_bf)
    m_ref[...] = m_new


def _decode_positions(n_rows, n_keys, nq, past, key_start, n_valid):
    r = lax.broadcasted_iota(jnp.int32, (n_rows, n_keys), 0)
    c = lax.broadcasted_iota(jnp.int32, (n_rows, n_keys), 1)
    q_pos = past + r % nq
    k_pos = key_start + c
    visible = ((k_pos // CHUNK) <= (q_pos // CHUNK)) & (c < n_valid)
    return q_pos, k_pos, visible


def _pad_rows(x, n):
    return jnp.concatenate([x, jnp.zeros((n - x.shape[0], x.shape[1]), x.dtype)], axis=0)


def _diff_decode_kernel(slope_ref, q_ref, ck_ref, cv_ref, kn_ref, vn_ref, lam_ref, subln_ref,
                        o_ref, qbd, m, l, acc, *, past, tk, nq, dv, lam_init):
    j = pl.program_id(1)
    n_rows = qbd.shape[0]

    @pl.when(j == 0)
    def _():
        qbd[...] = _block_diag_queries(q_ref[0], n_rows // nq, dv // 2)
        m[...] = jnp.full(m.shape, NEG_INF, F32)
        l[...] = jnp.zeros(l.shape, F32)
        acc[...] = jnp.zeros(acc.shape, F32)

    def step(k_f32, v_f32, key_start, n_valid):
        n_keys = k_f32.shape[0]
        s = _dot_nt(qbd[...], k_f32.astype(BF16))
        q_pos, k_pos, visible = _decode_positions(n_rows, n_keys, nq, past, key_start, n_valid)
        dist = jnp.abs(q_pos - k_pos).astype(F32)
        z = jnp.where(visible, s - slope_ref[...] * dist, NEG_INF)
        _decode_update(z, v_f32.astype(BF16), m, l, acc)

    step(ck_ref[0], cv_ref[0], j * tk, tk)

    @pl.when(j == pl.num_programs(1) - 1)
    def _():
        step(_pad_rows(kn_ref[0], LANES), _pad_rows(vn_ref[0], LANES), past, nq)
        lam = _lambda_full(lam_ref, lam_init)
        for hh in range(n_rows // (2 * nq)):
            r0 = hh * 2 * nq
            cols = slice(hh * dv, (hh + 1) * dv)
            o1 = acc[r0:r0 + nq, cols] / l[r0:r0 + nq, :]
            o2 = acc[r0 + nq:r0 + 2 * nq, cols] / l[r0 + nq:r0 + 2 * nq, :]
            o = _rms(o1 - lam * o2, subln_ref[...]) * (1.0 - lam_init)
            o_ref[0, :, cols] = o.astype(BF16)


def _diff_attn_decode(q, cache_k, cache_v, k_new, v_new, lam, subln, slope_rows, lam_init):
    db, nq, width = q.shape
    past = cache_k.shape[1]
    dv = subln.shape[-1]
    n_rows = (width // dv) * 2 * nq
    tk = _row_tile(past, 512)
    stream = lambda n: pl.BlockSpec((1, nq, n), lambda b, j: (b, 0, 0))
    cache = pl.BlockSpec((1, tk, width), lambda b, j: (b, j, 0))
    return pl.pallas_call(
        functools.partial(_diff_decode_kernel, past=past, tk=tk, nq=nq, dv=dv, lam_init=lam_init),
        grid=(db, past // tk),
        in_specs=[_const_spec(slope_rows.shape), stream(width), cache, cache, stream(width),
                  stream(width), _const_spec(lam.shape), _const_spec(subln.shape)],
        out_specs=stream(width),
        out_shape=jax.ShapeDtypeStruct((db, nq, width), BF16),
        scratch_shapes=[pltpu.VMEM((n_rows, width), BF16), pltpu.VMEM((n_rows, 1), F32),
                        pltpu.VMEM((n_rows, 1), F32), pltpu.VMEM((n_rows, width), F32)],
        compiler_params=_params("parallel", "arbitrary"),
        name="diff_attn_decode",
    )(slope_rows, q, cache_k, cache_v, k_new, v_new, lam, subln)


def _mla_decode_kernel(q_ref, cc_ref, cpe_ref, cn_ref, pen_ref, wuk_ref, wuv_ref, o_ref,
                       qbd, m, l, acc, *, past, tk, nq, dv):
    j = pl.program_id(1)
    n_rows = qbd.shape[0]
    n_heads = n_rows // nq

    @pl.when(j == 0)
    def _():
        qbd[...] = _block_diag_queries(q_ref[0], n_heads, LANES)
        m[...] = jnp.full(m.shape, NEG_INF, F32)
        l[...] = jnp.zeros(l.shape, F32)
        acc[...] = jnp.zeros(acc.shape, F32)

    def step(c_f32, pe_f32, key_start, n_valid):
        n_keys = c_f32.shape[0]
        c_bf = c_f32.astype(BF16)
        kcat = (_dot(c_bf, wuk_ref[...]) + jnp.concatenate([pe_f32] * n_heads, axis=1)).astype(BF16)
        v_bf = _dot(c_bf, wuv_ref[...]).astype(BF16)
        s = _dot_nt(qbd[...], kcat)
        _, _, visible = _decode_positions(n_rows, n_keys, nq, past, key_start, n_valid)
        _decode_update(jnp.where(visible, s, NEG_INF), v_bf, m, l, acc)

    step(cc_ref[0], cpe_ref[0], j * tk, tk)

    @pl.when(j == pl.num_programs(1) - 1)
    def _():
        step(_pad_rows(cn_ref[0], LANES), _pad_rows(pen_ref[0], LANES), past, nq)
        lane = lax.broadcasted_iota(jnp.int32, (nq, 2 * dv), 1)
        for pair in range(n_heads // 2):
            r0 = pair * 2 * nq
            cols = slice(pair * 2 * dv, (pair + 1) * 2 * dv)
            even = acc[r0:r0 + nq, cols] / l[r0:r0 + nq, :]
            odd = acc[r0 + nq:r0 + 2 * nq, cols] / l[r0 + nq:r0 + 2 * nq, :]
            o_ref[0, :, cols] = jnp.where(lane < dv, even, odd).astype(BF16)


def _mla_attn_decode(qcat, cache_c, cache_pe, c_new, pe_new, wuk_bf, wuv_bf):
    db, nq, wq = qcat.shape
    past, lora = cache_c.shape[1], cache_c.shape[2]
    n_heads = wq // LANES
    wv = wuv_bf.shape[1]
    dv = wv // n_heads
    n_rows = n_heads * nq
    tk = _row_tile(past, 512)
    stream = lambda n: pl.BlockSpec((1, nq, n), lambda b, j: (b, 0, 0))
    return pl.pallas_call(
        functools.partial(_mla_decode_kernel, past=past, tk=tk, nq=nq, dv=dv),
        grid=(db, past // tk),
        in_specs=[stream(wq),
                  pl.BlockSpec((1, tk, lora), lambda b, j: (b, j, 0)),
                  pl.BlockSpec((1, tk, LANES), lambda b, j: (b, j, 0)),
                  stream(lora), stream(LANES),
                  _const_spec(wuk_bf.shape), _const_spec(wuv_bf.shape)],
        out_specs=stream(wv),
        out_shape=jax.ShapeDtypeStruct((db, nq, wv), BF16),
        scratch_shapes=[pltpu.VMEM((n_rows, wq), BF16), pltpu.VMEM((n_rows, 1), F32),
                        pltpu.VMEM((n_rows, 1), F32), pltpu.VMEM((n_rows, wv), F32)],
        compiler_params=_params("parallel", "arbitrary"),
        name="mla_attn_decode",
    )(qcat, cache_c, cache_pe, c_new, pe_new, wuk_bf, wuv_bf)


def _silu(x):
    return x * jax.nn.sigmoid(x)


def _out_ffn_kernel(x_ref, o_ref, wo_ref, g_ref, wgu_ref, wd_ref, out_ref, *, f, n_chunks):
    x1 = x_ref[...] + _dot(o_ref[...], wo_ref[...])
    h = _rms(x1, g_ref[...]).astype(BF16)
    fc = f // n_chunks
    acc = x1
    for c in range(n_chunks):
        gate = _dot(h, wgu_ref[:, c * fc:(c + 1) * fc])
        up = _dot(h, wgu_ref[:, f + c * fc:f + (c + 1) * fc])
        acc = acc + _dot((_silu(gate) * up).astype(BF16), wd_ref[c * fc:(c + 1) * fc, :])
    out_ref[...] = acc


def _out_ffn(x, o, wo_bf, g, wgu_bf, wd_bf):
    n, d = x.shape
    f = wd_bf.shape[0]
    tm = _row_tile(n, 512)
    n_chunks = 2 if f % (2 * LANES) == 0 else 1
    row = lambda w: pl.BlockSpec((tm, w), lambda t: (t, 0))
    return pl.pallas_call(
        functools.partial(_out_ffn_kernel, f=f, n_chunks=n_chunks),
        grid=(n // tm,),
        in_specs=[row(d), row(o.shape[1]), _const_spec(wo_bf.shape), _const_spec((1, d)),
                  _const_spec(wgu_bf.shape), _const_spec(wd_bf.shape)],
        out_specs=row(d),
        out_shape=jax.ShapeDtypeStruct((n, d), F32),
        compiler_params=_params("parallel"),
        name="attn_out_ffn",
    )(x, o, wo_bf, g, wgu_bf, wd_bf)


def _mla_proj_kernel(x_ref, g_ref, waq_ref, wakv_ref, wpe_ref, gq_ref, gkv_ref, wuq_ref, wuqs_ref,
                     wuk_ref, wuv_ref, c0_ref, s0_ref, c64_ref, s64_ref,
                     ckv_ref, kpe_ref, pe64_ref, q_ref, k_ref, vtx_ref, *, rope, dv, scale):
    tm = x_ref.shape[1]
    h = _rms(x_ref[0], g_ref[...]).astype(BF16)
    c_q = _rms(_dot(h, waq_ref[...]), gq_ref[...]).astype(BF16)
    c_kv = _rms(_dot(h, wakv_ref[...]), gkv_ref[...])
    ckv_ref[0] = c_kv
    c_kv_bf = c_kv.astype(BF16)
    pe = _dot(h, wpe_ref[...])
    kpe0 = pe[:, 0:LANES] * c0_ref[...] + pe[:, LANES:2 * LANES] * s0_ref[...]
    kpe_ref[0] = kpe0[:, :rope]
    c64 = c64_ref[...]
    s64 = s64_ref[...]
    pe64 = pe[:, 2 * LANES:3 * LANES] * c64 + pe[:, 3 * LANES:] * s64
    pe64_ref[0] = pe64
    q = _dot(c_q, wuq_ref[...])
    q_sw = _dot(c_q, wuqs_ref[...])
    k = _dot(c_kv_bf, wuk_ref[...])
    for hh in range(q.shape[1] // LANES):
        cols = slice(hh * LANES, (hh + 1) * LANES)
        q_ref[0, :, cols] = ((q[:, cols] * c64 + q_sw[:, cols] * s64) * scale).astype(BF16)
        k_ref[0, :, cols] = (k[:, cols] + pe64).astype(BF16)
    vt = _dot(c_kv_bf, wuv_ref[...]).T.astype(BF16)
    ones = _ones_row_block(tm)
    for hh in range(vt.shape[0] // dv):
        r0 = hh * (dv + BF16_ROWS)
        vtx_ref[0, r0:r0 + dv, :] = vt[hh * dv:(hh + 1) * dv, :]
        vtx_ref[0, r0 + dv:r0 + dv + BF16_ROWS, :] = ones


def _mla_proj(x, g, w, tabs, rope, dv, scale):
    bx, tx, d = x.shape
    tm = _row_tile(tx, 256)
    lora = w["wakv"].shape[1]
    wq, wv = w["wuq"].shape[1], w["wuv"].shape[1]
    vrows = (wv // dv) * (dv + BF16_ROWS)
    row = lambda n: pl.BlockSpec((1, tm, n), lambda b, t: (b, t, 0))
    tab = pl.BlockSpec((tm, LANES), lambda b, t: (t, 0))
    weights = [w[k] for k in ("waq", "wakv", "wpe", "gq", "gkv", "wuq", "wuqs", "wuk", "wuv")]
    return pl.pallas_call(
        functools.partial(_mla_proj_kernel, rope=rope, dv=dv, scale=scale),
        grid=(bx, tx // tm),
        in_specs=[row(d), _const_spec((1, d))] + [_const_spec(a.shape) for a in weights] + [tab] * 4,
        out_specs=[row(lora), row(rope), row(LANES), row(wq), row(wq),
                   pl.BlockSpec((1, vrows, tm), lambda b, t: (b, 0, t))],
        out_shape=[jax.ShapeDtypeStruct((bx, tx, lora), F32),
                   jax.ShapeDtypeStruct((bx, tx, rope), F32),
                   jax.ShapeDtypeStruct((bx, tx, LANES), F32),
                   jax.ShapeDtypeStruct((bx, tx, wq), BF16),
                   jax.ShapeDtypeStruct((bx, tx, wq), BF16),
                   jax.ShapeDtypeStruct((bx, vrows, tx), BF16)],
        compiler_params=_params("parallel", "parallel"),
        name="mla_proj",
    )(x, g, *weights, *tabs)


def _route_kernel(x_ref, o_ref, wo_ref, g_ref, router_ref, x3_ref, h_ref, route_ref, *, n_exp):
    x3 = x_ref[...] + _dot(o_ref[...], wo_ref[...])
    h = _rms(x3, g_ref[...])
    lane = lax.broadcasted_iota(jnp.int32, route_ref.shape, 1).astype(F32)
    logits = jnp.dot(h, router_ref[...], preferred_element_type=F32,
                     precision=lax.Precision.HIGHEST)
    logits = jnp.where(lane < n_exp, logits, -jnp.inf)
    v1 = jnp.max(logits, axis=-1, keepdims=True)
    i1 = jnp.min(jnp.where(logits == v1, lane, float(LANES)), axis=-1, keepdims=True)
    rest = jnp.where(lane == i1, -jnp.inf, logits)
    v2 = jnp.max(rest, axis=-1, keepdims=True)
    i2 = jnp.min(jnp.where(rest == v2, lane, float(LANES)), axis=-1, keepdims=True)
    e2 = jnp.exp(v2 - v1)
    denom = 1.0 + e2
    route = jnp.zeros(route_ref.shape, F32)
    for n, val in enumerate((i1, i2, 1.0 / denom, e2 / denom)):
        route = jnp.where(lane == n, val, route)
    x3_ref[...] = x3
    h_ref[...] = h
    route_ref[...] = route


def _moe_route(x, o, wo_bf, g, router_pad, n_exp):
    n, d = x.shape
    tm = _row_tile(n, 512)
    row = lambda w: pl.BlockSpec((tm, w), lambda t: (t, 0))
    return pl.pallas_call(
        functools.partial(_route_kernel, n_exp=n_exp),
        grid=(n // tm,),
        in_specs=[row(d), row(o.shape[1]), _const_spec(wo_bf.shape), _const_spec((1, d)),
                  _const_spec(router_pad.shape)],
        out_specs=[row(d), row(d), row(LANES)],
        out_shape=[jax.ShapeDtypeStruct((n, d), F32), jax.ShapeDtypeStruct((n, d), F32),
                   jax.ShapeDtypeStruct((n, LANES), F32)],
        compiler_params=_params("parallel"),
        name="moe_route",
    )(x, o, wo_bf, g, router_pad)


def _gather_rows_start(idx_ref, src_hbm, dst, sem, n_rows):
    def body(t, carry):
        pltpu.make_async_copy(src_hbm.at[pl.ds(idx_ref[0, 0, t], 1)], dst.at[pl.ds(t, 1)], sem).start()
        return carry
    lax.fori_loop(0, n_rows, body, 0, unroll=8)


def _gather_rows_wait(src_hbm, dst, sem, n_rows):
    pltpu.make_async_copy(src_hbm.at[pl.ds(0, n_rows)], dst, sem).wait()


def _experts_kernel(te_ref, nu_ref, idx_ref, idx_next_ref, h_hbm, wg_ref, wu_ref, wd_ref, ys_ref,
                    hbuf, hb, sems, *, tm):
    r = pl.program_id(0)
    c = pl.program_id(1)
    n_used = nu_ref[0]
    slot = r % 2

    @pl.when((c == 0) & (r < n_used))
    def _():
        @pl.when(r == 0)
        def _():
            _gather_rows_start(idx_ref, h_hbm, hbuf.at[0], sems.at[0], tm)

        @pl.when(r + 1 < n_used)
        def _():
            _gather_rows_start(idx_next_ref, h_hbm, hbuf.at[1 - slot], sems.at[1 - slot], tm)

        _gather_rows_wait(h_hbm, hbuf.at[slot], sems.at[slot], tm)
        hb[...] = hbuf[slot].astype(BF16)

    @pl.when(r < n_used)
    def _():
        h = hb[...]
        act = (_silu(_dot(h, wg_ref[0])) * _dot(h, wu_ref[0])).astype(BF16)
        y = _dot(act, wd_ref[0])

        @pl.when(c == 0)
        def _():
            ys_ref[...] = y

        @pl.when(c > 0)
        def _():
            ys_ref[...] += y

    @pl.when((c == 0) & (r >= n_used))
    def _():
        ys_ref[...] = jnp.zeros(ys_ref.shape, F32)


def _moe_experts(h, src_tiles, tile_expert, n_used, wgu_bf, wd_bf, tm):
    n_tiles = src_tiles.shape[0]
    d = h.shape[1]
    f = wd_bf.shape[1]
    fc = 512 if f % 512 == 0 else f
    n_fc = f // fc
    chunk = lambda r, c, nu: jnp.where(r < nu[0], c, 0)
    grid_spec = pltpu.PrefetchScalarGridSpec(
        num_scalar_prefetch=2,
        grid=(n_tiles, n_fc),
        in_specs=[pl.BlockSpec((1, 1, tm), lambda r, c, te, nu: (r, 0, 0), memory_space=pltpu.SMEM),
                  pl.BlockSpec((1, 1, tm), lambda r, c, te, nu: (jnp.minimum(r + 1, n_tiles - 1), 0, 0),
                               memory_space=pltpu.SMEM),
                  pl.BlockSpec(memory_space=pl.ANY),
                  pl.BlockSpec((1, d, fc), lambda r, c, te, nu: (te[r], 0, chunk(r, c, nu))),
                  pl.BlockSpec((1, d, fc), lambda r, c, te, nu: (te[r], 0, n_fc + chunk(r, c, nu))),
                  pl.BlockSpec((1, fc, d), lambda r, c, te, nu: (te[r], chunk(r, c, nu), 0))],
        out_specs=pl.BlockSpec((tm, d), lambda r, c, te, nu: (r, 0)),
        scratch_shapes=[pltpu.VMEM((2, tm, d), F32), pltpu.VMEM((tm, d), BF16),
                        pltpu.SemaphoreType.DMA((2,))])
    return pl.pallas_call(
        functools.partial(_experts_kernel, tm=tm),
        grid_spec=grid_spec,
        out_shape=jax.ShapeDtypeStruct((n_tiles * tm, d), F32),
        compiler_params=_params("arbitrary", "arbitrary"),
        name="moe_experts",
    )(tile_expert, n_used, src_tiles, src_tiles, h, wgu_bf, wgu_bf, wd_bf)


def _combine_kernel(idx1_ref, idx2_ref, x3_ref, route_ref, ys_hbm, gfin_ref, out_ref, ybuf, sems, *, tm):
    _gather_rows_start(idx1_ref, ys_hbm, ybuf.at[0], sems.at[0], tm)
    _gather_rows_start(idx2_ref, ys_hbm, ybuf.at[1], sems.at[1], tm)
    route = route_ref[...]
    lane = lax.broadcasted_iota(jnp.int32, route.shape, 1)
    gate = lambda n: jnp.sum(jnp.where(lane == n, route, 0.0), axis=-1, keepdims=True)
    g1, g2 = gate(2), gate(3)
    _gather_rows_wait(ys_hbm, ybuf.at[0], sems.at[0], tm)
    _gather_rows_wait(ys_hbm, ybuf.at[1], sems.at[1], tm)
    out_ref[...] = _rms(x3_ref[...] + (g1 * ybuf[0] + g2 * ybuf[1]), gfin_ref[...])


def _moe_combine(x3, route, ys, slot1, slot2, g_final, tm):
    n, d = x3.shape
    idx = pl.BlockSpec((1, 1, tm), lambda t: (t, 0, 0), memory_space=pltpu.SMEM)
    row = lambda w: pl.BlockSpec((tm, w), lambda t: (t, 0))
    return pl.pallas_call(
        functools.partial(_combine_kernel, tm=tm),
        grid=(n // tm,),
        in_specs=[idx, idx, row(d), row(LANES), pl.BlockSpec(memory_space=pl.ANY), _const_spec((1, d))],
        out_specs=row(d),
        out_shape=jax.ShapeDtypeStruct((n, d), F32),
        scratch_shapes=[pltpu.VMEM((2, tm, d), F32), pltpu.SemaphoreType.DMA((2,))],
        compiler_params=_params("arbitrary"),
        name="moe_combine",
    )(slot1, slot2, x3, route, ys, g_final)


def _routing_tables(route, n_exp, tm):
    n = route.shape[0]
    experts = route[:, :2].astype(jnp.int32).reshape(-1)
    onehot = (experts[:, None] == jnp.arange(n_exp, dtype=jnp.int32)[None, :]).astype(jnp.int32)
    csum = jnp.cumsum(onehot, axis=0)
    rank = jnp.sum(csum * onehot, axis=1) - 1
    counts = csum[-1]
    padded = ((counts + tm - 1) // tm) * tm
    ends = jnp.cumsum(padded)
    starts = ends - padded
    slots = jnp.sum(starts[None, :] * onehot, axis=1) + rank
    n_tiles = (2 * n) // tm + n_exp
    src = jnp.zeros((n_tiles * tm,), jnp.int32).at[slots].set(jnp.arange(2 * n, dtype=jnp.int32) // 2)
    tile_start = jnp.arange(n_tiles, dtype=jnp.int32) * tm
    tile_expert = jnp.minimum(jnp.sum((tile_start[:, None] >= ends[None, :]).astype(jnp.int32), axis=1),
                              n_exp - 1)
    n_used = (ends[-1] // tm).reshape(1)
    slots = slots.reshape(n, 2)
    return slots[:, 0], slots[:, 1], src, tile_expert, n_used


def _out_moe(x, o, wo_bf, g, router_pad, n_exp, wgu_bf, wd_bf, g_final):
    n, d = x.shape
    tm = _row_tile(n, 512)
    x3, h, route = _moe_route(x, o, wo_bf, g, router_pad, n_exp)
    slot1, slot2, src, tile_expert, n_used = _routing_tables(route, n_exp, tm)
    ys = _moe_experts(h, src.reshape(-1, 1, tm), tile_expert, n_used, wgu_bf, wd_bf, tm)
    return _moe_combine(x3, route, ys, slot1.reshape(-1, 1, tm), slot2.reshape(-1, 1, tm), g_final, tm)


def _lane_block(x, start):
    return jnp.pad(x, ((0, 0), (start, LANES - start - x.shape[1])))


def _swap_halves(x):
    half = x.shape[-1] // 2
    return jnp.concatenate([x[..., half:], x[..., :half]], axis=-1)


def _prep_mla_weights(w_a, g_q, g_kv, w_uq, w_ukv, q_lora, kv_lora, n_heads, nope, rope):
    w_pe = w_a[:, q_lora + kv_lora:]
    w_pe_sw = _swap_halves(w_pe)
    wpe = jnp.concatenate([_lane_block(w_pe, 0), _lane_block(w_pe_sw, 0),
                           _lane_block(w_pe, nope), _lane_block(w_pe_sw, nope)], axis=1)
    uq = w_uq.reshape(q_lora, n_heads, nope + rope)
    uq_nope, uq_pe = uq[..., :nope], uq[..., nope:]
    tail = jnp.zeros((q_lora, n_heads, LANES - nope - rope), w_uq.dtype)
    wuq = jnp.concatenate([uq_nope, uq_pe, tail], axis=-1).reshape(q_lora, n_heads * LANES)
    wuqs = jnp.concatenate([jnp.zeros_like(uq_nope), _swap_halves(uq_pe), tail],
                           axis=-1).reshape(q_lora, n_heads * LANES)
    ukv = w_ukv.reshape(kv_lora, n_heads, -1)
    uk = ukv[..., :nope]
    wuk = jnp.concatenate([uk, jnp.zeros((kv_lora, n_heads, LANES - nope), uk.dtype)],
                          axis=-1).reshape(kv_lora, n_heads * LANES)
    wuv = ukv[..., nope:].reshape(kv_lora, -1)
    bf = lambda a: a.astype(BF16)
    return {"waq": bf(w_a[:, :q_lora]), "wakv": bf(w_a[:, q_lora:q_lora + kv_lora]), "wpe": bf(wpe),
            "gq": g_q[None, :], "gkv": g_kv[None, :], "wuq": bf(wuq), "wuqs": bf(wuqs),
            "wuk": bf(wuk), "wuv": bf(wuv)}


def _rope_tables(pos, nope, rope):
    half = rope // 2
    freqs = jnp.power(ROPE_THETA, -jnp.arange(half, dtype=F32) * 2.0 / rope)
    ang = pos.astype(F32)[:, None] * freqs[None, :]
    cos = jnp.concatenate([jnp.cos(ang)] * 2, axis=1)
    sin = jnp.concatenate([-jnp.sin(ang), jnp.sin(ang)], axis=1)
    c64 = _lane_block(cos, nope) + _lane_block(jnp.ones((pos.shape[0], nope), F32), 0)
    return _lane_block(cos, 0), _lane_block(sin, 0), c64, _lane_block(sin, nope)


def _bf16_split3(x):
    hi = x.astype(BF16).astype(F32)
    mid = (x - hi).astype(BF16).astype(F32)
    lo = (x - hi - mid).astype(BF16).astype(F32)
    return hi, mid, lo


def kernel(x_prompt, x_sample, cache_diff_k, cache_diff_v, cache_mla_ckv, cache_mla_kpe, norm_mix, norm_ffn, norm_final, diff_w_qkv, diff_lambda, diff_subln, diff_w_o, mla_w_a, mla_norm_q, mla_norm_kv, mla_w_uq, mla_w_ukv, mla_w_o, ffn_w_gu, ffn_w_down, moe_router, moe_w_gu, moe_w_down):
    b, t, d = x_prompt.shape
    db, nq, _ = x_sample.shape
    past = cache_diff_k.shape[2]
    n_da_heads, dv = cache_diff_k.shape[3], diff_subln.shape[-1]
    da_width = n_da_heads * dv
    q_lora, kv_lora = mla_norm_q.shape[-1], mla_norm_kv.shape[-1]
    rope = mla_w_a.shape[-1] - q_lora - kv_lora
    mla_width = mla_w_o.shape[1]
    n_exp = moe_router.shape[-1]
    assert norm_mix.shape[0] == 2, "one differential-attention layer followed by one MLA layer"

    uq_cols, ukv_cols = mla_w_uq.shape[-1], mla_w_ukv.shape[-1]
    n_mla_heads = (uq_cols - (ukv_cols - mla_width)) // rope
    nope = uq_cols // n_mla_heads - rope
    mla_v = mla_width // n_mla_heads
    assert n_mla_heads * (nope + mla_v) == ukv_cols and n_mla_heads % 2 == 0
    assert nope + rope <= LANES and 2 * mla_v == LANES

    bf = lambda a: a.astype(BF16)
    lam_init = 0.8 - 0.6 * math.exp(-0.3 * 0)
    da_scale = (dv // 2) ** -0.5 * LOG2E
    mla_scale = (nope + rope) ** -0.5 * LOG2E
    slopes2 = jnp.exp2(-8.0 * jnp.arange(1, n_da_heads + 1, dtype=F32) / n_da_heads) * LOG2E
    slope_parts = jnp.stack(_bf16_split3(slopes2) + (slopes2,), axis=1).reshape(-1)
    slope_rows = jnp.repeat(slopes2, 2 * nq)[:, None]

    w_qkv, w_o0 = bf(diff_w_qkv[0]), bf(diff_w_o[0])
    w_gu0, w_d0 = bf(ffn_w_gu[0]), bf(ffn_w_down[0])
    mla_w = _prep_mla_weights(mla_w_a[0], mla_norm_q[0], mla_norm_kv[0], mla_w_uq[0], mla_w_ukv[0],
                              q_lora, kv_lora, n_mla_heads, nope, rope)
    w_o1 = bf(mla_w_o[0])
    router_pad = jnp.pad(moe_router[0], ((0, 0), (0, LANES - n_exp)))
    moe_gu, moe_d = bf(moe_w_gu.reshape(moe_w_gu.shape[1:])), bf(moe_w_down.reshape(moe_w_down.shape[1:]))
    g_mix, g_ffn, g_fin = norm_mix[:, None, :], norm_ffn[:, None, :], norm_final[None, :]
    lam, subln = diff_lambda[0], diff_subln[0][None, :]

    def trunk(x, pos, attn0, attn1):
        bx, tx, _ = x.shape
        x2d = x.reshape(bx * tx, d)
        q, k, v, kx, vtx = _qkv_proj(x, g_mix[0], w_qkv, dv, da_scale)
        o = attn0(q, k, v, kx, vtx)
        x2 = _out_ffn(x2d, o.reshape(bx * tx, da_width), w_o0, g_ffn[0], w_gu0, w_d0)
        ckv, kpe, pe64, qcat, kcat, vtx1 = _mla_proj(x2.reshape(bx, tx, d), g_mix[1], mla_w,
                                                    _rope_tables(pos, nope, rope), rope, mla_v,
                                                    mla_scale)
        o = attn1(ckv, pe64, qcat, kcat, vtx1)
        y = _out_moe(x2, o.reshape(bx * tx, mla_width), w_o1, g_ffn[1], router_pad, n_exp,
                     moe_gu, moe_d, g_fin)
        return y, k, v, ckv, kpe

    y_p, k_p, v_p, ckv_p, kpe_p = trunk(
        x_prompt, jnp.arange(t, dtype=jnp.int32),
        lambda q, k, v, kx, vtx: _diff_attn_prompt(q, kx, vtx, lam, subln, slope_parts, lam_init),
        lambda ckv, pe64, qcat, kcat, vtx1: _mla_attn_prompt(qcat, kcat, vtx1, mla_v))

    n_s = db * nq
    cache_k = cache_diff_k[0].reshape(db, past, da_width)
    cache_v = cache_diff_v[0].reshape(db, past, da_width)
    cache_pe = jnp.pad(cache_mla_kpe[0], ((0, 0), (0, 0), (nope, LANES - nope - rope)))
    pos_s = past + jnp.tile(jnp.arange(nq, dtype=jnp.int32), db)

    def attn0_s(q, k, v, kx, vtx):
        r = lambda a: a.reshape(db, nq, da_width)
        return _diff_attn_decode(r(q), cache_k, cache_v, r(k), r(v), lam, subln, slope_rows, lam_init)

    def attn1_s(ckv, pe64, qcat, kcat, vtx1):
        return _mla_attn_decode(qcat.reshape(db, nq, -1), cache_mla_ckv[0], cache_pe,
                                ckv.reshape(db, nq, kv_lora), pe64.reshape(db, nq, LANES),
                                mla_w["wuk"], mla_w["wuv"])

    y_s, k_s, v_s, ckv_s, kpe_s = trunk(x_sample.reshape(1, n_s, d), pos_s, attn0_s, attn1_s)

    return (y_p.reshape(b, t, d), y_s.reshape(db, nq, d),
            k_p.reshape(1, b, t, n_da_heads, dv), v_p.reshape(1, b, t, n_da_heads, dv),
            ckv_p.reshape(1, b, t, kv_lora), kpe_p.reshape(1, b, t, rope),
            k_s.reshape(1, db, nq, n_da_heads, dv), v_s.reshape(1, db, nq, n_da_heads, dv),
            ckv_s.reshape(1, db, nq, kv_lora), kpe_s.reshape(1, db, nq, rope))
```

```python
import functools
import math

import jax
import jax.numpy as jnp
from jax import lax
from jax.experimental import pallas as pl
from jax.experimental.pallas import tpu as pltpu

NORM_EPS = 1e-6
NEG_INF = -1e30
CHUNK = 64
ROPE_THETA = 10000.0
LOG2E = math.log2(math.e)
LANES = 128
MXU_DIM = 256
BF16_ROWS = 16
EXPERT_CHUNK_CAP = 1024
FLASH_Q_BLOCK = 1024
VMEM_LIMIT_BYTES = 56 * 2**20
BF16 = jnp.bfloat16
F32 = jnp.float32


def _params(*sem):
    return pltpu.CompilerParams(dimension_semantics=sem, vmem_limit_bytes=VMEM_LIMIT_BYTES)


def _const_spec(shape):
    nd = len(shape)
    return pl.BlockSpec(shape, lambda *_: (0,) * nd, pipeline_mode=pl.Buffered(1))


def _rms(x, g):
    ms = jnp.mean(x * x, axis=-1, keepdims=True)
    return x * lax.rsqrt(ms + NORM_EPS) * g


def _dot(a, b):
    return jnp.dot(a, b, preferred_element_type=F32)


def _dot_nt(a, b):
    return lax.dot_general(a, b, (((1,), (1,)), ((), ())), preferred_element_type=F32)


def _row_tile(t, cap):
    tm = min(t, cap)
    assert t % tm == 0
    return tm


def _ones_row_block(n_cols):
    sub = lax.broadcasted_iota(jnp.int32, (BF16_ROWS, n_cols), 0)
    return jnp.where(sub == 0, 1.0, 0.0).astype(BF16)


def _qkv_kernel(x_ref, g_ref, w_ref, q_ref, k_ref, v_ref, kx_ref, vtx_ref, *, width, dv, scale):
    tm = x_ref.shape[1]
    h = _rms(x_ref[0], g_ref[...]).astype(BF16)
    qkv = _dot(h, w_ref[...])
    q_ref[0] = (qkv[:, :width] * scale).astype(BF16)
    k = qkv[:, width:2 * width]
    v = qkv[:, 2 * width:]
    k_ref[0] = k
    v_ref[0] = v
    k_bf = k.astype(BF16)
    vt = v.T.astype(BF16)
    pos = lax.broadcasted_iota(jnp.int32, (tm, LANES), 0) + pl.program_id(1) * tm
    lane = lax.broadcasted_iota(jnp.int32, (tm, LANES), 1)
    pos_feat = jnp.where(lane < 3, pos // LANES, jnp.where(lane < 6, pos % LANES, 0))
    pos_feat = pos_feat.astype(F32).astype(BF16)
    ones = _ones_row_block(tm)
    for hh in range(width // dv):
        kx_ref[0, :, 2 * dv * hh:2 * dv * hh + dv] = k_bf[:, hh * dv:(hh + 1) * dv]
        kx_ref[0, :, 2 * dv * hh + dv:2 * dv * (hh + 1)] = pos_feat
        r0 = hh * (dv + BF16_ROWS)
        vtx_ref[0, r0:r0 + dv, :] = vt[hh * dv:(hh + 1) * dv, :]
        vtx_ref[0, r0 + dv:r0 + dv + BF16_ROWS, :] = ones


def _qkv_proj(x, g, w_bf, dv, scale):
    bx, tx, d = x.shape
    width = w_bf.shape[1] // 3
    n_heads = width // dv
    assert dv == LANES and tx <= LANES * MXU_DIM
    tm = _row_tile(tx, 512)
    row = lambda n: pl.BlockSpec((1, tm, n), lambda b, t: (b, t, 0))
    vrows = n_heads * (dv + BF16_ROWS)
    return pl.pallas_call(
        functools.partial(_qkv_kernel, width=width, dv=dv, scale=scale),
        grid=(bx, tx // tm),
        in_specs=[row(d), _const_spec((1, d)), _const_spec(w_bf.shape)],
        out_specs=[row(width), row(width), row(width), row(2 * width),
                   pl.BlockSpec((1, vrows, tm), lambda b, t: (b, 0, t))],
        out_shape=[jax.ShapeDtypeStruct((bx, tx, width), BF16),
                   jax.ShapeDtypeStruct((bx, tx, width), F32),
                   jax.ShapeDtypeStruct((bx, tx, width), F32),
                   jax.ShapeDtypeStruct((bx, tx, 2 * width), BF16),
                   jax.ShapeDtypeStruct((bx, vrows, tx), BF16)],
        compiler_params=_params("parallel", "parallel"),
        name="qkv_proj",
    )(x, g, w_bf)


def _flash_streams(i, tq, key_op, query_op, value_op, diag_tile, z_a, z_b, cm_a, cm_b, m, acc):
    n_streams = 2
    tk = tq // 2

    units = [(s, slice(n * MXU_DIM, (n + 1) * MXU_DIM))
             for s in range(n_streams) for n in range(tq // MXU_DIM)]

    def produce(blk, z, cm, extra, s, cols):
        off = pl.multiple_of(blk * tk, tk)
        scores = _dot_nt(key_op(s, off), query_op(s, cols))
        if extra is not None:
            scores = scores + extra[:, cols]
        z[s, :, cols] = scores
        cm[s, :, cols] = jnp.max(scores, axis=0, keepdims=True)

    def consume(blk, z, cm, s, cols):
        off = pl.multiple_of(blk * tk, tk)
        m_old = m[s, :, cols]
        m_new = jnp.maximum(m_old, cm[s, :, cols])
        p = jnp.exp2(z[s, :, cols] - m_new).astype(BF16)
        acc[s, :, cols] = jnp.exp2(m_old - m_new) * acc[s, :, cols] + _dot(value_op(s, off), p)
        m[s, :, cols] = m_new

    def step(p_blk, p_z, p_cm, extra, c_blk, c_z, c_cm):
        for s, cols in units:
            produce(p_blk, p_z, p_cm, extra, s, cols)
            consume(c_blk, c_z, c_cm, s, cols)

    m[...] = jnp.full(m.shape, NEG_INF, F32)
    acc[...] = jnp.zeros(acc.shape, F32)
    first_tile = diag_tile(0)
    for s, cols in units:
        produce(2 * i, z_a, cm_a, first_tile, s, cols)

    def body(u, carry):
        step(2 * u, z_b, cm_b, None, jnp.where(u == 0, 2 * i, 2 * u - 1), z_a, cm_a)
        step(2 * u + 1, z_a, cm_a, None, 2 * u, z_b, cm_b)
        return carry

    lax.fori_loop(0, i, body, 0)
    step(2 * i + 1, z_b, cm_b, diag_tile(1), jnp.where(i == 0, 0, 2 * i - 1), z_a, cm_a)
    for s, cols in units:
        consume(2 * i + 1, z_b, cm_b, s, cols)


def _local_positions(tq, n):
    tk = tq // 2
    kl = lax.broadcasted_iota(jnp.int32, (tk, tq), 0) + n * tk
    ql = lax.broadcasted_iota(jnp.int32, (tk, tq), 1)
    return kl, ql, (kl // CHUNK) <= (ql // CHUNK)


def _lambda_full(lam_ref, lam_init):
    lf = lam_ref[...]
    s01 = jnp.sum(lf[0:1, :] * lf[1:2, :], axis=-1, keepdims=True)
    s23 = jnp.sum(lf[2:3, :] * lf[3:4, :], axis=-1, keepdims=True)
    return jnp.exp(s01) - jnp.exp(s23) + lam_init


def _diff_flash_kernel(slope_ref, q_ref, kx_ref, vtx_ref, lam_ref, subln_ref, o_ref,
                       qx, z_a, z_b, cm_a, cm_b, m, acc, *, tq, dv, lam_init):
    h = pl.program_id(1)
    i = pl.program_id(2)
    s_hi, s_mid, s_lo, slope2 = (slope_ref[4 * h + n] for n in range(4))
    q = q_ref[0]
    lane = lax.broadcasted_iota(jnp.int32, q.shape, 1)
    zero = jnp.zeros_like(q)
    slope_feat = jnp.zeros(q.shape, F32)
    for n, val in enumerate((LANES * s_hi, LANES * s_mid, LANES * s_lo, s_hi, s_mid, s_lo)):
        slope_feat = jnp.where(lane == n, val, slope_feat)
    slope_feat = slope_feat.astype(BF16)
    qx[0, :, 0:dv] = jnp.where(lane < dv // 2, q, zero)
    qx[1, :, 0:dv] = jnp.where(lane >= dv // 2, q, zero)
    qx[0, :, dv:2 * dv] = slope_feat
    qx[1, :, dv:2 * dv] = slope_feat

    def diag_tile(n):
        kl, ql, visible = _local_positions(tq, n)
        return jnp.where(visible, (-2.0 * slope2) * jnp.maximum(kl - ql, 0).astype(F32), NEG_INF)

    tk = tq // 2
    _flash_streams(i, tq,
                   key_op=lambda s, off: kx_ref[0, pl.ds(off, tk), :],
                   query_op=lambda s, cols: qx[s, cols, :],
                   value_op=lambda s, off: vtx_ref[0, :, pl.ds(off, tk)],
                   diag_tile=diag_tile, z_a=z_a, z_b=z_b, cm_a=cm_a, cm_b=cm_b, m=m, acc=acc)

    lam = _lambda_full(lam_ref, lam_init)
    o_t = (acc[0, 0:dv, :] / acc[0, dv:dv + 1, :]
           - lam * (acc[1, 0:dv, :] / acc[1, dv:dv + 1, :]))
    o = _rms(o_t.T, subln_ref[...]) * (1.0 - lam_init)
    o_ref[0] = o.astype(BF16)


def _flash_scratch(tq, q_feat, acc_rows):
    return ([pltpu.VMEM((2, tq, q_feat), BF16)] if q_feat else []) + [
        pltpu.VMEM((2, tq // 2, tq), F32), pltpu.VMEM((2, tq // 2, tq), F32),
        pltpu.VMEM((2, 1, tq), F32), pltpu.VMEM((2, 1, tq), F32),
        pltpu.VMEM((2, 1, tq), F32), pltpu.VMEM((2, acc_rows, tq), F32)]


def _diff_attn_prompt(q, kx, vtx, lam, subln, slope_parts, lam_init):
    b, t, width = q.shape
    dv = subln.shape[-1]
    n_heads = width // dv
    tq = _row_tile(t, FLASH_Q_BLOCK)
    assert tq % (2 * MXU_DIM) == 0
    vrows = dv + BF16_ROWS
    return pl.pallas_call(
        functools.partial(_diff_flash_kernel, tq=tq, dv=dv, lam_init=lam_init),
        grid=(b, n_heads, t // tq),
        in_specs=[pl.BlockSpec(memory_space=pltpu.SMEM),
                  pl.BlockSpec((1, tq, dv), lambda b_, h, i: (b_, i, h)),
                  pl.BlockSpec((1, t, 2 * dv), lambda b_, h, i: (b_, 0, h)),
                  pl.BlockSpec((1, vrows, t), lambda b_, h, i: (b_, h, 0)),
                  pl.BlockSpec(lam.shape, lambda b_, h, i: (0, 0)),
                  pl.BlockSpec(subln.shape, lambda b_, h, i: (0, 0))],
        out_specs=pl.BlockSpec((1, tq, dv), lambda b_, h, i: (b_, i, h)),
        out_shape=jax.ShapeDtypeStruct((b, t, width), BF16),
        scratch_shapes=_flash_scratch(tq, 2 * dv, vrows),
        compiler_params=_params("parallel", "parallel", "arbitrary"),
        name="diff_attn_prompt",
    )(slope_parts, q, kx, vtx, lam, subln)


def _mla_flash_kernel(q_ref, k_ref, vtx_ref, o_ref, z_a, z_b, cm_a, cm_b, m, acc, *, tq, dv):
    i = pl.program_id(2)
    vrows = dv + BF16_ROWS
    tk = tq // 2
    _flash_streams(i, tq,
                   key_op=lambda s, off: k_ref[0, pl.ds(off, tk), s * LANES:(s + 1) * LANES],
                   query_op=lambda s, cols: q_ref[0, cols, s * LANES:(s + 1) * LANES],
                   value_op=lambda s, off: vtx_ref[0, s * vrows:(s + 1) * vrows, pl.ds(off, tk)],
                   diag_tile=lambda n: jnp.where(_local_positions(tq, n)[2], 0.0, NEG_INF),
                   z_a=z_a, z_b=z_b, cm_a=cm_a, cm_b=cm_b, m=m, acc=acc)
    o_t = jnp.concatenate([acc[s, 0:dv, :] / acc[s, dv:dv + 1, :] for s in range(2)], axis=0)
    o_ref[0] = o_t.T.astype(BF16)


def _mla_attn_prompt(qcat, kcat, vtx, dv):
    b, t, wq = qcat.shape
    n_pairs = wq // (2 * LANES)
    tq = _row_tile(t, FLASH_Q_BLOCK)
    assert tq % (2 * MXU_DIM) == 0
    vrows = dv + BF16_ROWS
    return pl.pallas_call(
        functools.partial(_mla_flash_kernel, tq=tq, dv=dv),
        grid=(b, n_pairs, t // tq),
        in_specs=[pl.BlockSpec((1, tq, 2 * LANES), lambda b_, h, i: (b_, i, h)),
                  pl.BlockSpec((1, t, 2 * LANES), lambda b_, h, i: (b_, 0, h)),
                  pl.BlockSpec((1, 2 * vrows, t), lambda b_, h, i: (b_, h, 0))],
        out_specs=pl.BlockSpec((1, tq, 2 * dv), lambda b_, h, i: (b_, i, h)),
        out_shape=jax.ShapeDtypeStruct((b, t, n_pairs * 2 * dv), BF16),
        scratch_shapes=_flash_scratch(tq, 0, vrows),
        compiler_params=_params("parallel", "parallel", "arbitrary"),
        name="mla_attn_prompt",
    )(qcat, kcat, vtx)


def _block_diag_queries(q, n_groups, group_width):
    qt = jnp.concatenate([q] * n_groups, axis=0)
    r = lax.broadcasted_iota(jnp.int32, qt.shape, 0)
    c = lax.broadcasted_iota(jnp.int32, qt.shape, 1)
    return jnp.where((c // group_width) == (r // q.shape[0]), qt, jnp.zeros_like(qt))


def _decode_update(z, v_bf, m_ref, l_ref, acc_ref):
    m_old = m_ref[...]
    m_new = jnp.maximum(m_old, jnp.max(z, axis=-1, keepdims=True))
    alpha = jnp.exp2(m_old - m_new)
    p = jnp.exp2(z - m_new)
    l_ref[...] = alpha * l_ref[...] + jnp.sum(p, axis=-1, keepdims=True)
    acc_ref[...] = alpha * acc_ref[...] + _dot(p.astype(BF16), v_bf)
    m_ref[...] = m_new


def _decode_positions(n_rows, n_keys, nq, past, key_start, n_valid):
    r = lax.broadcasted_iota(jnp.int32, (n_rows, n_keys), 0)
    c = lax.broadcasted_iota(jnp.int32, (n_rows, n_keys), 1)
    q_pos = past + r % nq
    k_pos = key_start + c
    visible = ((k_pos // CHUNK) <= (q_pos // CHUNK)) & (c < n_valid)
    return q_pos, k_pos, visible


def _pad_rows(x, n):
    return jnp.concatenate([x, jnp.zeros((n - x.shape[0], x.shape[1]), x.dtype)], axis=0)


def _diff_decode_kernel(slope_ref, q_ref, ck_ref, cv_ref, kn_ref, vn_ref, lam_ref, subln_ref,
                        o_ref, qbd, m, l, acc, *, past, tk, nq, dv, lam_init):
    j = pl.program_id(1)
    n_rows = qbd.shape[0]

    @pl.when(j == 0)
    def _():
        qbd[...] = _block_diag_queries(q_ref[0], n_rows // nq, dv // 2)
        m[...] = jnp.full(m.shape, NEG_INF, F32)
        l[...] = jnp.zeros(l.shape, F32)
        acc[...] = jnp.zeros(acc.shape, F32)

    def step(k_f32, v_f32, key_start, n_valid):
        n_keys = k_f32.shape[0]
        s = _dot_nt(qbd[...], k_f32.astype(BF16))
        q_pos, k_pos, visible = _decode_positions(n_rows, n_keys, nq, past, key_start, n_valid)
        dist = jnp.abs(q_pos - k_pos).astype(F32)
        z = jnp.where(visible, s - slope_ref[...] * dist, NEG_INF)
        _decode_update(z, v_f32.astype(BF16), m, l, acc)

    step(ck_ref[0], cv_ref[0], j * tk, tk)

    @pl.when(j == pl.num_programs(1) - 1)
    def _():
        step(_pad_rows(kn_ref[0], LANES), _pad_rows(vn_ref[0], LANES), past, nq)
        lam = _lambda_full(lam_ref, lam_init)
        for hh in range(n_rows // (2 * nq)):
            r0 = hh * 2 * nq
            cols = slice(hh * dv, (hh + 1) * dv)
            o1 = acc[r0:r0 + nq, cols] / l[r0:r0 + nq, :]
            o2 = acc[r0 + nq:r0 + 2 * nq, cols] / l[r0 + nq:r0 + 2 * nq, :]
            o = _rms(o1 - lam * o2, subln_ref[...]) * (1.0 - lam_init)
            o_ref[0, :, cols] = o.astype(BF16)


def _diff_attn_decode(q, cache_k, cache_v, k_new, v_new, lam, subln, slope_rows, lam_init):
    db, nq, width = q.shape
    past = cache_k.shape[1]
    dv = subln.shape[-1]
    n_rows = (width // dv) * 2 * nq
    tk = _row_tile(past, 512)
    stream = lambda n: pl.BlockSpec((1, nq, n), lambda b, j: (b, 0, 0))
    cache = pl.BlockSpec((1, tk, width), lambda b, j: (b, j, 0))
    return pl.pallas_call(
        functools.partial(_diff_decode_kernel, past=past, tk=tk, nq=nq, dv=dv, lam_init=lam_init),
        grid=(db, past // tk),
        in_specs=[_const_spec(slope_rows.shape), stream(width), cache, cache, stream(width),
                  stream(width), _const_spec(lam.shape), _const_spec(subln.shape)],
        out_specs=stream(width),
        out_shape=jax.ShapeDtypeStruct((db, nq, width), BF16),
        scratch_shapes=[pltpu.VMEM((n_rows, width), BF16), pltpu.VMEM((n_rows, 1), F32),
                        pltpu.VMEM((n_rows, 1), F32), pltpu.VMEM((n_rows, width), F32)],
        compiler_params=_params("parallel", "arbitrary"),
        name="diff_attn_decode",
    )(slope_rows, q, cache_k, cache_v, k_new, v_new, lam, subln)


def _mla_decode_kernel(q_ref, cc_ref, cpe_ref, cn_ref, pen_ref, wuk_ref, wuv_ref, o_ref,
                       qbd, m, l, acc, *, past, tk, nq, dv):
    j = pl.program_id(1)
    n_rows = qbd.shape[0]
    n_heads = n_rows // nq

    @pl.when(j == 0)
    def _():
        qbd[...] = _block_diag_queries(q_ref[0], n_heads, LANES)
        m[...] = jnp.full(m.shape, NEG_INF, F32)
        l[...] = jnp.zeros(l.shape, F32)
        acc[...] = jnp.zeros(acc.shape, F32)

    def step(c_f32, pe_f32, key_start, n_valid):
        n_keys = c_f32.shape[0]
        c_bf = c_f32.astype(BF16)
        kcat = (_dot(c_bf, wuk_ref[...]) + jnp.concatenate([pe_f32] * n_heads, axis=1)).astype(BF16)
        v_bf = _dot(c_bf, wuv_ref[...]).astype(BF16)
        s = _dot_nt(qbd[...], kcat)
        _, _, visible = _decode_positions(n_rows, n_keys, nq, past, key_start, n_valid)
        _decode_update(jnp.where(visible, s, NEG_INF), v_bf, m, l, acc)

    step(cc_ref[0], cpe_ref[0], j * tk, tk)

    @pl.when(j == pl.num_programs(1) - 1)
    def _():
        step(_pad_rows(cn_ref[0], LANES), _pad_rows(pen_ref[0], LANES), past, nq)
        lane = lax.broadcasted_iota(jnp.int32, (nq, 2 * dv), 1)
        for pair in range(n_heads // 2):
            r0 = pair * 2 * nq
            cols = slice(pair * 2 * dv, (pair + 1) * 2 * dv)
            even = acc[r0:r0 + nq, cols] / l[r0:r0 + nq, :]
            odd = acc[r0 + nq:r0 + 2 * nq, cols] / l[r0 + nq:r0 + 2 * nq, :]
            o_ref[0, :, cols] = jnp.where(lane < dv, even, odd).astype(BF16)


def _mla_attn_decode(qcat, cache_c, cache_pe, c_new, pe_new, wuk_bf, wuv_bf):
    db, nq, wq = qcat.shape
    past, lora = cache_c.shape[1], cache_c.shape[2]
    n_heads = wq // LANES
    wv = wuv_bf.shape[1]
    dv = wv // n_heads
    n_rows = n_heads * nq
    tk = _row_tile(past, 512)
    stream = lambda n: pl.BlockSpec((1, nq, n), lambda b, j: (b, 0, 0))
    return pl.pallas_call(
        functools.partial(_mla_decode_kernel, past=past, tk=tk, nq=nq, dv=dv),
        grid=(db, past // tk),
        in_specs=[stream(wq),
                  pl.BlockSpec((1, tk, lora), lambda b, j: (b, j, 0)),
                  pl.BlockSpec((1, tk, LANES), lambda b, j: (b, j, 0)),
                  stream(lora), stream(LANES),
                  _const_spec(wuk_bf.shape), _const_spec(wuv_bf.shape)],
        out_specs=stream(wv),
        out_shape=jax.ShapeDtypeStruct((db, nq, wv), BF16),
        scratch_shapes=[pltpu.VMEM((n_rows, wq), BF16), pltpu.VMEM((n_rows, 1), F32),
                        pltpu.VMEM((n_rows, 1), F32), pltpu.VMEM((n_rows, wv), F32)],
        compiler_params=_params("parallel", "arbitrary"),
        name="mla_attn_decode",
    )(qcat, cache_c, cache_pe, c_new, pe_new, wuk_bf, wuv_bf)


def _silu(x):
    return x * jax.nn.sigmoid(x)


def _out_ffn_kernel(x_ref, o_ref, wo_ref, g_ref, wgu_ref, wd_ref, out_ref, *, f, n_chunks):
    x1 = x_ref[...] + _dot(o_ref[...], wo_ref[...])
    h = _rms(x1, g_ref[...]).astype(BF16)
    fc = f // n_chunks
    acc = x1
    for c in range(n_chunks):
        gate = _dot(h, wgu_ref[:, c * fc:(c + 1) * fc])
        up = _dot(h, wgu_ref[:, f + c * fc:f + (c + 1) * fc])
        acc = acc + _dot((_silu(gate) * up).astype(BF16), wd_ref[c * fc:(c + 1) * fc, :])
    out_ref[...] = acc


def _out_ffn(x, o, wo_bf, g, wgu_bf, wd_bf):
    n, d = x.shape
    f = wd_bf.shape[0]
    tm = _row_tile(n, 512)
    n_chunks = 2 if f % (2 * LANES) == 0 else 1
    row = lambda w: pl.BlockSpec((tm, w), lambda t: (t, 0))
    return pl.pallas_call(
        functools.partial(_out_ffn_kernel, f=f, n_chunks=n_chunks),
        grid=(n // tm,),
        in_specs=[row(d), row(o.shape[1]), _const_spec(wo_bf.shape), _const_spec((1, d)),
                  _const_spec(wgu_bf.shape), _const_spec(wd_bf.shape)],
        out_specs=row(d),
        out_shape=jax.ShapeDtypeStruct((n, d), F32),
        compiler_params=_params("parallel"),
        name="attn_out_ffn",
    )(x, o, wo_bf, g, wgu_bf, wd_bf)


def _mla_proj_kernel(x_ref, g_ref, waq_ref, wakv_ref, wpe_ref, gq_ref, gkv_ref, wuq_ref, wuqs_ref,
                     wuk_ref, wuv_ref, c0_ref, s0_ref, c64_ref, s64_ref,
                     ckv_ref, kpe_ref, pe64_ref, q_ref, k_ref, vtx_ref, *, rope, dv, scale):
    tm = x_ref.shape[1]
    h = _rms(x_ref[0], g_ref[...]).astype(BF16)
    c_q = _rms(_dot(h, waq_ref[...]), gq_ref[...]).astype(BF16)
    c_kv = _rms(_dot(h, wakv_ref[...]), gkv_ref[...])
    ckv_ref[0] = c_kv
    c_kv_bf = c_kv.astype(BF16)
    pe = _dot(h, wpe_ref[...])
    kpe0 = pe[:, 0:LANES] * c0_ref[...] + pe[:, LANES:2 * LANES] * s0_ref[...]
    kpe_ref[0] = kpe0[:, :rope]
    c64 = c64_ref[...]
    s64 = s64_ref[...]
    pe64 = pe[:, 2 * LANES:3 * LANES] * c64 + pe[:, 3 * LANES:] * s64
    pe64_ref[0] = pe64
    q = _dot(c_q, wuq_ref[...])
    q_sw = _dot(c_q, wuqs_ref[...])
    k = _dot(c_kv_bf, wuk_ref[...])
    for hh in range(q.shape[1] // LANES):
        cols = slice(hh * LANES, (hh + 1) * LANES)
        q_ref[0, :, cols] = ((q[:, cols] * c64 + q_sw[:, cols] * s64) * scale).astype(BF16)
        k_ref[0, :, cols] = (k[:, cols] + pe64).astype(BF16)
    vt = _dot(c_kv_bf, wuv_ref[...]).T.astype(BF16)
    ones = _ones_row_block(tm)
    for hh in range(vt.shape[0] // dv):
        r0 = hh * (dv + BF16_ROWS)
        vtx_ref[0, r0:r0 + dv, :] = vt[hh * dv:(hh + 1) * dv, :]
        vtx_ref[0, r0 + dv:r0 + dv + BF16_ROWS, :] = ones


def _mla_proj(x, g, w, tabs, rope, dv, scale):
    bx, tx, d = x.shape
    tm = _row_tile(tx, 256)
    lora = w["wakv"].shape[1]
    wq, wv = w["wuq"].shape[1], w["wuv"].shape[1]
    vrows = (wv // dv) * (dv + BF16_ROWS)
    row = lambda n: pl.BlockSpec((1, tm, n), lambda b, t: (b, t, 0))
    tab = pl.BlockSpec((tm, LANES), lambda b, t: (t, 0))
    weights = [w[k] for k in ("waq", "wakv", "wpe", "gq", "gkv", "wuq", "wuqs", "wuk", "wuv")]
    return pl.pallas_call(
        functools.partial(_mla_proj_kernel, rope=rope, dv=dv, scale=scale),
        grid=(bx, tx // tm),
        in_specs=[row(d), _const_spec((1, d))] + [_const_spec(a.shape) for a in weights] + [tab] * 4,
        out_specs=[row(lora), row(rope), row(LANES), row(wq), row(wq),
                   pl.BlockSpec((1, vrows, tm), lambda b, t: (b, 0, t))],
        out_shape=[jax.ShapeDtypeStruct((bx, tx, lora), F32),
                   jax.ShapeDtypeStruct((bx, tx, rope), F32),
                   jax.ShapeDtypeStruct((bx, tx, LANES), F32),
                   jax.ShapeDtypeStruct((bx, tx, wq), BF16),
                   jax.ShapeDtypeStruct((bx, tx, wq), BF16),
                   jax.ShapeDtypeStruct((bx, vrows, tx), BF16)],
        compiler_params=_params("parallel", "parallel"),
        name="mla_proj",
    )(x, g, *weights, *tabs)


def _route_kernel(x_ref, o_ref, wo_ref, g_ref, router_ref, x3_ref, h_ref, route_ref, *, n_exp):
    x3 = x_ref[...] + _dot(o_ref[...], wo_ref[...])
    h = _rms(x3, g_ref[...])
    lane = lax.broadcasted_iota(jnp.int32, route_ref.shape, 1).astype(F32)
    logits = jnp.dot(h, router_ref[...], preferred_element_type=F32,
                     precision=lax.Precision.HIGHEST)
    logits = jnp.where(lane < n_exp, logits, -jnp.inf)
    v1 = jnp.max(logits, axis=-1, keepdims=True)
    i1 = jnp.min(jnp.where(logits == v1, lane, float(LANES)), axis=-1, keepdims=True)
    rest = jnp.where(lane == i1, -jnp.inf, logits)
    v2 = jnp.max(rest, axis=-1, keepdims=True)
    i2 = jnp.min(jnp.where(rest == v2, lane, float(LANES)), axis=-1, keepdims=True)
    e2 = jnp.exp(v2 - v1)
    denom = 1.0 + e2
    route = jnp.zeros(route_ref.shape, F32)
    for n, val in enumerate((i1, i2, 1.0 / denom, e2 / denom)):
        route = jnp.where(lane == n, val, route)
    x3_ref[...] = x3
    h_ref[...] = h
    route_ref[...] = route


def _moe_route(x, o, wo_bf, g, router_pad, n_exp):
    n, d = x.shape
    tm = _row_tile(n, 512)
    row = lambda w: pl.BlockSpec((tm, w), lambda t: (t, 0))
    return pl.pallas_call(
        functools.partial(_route_kernel, n_exp=n_exp),
        grid=(n // tm,),
        in_specs=[row(d), row(o.shape[1]), _const_spec(wo_bf.shape), _const_spec((1, d)),
                  _const_spec(router_pad.shape)],
        out_specs=[row(d), row(d), row(LANES)],
        out_shape=[jax.ShapeDtypeStruct((n, d), F32), jax.ShapeDtypeStruct((n, d), F32),
                   jax.ShapeDtypeStruct((n, LANES), F32)],
        compiler_params=_params("parallel"),
        name="moe_route",
    )(x, o, wo_bf, g, router_pad)


def _gather_rows_start(idx_ref, src_hbm, dst, sem, n_rows):
    def body(t, carry):
        pltpu.make_async_copy(src_hbm.at[pl.ds(idx_ref[0, 0, t], 1)], dst.at[pl.ds(t, 1)], sem).start()
        return carry
    lax.fori_loop(0, n_rows, body, 0, unroll=8)


def _gather_rows_wait(src_hbm, dst, sem, n_rows):
    pltpu.make_async_copy(src_hbm.at[pl.ds(0, n_rows)], dst, sem).wait()


def _experts_kernel(te_ref, nu_ref, idx_ref, idx_next_ref, h_hbm, wg_ref, wu_ref, wd_ref, ys_ref,
                    hbuf, hb, sems, *, tm):
    r = pl.program_id(0)
    c = pl.program_id(1)
    n_used = nu_ref[0]
    slot = r % 2

    @pl.when((c == 0) & (r < n_used))
    def _():
        @pl.when(r == 0)
        def _():
            _gather_rows_start(idx_ref, h_hbm, hbuf.at[0], sems.at[0], tm)

        @pl.when(r + 1 < n_used)
        def _():
            _gather_rows_start(idx_next_ref, h_hbm, hbuf.at[1 - slot], sems.at[1 - slot], tm)

        _gather_rows_wait(h_hbm, hbuf.at[slot], sems.at[slot], tm)
        hb[...] = hbuf[slot].astype(BF16)

    @pl.when(r < n_used)
    def _():
        h = hb[...]
        act = (_silu(_dot(h, wg_ref[0])) * _dot(h, wu_ref[0])).astype(BF16)
        y = _dot(act, wd_ref[0])

        @pl.when(c == 0)
        def _():
            ys_ref[...] = y

        @pl.when(c > 0)
        def _():
            ys_ref[...] += y

    @pl.when((c == 0) & (r >= n_used))
    def _():
        ys_ref[...] = jnp.zeros(ys_ref.shape, F32)


def _moe_experts(h, src_tiles, tile_expert, n_used, wgu_bf, wd_bf, tm):
    n_tiles = src_tiles.shape[0]
    d = h.shape[1]
    f = wd_bf.shape[1]
    fc = max(c for c in range(LANES, EXPERT_CHUNK_CAP + 1, LANES) if f % c == 0)
    n_fc = f // fc
    chunk = lambda r, c, nu: jnp.where(r < nu[0], c, 0)
    grid_spec = pltpu.PrefetchScalarGridSpec(
        num_scalar_prefetch=2,
        grid=(n_tiles, n_fc),
        in_specs=[pl.BlockSpec((1, 1, tm), lambda r, c, te, nu: (r, 0, 0), memory_space=pltpu.SMEM),
                  pl.BlockSpec((1, 1, tm), lambda r, c, te, nu: (jnp.minimum(r + 1, n_tiles - 1), 0, 0),
                               memory_space=pltpu.SMEM),
                  pl.BlockSpec(memory_space=pl.ANY),
                  pl.BlockSpec((1, d, fc), lambda r, c, te, nu: (te[r], 0, chunk(r, c, nu))),
                  pl.BlockSpec((1, d, fc), lambda r, c, te, nu: (te[r], 0, n_fc + chunk(r, c, nu))),
                  pl.BlockSpec((1, fc, d), lambda r, c, te, nu: (te[r], chunk(r, c, nu), 0))],
        out_specs=pl.BlockSpec((tm, d), lambda r, c, te, nu: (r, 0)),
        scratch_shapes=[pltpu.VMEM((2, tm, d), F32), pltpu.VMEM((tm, d), BF16),
                        pltpu.SemaphoreType.DMA((2,))])
    return pl.pallas_call(
        functools.partial(_experts_kernel, tm=tm),
        grid_spec=grid_spec,
        out_shape=jax.ShapeDtypeStruct((n_tiles * tm, d), F32),
        compiler_params=_params("arbitrary", "arbitrary"),
        name="moe_experts",
    )(tile_expert, n_used, src_tiles, src_tiles, h, wgu_bf, wgu_bf, wd_bf)


def _combine_kernel(idx1_ref, idx2_ref, x3_ref, route_ref, ys_hbm, gfin_ref, out_ref, ybuf, sems, *, tm):
    _gather_rows_start(idx1_ref, ys_hbm, ybuf.at[0], sems.at[0], tm)
    _gather_rows_start(idx2_ref, ys_hbm, ybuf.at[1], sems.at[1], tm)
    route = route_ref[...]
    lane = lax.broadcasted_iota(jnp.int32, route.shape, 1)
    gate = lambda n: jnp.sum(jnp.where(lane == n, route, 0.0), axis=-1, keepdims=True)
    g1, g2 = gate(2), gate(3)
    _gather_rows_wait(ys_hbm, ybuf.at[0], sems.at[0], tm)
    _gather_rows_wait(ys_hbm, ybuf.at[1], sems.at[1], tm)
    out_ref[...] = _rms(x3_ref[...] + (g1 * ybuf[0] + g2 * ybuf[1]), gfin_ref[...])


def _moe_combine(x3, route, ys, slot1, slot2, g_final, tm):
    n, d = x3.shape
    idx = pl.BlockSpec((1, 1, tm), lambda t: (t, 0, 0), memory_space=pltpu.SMEM)
    row = lambda w: pl.BlockSpec((tm, w), lambda t: (t, 0))
    return pl.pallas_call(
        functools.partial(_combine_kernel, tm=tm),
        grid=(n // tm,),
        in_specs=[idx, idx, row(d), row(LANES), pl.BlockSpec(memory_space=pl.ANY), _const_spec((1, d))],
        out_specs=row(d),
        out_shape=jax.ShapeDtypeStruct((n, d), F32),
        scratch_shapes=[pltpu.VMEM((2, tm, d), F32), pltpu.SemaphoreType.DMA((2,))],
        compiler_params=_params("arbitrary"),
        name="moe_combine",
    )(slot1, slot2, x3, route, ys, g_final)


def _routing_tables(route, n_exp, tm):
    n = route.shape[0]
    experts = route[:, :2].astype(jnp.int32).reshape(-1)
    onehot = (experts[:, None] == jnp.arange(n_exp, dtype=jnp.int32)[None, :]).astype(jnp.int32)
    csum = jnp.cumsum(onehot, axis=0)
    rank = jnp.sum(csum * onehot, axis=1) - 1
    counts = csum[-1]
    padded = ((counts + tm - 1) // tm) * tm
    ends = jnp.cumsum(padded)
    starts = ends - padded
    slots = jnp.sum(starts[None, :] * onehot, axis=1) + rank
    n_tiles = (2 * n) // tm + n_exp
    src = jnp.zeros((n_tiles * tm,), jnp.int32).at[slots].set(jnp.arange(2 * n, dtype=jnp.int32) // 2)
    tile_start = jnp.arange(n_tiles, dtype=jnp.int32) * tm
    tile_expert = jnp.minimum(jnp.sum((tile_start[:, None] >= ends[None, :]).astype(jnp.int32), axis=1),
                              n_exp - 1)
    n_used = (ends[-1] // tm).reshape(1)
    slots = slots.reshape(n, 2)
    return slots[:, 0], slots[:, 1], src, tile_expert, n_used


def _out_moe(x, o, wo_bf, g, router_pad, n_exp, wgu_bf, wd_bf, g_final):
    n, d = x.shape
    tm = _row_tile(n, 512)
    x3, h, route = _moe_route(x, o, wo_bf, g, router_pad, n_exp)
    slot1, slot2, src, tile_expert, n_used = _routing_tables(route, n_exp, tm)
    ys = _moe_experts(h, src.reshape(-1, 1, tm), tile_expert, n_used, wgu_bf, wd_bf, tm)
    return _moe_combine(x3, route, ys, slot1.reshape(-1, 1, tm), slot2.reshape(-1, 1, tm), g_final, tm)


def _lane_block(x, start):
    return jnp.pad(x, ((0, 0), (start, LANES - start - x.shape[1])))


def _swap_halves(x):
    half = x.shape[-1] // 2
    return jnp.concatenate([x[..., half:], x[..., :half]], axis=-1)


def _prep_mla_weights(w_a, g_q, g_kv, w_uq, w_ukv, q_lora, kv_lora, n_heads, nope, rope):
    w_pe = w_a[:, q_lora + kv_lora:]
    w_pe_sw = _swap_halves(w_pe)
    wpe = jnp.concatenate([_lane_block(w_pe, 0), _lane_block(w_pe_sw, 0),
                           _lane_block(w_pe, nope), _lane_block(w_pe_sw, nope)], axis=1)
    uq = w_uq.reshape(q_lora, n_heads, nope + rope)
    uq_nope, uq_pe = uq[..., :nope], uq[..., nope:]
    tail = jnp.zeros((q_lora, n_heads, LANES - nope - rope), w_uq.dtype)
    wuq = jnp.concatenate([uq_nope, uq_pe, tail], axis=-1).reshape(q_lora, n_heads * LANES)
    wuqs = jnp.concatenate([jnp.zeros_like(uq_nope), _swap_halves(uq_pe), tail],
                           axis=-1).reshape(q_lora, n_heads * LANES)
    ukv = w_ukv.reshape(kv_lora, n_heads, -1)
    uk = ukv[..., :nope]
    wuk = jnp.concatenate([uk, jnp.zeros((kv_lora, n_heads, LANES - nope), uk.dtype)],
                          axis=-1).reshape(kv_lora, n_heads * LANES)
    wuv = ukv[..., nope:].reshape(kv_lora, -1)
    bf = lambda a: a.astype(BF16)
    return {"waq": bf(w_a[:, :q_lora]), "wakv": bf(w_a[:, q_lora:q_lora + kv_lora]), "wpe": bf(wpe),
            "gq": g_q[None, :], "gkv": g_kv[None, :], "wuq": bf(wuq), "wuqs": bf(wuqs),
            "wuk": bf(wuk), "wuv": bf(wuv)}


def _rope_tables(pos, nope, rope):
    half = rope // 2
    freqs = jnp.power(ROPE_THETA, -jnp.arange(half, dtype=F32) * 2.0 / rope)
    ang = pos.astype(F32)[:, None] * freqs[None, :]
    cos = jnp.concatenate([jnp.cos(ang)] * 2, axis=1)
    sin = jnp.concatenate([-jnp.sin(ang), jnp.sin(ang)], axis=1)
    c64 = _lane_block(cos, nope) + _lane_block(jnp.ones((pos.shape[0], nope), F32), 0)
    return _lane_block(cos, 0), _lane_block(sin, 0), c64, _lane_block(sin, nope)


def _bf16_split3(x):
    hi = x.astype(BF16).astype(F32)
    mid = (x - hi).astype(BF16).astype(F32)
    lo = (x - hi - mid).astype(BF16).astype(F32)
    return hi, mid, lo


def kernel(x_prompt, x_sample, cache_diff_k, cache_diff_v, cache_mla_ckv, cache_mla_kpe, norm_mix, norm_ffn, norm_final, diff_w_qkv, diff_lambda, diff_subln, diff_w_o, mla_w_a, mla_norm_q, mla_norm_kv, mla_w_uq, mla_w_ukv, mla_w_o, ffn_w_gu, ffn_w_down, moe_router, moe_w_gu, moe_w_down):
    b, t, d = x_prompt.shape
    db, nq, _ = x_sample.shape
    past = cache_diff_k.shape[2]
    n_da_heads, dv = cache_diff_k.shape[3], diff_subln.shape[-1]
    da_width = n_da_heads * dv
    q_lora, kv_lora = mla_norm_q.shape[-1], mla_norm_kv.shape[-1]
    rope = mla_w_a.shape[-1] - q_lora - kv_lora
    mla_width = mla_w_o.shape[1]
    n_exp = moe_router.shape[-1]
    assert norm_mix.shape[0] == 2, "one differential-attention layer followed by one MLA layer"

    uq_cols, ukv_cols = mla_w_uq.shape[-1], mla_w_ukv.shape[-1]
    n_mla_heads = (uq_cols - (ukv_cols - mla_width)) // rope
    nope = uq_cols // n_mla_heads - rope
    mla_v = mla_width // n_mla_heads
    assert n_mla_heads * (nope + mla_v) == ukv_cols and n_mla_heads % 2 == 0
    assert nope + rope <= LANES and 2 * mla_v == LANES

    bf = lambda a: a.astype(BF16)
    lam_init = 0.8 - 0.6 * math.exp(-0.3 * 0)
    da_scale = (dv // 2) ** -0.5 * LOG2E
    mla_scale = (nope + rope) ** -0.5 * LOG2E
    slopes2 = jnp.exp2(-8.0 * jnp.arange(1, n_da_heads + 1, dtype=F32) / n_da_heads) * LOG2E
    slope_parts = jnp.stack(_bf16_split3(slopes2) + (slopes2,), axis=1).reshape(-1)
    slope_rows = jnp.repeat(slopes2, 2 * nq)[:, None]

    w_qkv, w_o0 = bf(diff_w_qkv[0]), bf(diff_w_o[0])
    w_gu0, w_d0 = bf(ffn_w_gu[0]), bf(ffn_w_down[0])
    mla_w = _prep_mla_weights(mla_w_a[0], mla_norm_q[0], mla_norm_kv[0], mla_w_uq[0], mla_w_ukv[0],
                              q_lora, kv_lora, n_mla_heads, nope, rope)
    w_o1 = bf(mla_w_o[0])
    router_pad = jnp.pad(moe_router[0], ((0, 0), (0, LANES - n_exp)))
    moe_gu, moe_d = bf(moe_w_gu.reshape(moe_w_gu.shape[1:])), bf(moe_w_down.reshape(moe_w_down.shape[1:]))
    g_mix, g_ffn, g_fin = norm_mix[:, None, :], norm_ffn[:, None, :], norm_final[None, :]
    lam, subln = diff_lambda[0], diff_subln[0][None, :]

    def trunk(x, pos, attn0, attn1):
        bx, tx, _ = x.shape
        x2d = x.reshape(bx * tx, d)
        q, k, v, kx, vtx = _qkv_proj(x, g_mix[0], w_qkv, dv, da_scale)
        o = attn0(q, k, v, kx, vtx)
        x2 = _out_ffn(x2d, o.reshape(bx * tx, da_width), w_o0, g_ffn[0], w_gu0, w_d0)
        ckv, kpe, pe64, qcat, kcat, vtx1 = _mla_proj(x2.reshape(bx, tx, d), g_mix[1], mla_w,
                                                    _rope_tables(pos, nope, rope), rope, mla_v,
                                                    mla_scale)
        o = attn1(ckv, pe64, qcat, kcat, vtx1)
        y = _out_moe(x2, o.reshape(bx * tx, mla_width), w_o1, g_ffn[1], router_pad, n_exp,
                     moe_gu, moe_d, g_fin)
        return y, k, v, ckv, kpe

    y_p, k_p, v_p, ckv_p, kpe_p = trunk(
        x_prompt, jnp.arange(t, dtype=jnp.int32),
        lambda q, k, v, kx, vtx: _diff_attn_prompt(q, kx, vtx, lam, subln, slope_parts, lam_init),
        lambda ckv, pe64, qcat, kcat, vtx1: _mla_attn_prompt(qcat, kcat, vtx1, mla_v))

    n_s = db * nq
    cache_k = cache_diff_k[0].reshape(db, past, da_width)
    cache_v = cache_diff_v[0].reshape(db, past, da_width)
    cache_pe = jnp.pad(cache_mla_kpe[0], ((0, 0), (0, 0), (nope, LANES - nope - rope)))
    pos_s = past + jnp.tile(jnp.arange(nq, dtype=jnp.int32), db)

    def attn0_s(q, k, v, kx, vtx):
        r = lambda a: a.reshape(db, nq, da_width)
        return _diff_attn_decode(r(q), cache_k, cache_v, r(k), r(v), lam, subln, slope_rows, lam_init)

    def attn1_s(ckv, pe64, qcat, kcat, vtx1):
        return _mla_attn_decode(qcat.reshape(db, nq, -1), cache_mla_ckv[0], cache_pe,
                                ckv.reshape(db, nq, kv_lora), pe64.reshape(db, nq, LANES),
                                mla_w["wuk"], mla_w["wuv"])

    y_s, k_s, v_s, ckv_s, kpe_s = trunk(x_sample.reshape(1, n_s, d), pos_s, attn0_s, attn1_s)

    return (y_p.reshape(b, t, d), y_s.reshape(db, nq, d),
            k_p.reshape(1, b, t, n_da_heads, dv), v_p.reshape(1, b, t, n_da_heads, dv),
            ckv_p.reshape(1, b, t, kv_lora), kpe_p.reshape(1, b, t, rope),
            k_s.reshape(1, db, nq, n_da_heads, dv), v_s.reshape(1, db, nq, n_da_heads, dv),
            ckv_s.reshape(1, db, nq, kv_lora), kpe_s.reshape(1, db, nq, rope))
```

```python
import functools
import math

import jax
import jax.numpy as jnp
from jax import lax
from jax.experimental import pallas as pl
from jax.experimental.pallas import tpu as pltpu

NORM_EPS = 1e-6
NEG_INF = -1e30
CHUNK = 64
ROPE_THETA = 10000.0
LOG2E = math.log2(math.e)
LANES = 128
MXU_DIM = 256
BF16_ROWS = 16
EXPERT_CHUNK_CAP = 1024
FLASH_Q_BLOCK = 1024
VMEM_LIMIT_BYTES = 56 * 2**20
BF16 = jnp.bfloat16
F32 = jnp.float32


def _params(*sem):
    return pltpu.CompilerParams(dimension_semantics=sem, vmem_limit_bytes=VMEM_LIMIT_BYTES)


def _const_spec(shape):
    nd = len(shape)
    return pl.BlockSpec(shape, lambda *_: (0,) * nd, pipeline_mode=pl.Buffered(1))


def _rms(x, g):
    ms = jnp.mean(x * x, axis=-1, keepdims=True)
    return x * lax.rsqrt(ms + NORM_EPS) * g


def _dot(a, b):
    return jnp.dot(a, b, preferred_element_type=F32)


def _dot_nt(a, b):
    return lax.dot_general(a, b, (((1,), (1,)), ((), ())), preferred_element_type=F32)


def _row_tile(t, cap):
    tm = min(t, cap)
    assert t % tm == 0
    return tm


def _ones_row_block(n_cols):
    sub = lax.broadcasted_iota(jnp.int32, (BF16_ROWS, n_cols), 0)
    return jnp.where(sub == 0, 1.0, 0.0).astype(BF16)


def _qkv_kernel(x_ref, g_ref, w_ref, q_ref, k_ref, v_ref, kx_ref, vtx_ref, *, width, dv, scale):
    tm = x_ref.shape[1]
    h = _rms(x_ref[0], g_ref[...]).astype(BF16)
    qkv = _dot(h, w_ref[...])
    q_ref[0] = (qkv[:, :width] * scale).astype(BF16)
    k = qkv[:, width:2 * width]
    v = qkv[:, 2 * width:]
    k_ref[0] = k
    v_ref[0] = v
    k_bf = k.astype(BF16)
    vt = v.T.astype(BF16)
    pos = lax.broadcasted_iota(jnp.int32, (tm, LANES), 0) + pl.program_id(1) * tm
    lane = lax.broadcasted_iota(jnp.int32, (tm, LANES), 1)
    pos_feat = jnp.where(lane < 3, pos // LANES, jnp.where(lane < 6, pos % LANES, 0))
    pos_feat = pos_feat.astype(F32).astype(BF16)
    ones = _ones_row_block(tm)
    for hh in range(width // dv):
        kx_ref[0, :, 2 * dv * hh:2 * dv * hh + dv] = k_bf[:, hh * dv:(hh + 1) * dv]
        kx_ref[0, :, 2 * dv * hh + dv:2 * dv * (hh + 1)] = pos_feat
        r0 = hh * (dv + BF16_ROWS)
        vtx_ref[0, r0:r0 + dv, :] = vt[hh * dv:(hh + 1) * dv, :]
        vtx_ref[0, r0 + dv:r0 + dv + BF16_ROWS, :] = ones


def _qkv_proj(x, g, w_bf, dv, scale):
    bx, tx, d = x.shape
    width = w_bf.shape[1] // 3
    n_heads = width // dv
    assert dv == LANES and tx <= LANES * MXU_DIM
    tm = _row_tile(tx, 512)
    row = lambda n: pl.BlockSpec((1, tm, n), lambda b, t: (b, t, 0))
    vrows = n_heads * (dv + BF16_ROWS)
    return pl.pallas_call(
        functools.partial(_qkv_kernel, width=width, dv=dv, scale=scale),
        grid=(bx, tx // tm),
        in_specs=[row(d), _const_spec((1, d)), _const_spec(w_bf.shape)],
        out_specs=[row(width), row(width), row(width), row(2 * width),
                   pl.BlockSpec((1, vrows, tm), lambda b, t: (b, 0, t))],
        out_shape=[jax.ShapeDtypeStruct((bx, tx, width), BF16),
                   jax.ShapeDtypeStruct((bx, tx, width), F32),
                   jax.ShapeDtypeStruct((bx, tx, width), F32),
                   jax.ShapeDtypeStruct((bx, tx, 2 * width), BF16),
                   jax.ShapeDtypeStruct((bx, vrows, tx), BF16)],
        compiler_params=_params("parallel", "parallel"),
        name="qkv_proj",
    )(x, g, w_bf)


def _flash_streams(i, tq, key_op, query_op, value_op, diag_tile, z_a, z_b, cm_a, cm_b, m, acc):
    n_streams = 2
    tk = tq // 2

    units = [(s, slice(n * MXU_DIM, (n + 1) * MXU_DIM))
             for s in range(n_streams) for n in range(tq // MXU_DIM)]

    def produce(blk, z, cm, extra, s, cols):
        off = pl.multiple_of(blk * tk, tk)
        scores = _dot_nt(key_op(s, off), query_op(s, cols))
        if extra is not None:
            scores = scores + extra[:, cols]
        z[s, :, cols] = scores
        cm[s, :, cols] = jnp.max(scores, axis=0, keepdims=True)

    def consume(blk, z, cm, s, cols):
        off = pl.multiple_of(blk * tk, tk)
        m_old = m[s, :, cols]
        m_new = jnp.maximum(m_old, cm[s, :, cols])
        p = jnp.exp2(z[s, :, cols] - m_new).astype(BF16)
        acc[s, :, cols] = jnp.exp2(m_old - m_new) * acc[s, :, cols] + _dot(value_op(s, off), p)
        m[s, :, cols] = m_new

    def step(p_blk, p_z, p_cm, extra, c_blk, c_z, c_cm):
        for s, cols in units:
            produce(p_blk, p_z, p_cm, extra, s, cols)
            consume(c_blk, c_z, c_cm, s, cols)

    m[...] = jnp.full(m.shape, NEG_INF, F32)
    acc[...] = jnp.zeros(acc.shape, F32)
    first_tile = diag_tile(0)
    for s, cols in units:
        produce(2 * i, z_a, cm_a, first_tile, s, cols)

    def pair(u):
        step(2 * u, z_b, cm_b, None, jnp.where(u == 0, 2 * i, 2 * u - 1), z_a, cm_a)
        step(2 * u + 1, z_a, cm_a, None, 2 * u, z_b, cm_b)

    odd = i % 2

    @pl.when(odd == 1)
    def _():
        pair(0)

    def body(v, carry):
        pair(odd + 2 * v)
        pair(odd + 2 * v + 1)
        return carry

    lax.fori_loop(0, i // 2, body, 0)

    last_tile = diag_tile(1)
    last_a = jnp.where(i == 0, 0, 2 * i - 1)
    late_units = [(s, cols) for s, cols in units if cols.stop > tk]
    for s, cols in units:
        if cols.stop > tk:
            produce(2 * i + 1, z_b, cm_b, last_tile, s, cols)
        consume(last_a, z_a, cm_a, s, cols)
    for s, cols in late_units:
        consume(2 * i + 1, z_b, cm_b, s, cols)


def _local_positions(tq, n):
    tk = tq // 2
    kl = lax.broadcasted_iota(jnp.int32, (tk, tq), 0) + n * tk
    ql = lax.broadcasted_iota(jnp.int32, (tk, tq), 1)
    return kl, ql, (kl // CHUNK) <= (ql // CHUNK)


def _lambda_full(lam_ref, lam_init):
    lf = lam_ref[...]
    s01 = jnp.sum(lf[0:1, :] * lf[1:2, :], axis=-1, keepdims=True)
    s23 = jnp.sum(lf[2:3, :] * lf[3:4, :], axis=-1, keepdims=True)
    return jnp.exp(s01) - jnp.exp(s23) + lam_init


def _diff_flash_kernel(slope_ref, q_ref, kx_ref, vtx_ref, lam_ref, subln_ref, o_ref,
                       qx, z_a, z_b, cm_a, cm_b, m, acc, *, tq, dv, lam_init):
    h = pl.program_id(1)
    i = pl.program_id(2)
    s_hi, s_mid, s_lo, slope2 = (slope_ref[4 * h + n] for n in range(4))
    q = q_ref[0]
    lane = lax.broadcasted_iota(jnp.int32, q.shape, 1)
    zero = jnp.zeros_like(q)
    slope_feat = jnp.zeros(q.shape, F32)
    for n, val in enumerate((LANES * s_hi, LANES * s_mid, LANES * s_lo, s_hi, s_mid, s_lo)):
        slope_feat = jnp.where(lane == n, val, slope_feat)
    slope_feat = slope_feat.astype(BF16)
    qx[0, :, 0:dv] = jnp.where(lane < dv // 2, q, zero)
    qx[1, :, 0:dv] = jnp.where(lane >= dv // 2, q, zero)
    qx[0, :, dv:2 * dv] = slope_feat
    qx[1, :, dv:2 * dv] = slope_feat

    def diag_tile(n):
        kl, ql, visible = _local_positions(tq, n)
        return jnp.where(visible, (-2.0 * slope2) * jnp.maximum(kl - ql, 0).astype(F32), NEG_INF)

    tk = tq // 2
    _flash_streams(i, tq,
                   key_op=lambda s, off: kx_ref[0, pl.ds(off, tk), :],
                   query_op=lambda s, cols: qx[s, cols, :],
                   value_op=lambda s, off: vtx_ref[0, :, pl.ds(off, tk)],
                   diag_tile=diag_tile, z_a=z_a, z_b=z_b, cm_a=cm_a, cm_b=cm_b, m=m, acc=acc)

    lam = _lambda_full(lam_ref, lam_init)
    o_t = (acc[0, 0:dv, :] / acc[0, dv:dv + 1, :]
           - lam * (acc[1, 0:dv, :] / acc[1, dv:dv + 1, :]))
    o = _rms(o_t.T, subln_ref[...]) * (1.0 - lam_init)
    o_ref[0] = o.astype(BF16)


def _flash_scratch(tq, q_feat, acc_rows):
    return ([pltpu.VMEM((2, tq, q_feat), BF16)] if q_feat else []) + [
        pltpu.VMEM((2, tq // 2, tq), F32), pltpu.VMEM((2, tq // 2, tq), F32),
        pltpu.VMEM((2, 1, tq), F32), pltpu.VMEM((2, 1, tq), F32),
        pltpu.VMEM((2, 1, tq), F32), pltpu.VMEM((2, acc_rows, tq), F32)]


def _diff_attn_prompt(q, kx, vtx, lam, subln, slope_parts, lam_init):
    b, t, width = q.shape
    dv = subln.shape[-1]
    n_heads = width // dv
    tq = _row_tile(t, FLASH_Q_BLOCK)
    assert tq % (2 * MXU_DIM) == 0
    vrows = dv + BF16_ROWS
    return pl.pallas_call(
        functools.partial(_diff_flash_kernel, tq=tq, dv=dv, lam_init=lam_init),
        grid=(b, n_heads, t // tq),
        in_specs=[pl.BlockSpec(memory_space=pltpu.SMEM),
                  pl.BlockSpec((1, tq, dv), lambda b_, h, i: (b_, i, h)),
                  pl.BlockSpec((1, t, 2 * dv), lambda b_, h, i: (b_, 0, h)),
                  pl.BlockSpec((1, vrows, t), lambda b_, h, i: (b_, h, 0)),
                  pl.BlockSpec(lam.shape, lambda b_, h, i: (0, 0)),
                  pl.BlockSpec(subln.shape, lambda b_, h, i: (0, 0))],
        out_specs=pl.BlockSpec((1, tq, dv), lambda b_, h, i: (b_, i, h)),
        out_shape=jax.ShapeDtypeStruct((b, t, width), BF16),
        scratch_shapes=_flash_scratch(tq, 2 * dv, vrows),
        compiler_params=_params("parallel", "parallel", "arbitrary"),
        name="diff_attn_prompt",
    )(slope_parts, q, kx, vtx, lam, subln)


def _mla_flash_kernel(q_ref, k_ref, vtx_ref, o_ref, z_a, z_b, cm_a, cm_b, m, acc, *, tq, dv):
    i = pl.program_id(2)
    vrows = dv + BF16_ROWS
    tk = tq // 2
    _flash_streams(i, tq,
                   key_op=lambda s, off: k_ref[0, pl.ds(off, tk), s * LANES:(s + 1) * LANES],
                   query_op=lambda s, cols: q_ref[0, cols, s * LANES:(s + 1) * LANES],
                   value_op=lambda s, off: vtx_ref[0, s * vrows:(s + 1) * vrows, pl.ds(off, tk)],
                   diag_tile=lambda n: jnp.where(_local_positions(tq, n)[2], 0.0, NEG_INF),
                   z_a=z_a, z_b=z_b, cm_a=cm_a, cm_b=cm_b, m=m, acc=acc)
    o_t = jnp.concatenate([acc[s, 0:dv, :] / acc[s, dv:dv + 1, :] for s in range(2)], axis=0)
    o_ref[0] = o_t.T.astype(BF16)


def _mla_attn_prompt(qcat, kcat, vtx, dv):
    b, t, wq = qcat.shape
    n_pairs = wq // (2 * LANES)
    tq = _row_tile(t, FLASH_Q_BLOCK)
    assert tq % (2 * MXU_DIM) == 0
    vrows = dv + BF16_ROWS
    return pl.pallas_call(
        functools.partial(_mla_flash_kernel, tq=tq, dv=dv),
        grid=(b, n_pairs, t // tq),
        in_specs=[pl.BlockSpec((1, tq, 2 * LANES), lambda b_, h, i: (b_, i, h)),
                  pl.BlockSpec((1, t, 2 * LANES), lambda b_, h, i: (b_, 0, h)),
                  pl.BlockSpec((1, 2 * vrows, t), lambda b_, h, i: (b_, h, 0))],
        out_specs=pl.BlockSpec((1, tq, 2 * dv), lambda b_, h, i: (b_, i, h)),
        out_shape=jax.ShapeDtypeStruct((b, t, n_pairs * 2 * dv), BF16),
        scratch_shapes=_flash_scratch(tq, 0, vrows),
        compiler_params=_params("parallel", "parallel", "arbitrary"),
        name="mla_attn_prompt",
    )(qcat, kcat, vtx)


def _block_diag_queries(q, n_groups, group_width):
    qt = jnp.concatenate([q] * n_groups, axis=0)
    r = lax.broadcasted_iota(jnp.int32, qt.shape, 0)
    c = lax.broadcasted_iota(jnp.int32, qt.shape, 1)
    return jnp.where((c // group_width) == (r // q.shape[0]), qt, jnp.zeros_like(qt))


def _decode_update(z, v_bf, m_ref, l_ref, acc_ref):
    m_old = m_ref[...]
    m_new = jnp.maximum(m_old, jnp.max(z, axis=-1, keepdims=True))
    alpha = jnp.exp2(m_old - m_new)
    p = jnp.exp2(z - m_new)
    l_ref[...] = alpha * l_ref[...] + jnp.sum(p, axis=-1, keepdims=True)
    acc_ref[...] = alpha * acc_ref[...] + _dot(p.astype(BF16), v_bf)
    m_ref[...] = m_new


def _decode_positions(n_rows, n_keys, nq, past, key_start, n_valid):
    r = lax.broadcasted_iota(jnp.int32, (n_rows, n_keys), 0)
    c = lax.broadcasted_iota(jnp.int32, (n_rows, n_keys), 1)
    q_pos = past + r % nq
    k_pos = key_start + c
    visible = ((k_pos // CHUNK) <= (q_pos // CHUNK)) & (c < n_valid)
    return q_pos, k_pos, visible


def _pad_rows(x, n):
    return jnp.concatenate([x, jnp.zeros((n - x.shape[0], x.shape[1]), x.dtype)], axis=0)


def _diff_decode_kernel(slope_ref, q_ref, ck_ref, cv_ref, kn_ref, vn_ref, lam_ref, subln_ref,
                        o_ref, qrows, qbd, bias, m, l, acc, *, past, tk, nq, dv, lam_init):
    j = pl.program_id(1)
    n_rows = qrows.shape[0]
    n_heads = n_rows // (2 * nq)
    n_cols = tk * n_heads

    @pl.when(j == 0)
    def _():
        q = q_ref[0]
        qbd[...] = _block_diag_queries(q, 2 * n_heads, dv // 2)
        lane = lax.broadcasted_iota(jnp.int32, (nq, dv), 1)
        for hh in range(n_heads):
            q_h = q[:, hh * dv:(hh + 1) * dv]
            r0 = hh * 2 * nq
            qrows[r0:r0 + nq, :] = jnp.where(lane < dv // 2, q_h, jnp.zeros_like(q_h))
            qrows[r0 + nq:r0 + 2 * nq, :] = jnp.where(lane >= dv // 2, q_h, jnp.zeros_like(q_h))
        r = lax.broadcasted_iota(jnp.int32, (n_rows, n_cols), 0)
        c = lax.broadcasted_iota(jnp.int32, (n_rows, n_cols), 1)
        own_head = (c % n_heads) == (r // (2 * nq))
        rel = (c // n_heads - r % nq).astype(F32)
        bias[...] = jnp.where(own_head, slope_ref[...] * rel, NEG_INF)
        m[...] = jnp.full(m.shape, NEG_INF, F32)
        l[...] = jnp.zeros(l.shape, F32)
        acc[...] = jnp.zeros(acc.shape, F32)

    def update(z, pv):
        m_old = m[...]
        m_new = jnp.maximum(m_old, jnp.max(z, axis=-1, keepdims=True))
        alpha = jnp.exp2(m_old - m_new)
        p = jnp.exp2(z - m_new)
        l[...] = alpha * l[...] + jnp.sum(p, axis=-1, keepdims=True)
        acc[...] = alpha * acc[...] + pv(p.astype(BF16))
        m[...] = m_new

    k2 = ck_ref[0].reshape(n_cols, dv).astype(BF16)
    v2 = cv_ref[0].reshape(n_cols, dv).astype(BF16)
    shift = slope_ref[...] * (past - j * tk).astype(F32)
    update(_dot_nt(qrows[...], k2) + bias[...] - shift, lambda p: _dot(p, v2))

    @pl.when(j == pl.num_programs(1) - 1)
    def _():
        k_new = _pad_rows(kn_ref[0], LANES).astype(BF16)
        v_new = _pad_rows(vn_ref[0], LANES).astype(BF16)
        q_pos, k_pos, visible = _decode_positions(n_rows, LANES, nq, past, past, nq)
        dist = jnp.abs(q_pos - k_pos).astype(F32)
        z = jnp.where(visible, _dot_nt(qbd[...], k_new) - slope_ref[...] * dist, NEG_INF)

        def pv_new(p):
            full = _dot(p, v_new)
            return jnp.concatenate([full[hh * 2 * nq:(hh + 1) * 2 * nq, hh * dv:(hh + 1) * dv]
                                    for hh in range(n_heads)], axis=0)

        update(z, pv_new)
        lam = _lambda_full(lam_ref, lam_init)
        for hh in range(n_heads):
            r0 = hh * 2 * nq
            o1 = acc[r0:r0 + nq, :] / l[r0:r0 + nq, :]
            o2 = acc[r0 + nq:r0 + 2 * nq, :] / l[r0 + nq:r0 + 2 * nq, :]
            o = _rms(o1 - lam * o2, subln_ref[...]) * (1.0 - lam_init)
            o_ref[0, :, hh * dv:(hh + 1) * dv] = o.astype(BF16)


def _diff_attn_decode(q, cache_k, cache_v, k_new, v_new, lam, subln, slope_rows, lam_init):
    db, nq, width = q.shape
    past, n_heads, dv = cache_k.shape[1:]
    n_rows = n_heads * 2 * nq
    tk = _row_tile(past, 512)
    stream = lambda n: pl.BlockSpec((1, nq, n), lambda b, j: (b, 0, 0))
    cache = pl.BlockSpec((1, tk, n_heads, dv), lambda b, j: (b, j, 0, 0))
    return pl.pallas_call(
        functools.partial(_diff_decode_kernel, past=past, tk=tk, nq=nq, dv=dv, lam_init=lam_init),
        grid=(db, past // tk),
        in_specs=[_const_spec(slope_rows.shape), stream(width), cache, cache, stream(width),
                  stream(width), _const_spec(lam.shape), _const_spec(subln.shape)],
        out_specs=stream(width),
        out_shape=jax.ShapeDtypeStruct((db, nq, width), BF16),
        scratch_shapes=[pltpu.VMEM((n_rows, dv), BF16), pltpu.VMEM((n_rows, width), BF16),
                        pltpu.VMEM((n_rows, tk * n_heads), F32), pltpu.VMEM((n_rows, 1), F32),
                        pltpu.VMEM((n_rows, 1), F32), pltpu.VMEM((n_rows, dv), F32)],
        compiler_params=_params("parallel", "arbitrary"),
        name="diff_attn_decode",
    )(slope_rows, q, cache_k, cache_v, k_new, v_new, lam, subln)


def _mla_decode_kernel(q_ref, cc_ref, cpe_ref, cn_ref, pen_ref, wuk_ref, wuv_ref, o_ref,
                       qbd, m, l, acc, *, past, tk, nq, dv):
    j = pl.program_id(1)
    n_rows = qbd.shape[0]
    n_heads = n_rows // nq

    @pl.when(j == 0)
    def _():
        qbd[...] = _block_diag_queries(q_ref[0], n_heads, LANES)
        m[...] = jnp.full(m.shape, NEG_INF, F32)
        l[...] = jnp.zeros(l.shape, F32)
        acc[...] = jnp.zeros(acc.shape, F32)

    def step(c_f32, pe_f32, key_start, n_valid):
        n_keys = c_f32.shape[0]
        c_bf = c_f32.astype(BF16)
        kcat = (_dot(c_bf, wuk_ref[...]) + jnp.concatenate([pe_f32] * n_heads, axis=1)).astype(BF16)
        v_bf = _dot(c_bf, wuv_ref[...]).astype(BF16)
        s = _dot_nt(qbd[...], kcat)
        _, _, visible = _decode_positions(n_rows, n_keys, nq, past, key_start, n_valid)
        _decode_update(jnp.where(visible, s, NEG_INF), v_bf, m, l, acc)

    step(cc_ref[0], cpe_ref[0], j * tk, tk)

    @pl.when(j == pl.num_programs(1) - 1)
    def _():
        step(_pad_rows(cn_ref[0], LANES), _pad_rows(pen_ref[0], LANES), past, nq)
        lane = lax.broadcasted_iota(jnp.int32, (nq, 2 * dv), 1)
        for pair in range(n_heads // 2):
            r0 = pair * 2 * nq
            cols = slice(pair * 2 * dv, (pair + 1) * 2 * dv)
            even = acc[r0:r0 + nq, cols] / l[r0:r0 + nq, :]
            odd = acc[r0 + nq:r0 + 2 * nq, cols] / l[r0 + nq:r0 + 2 * nq, :]
            o_ref[0, :, cols] = jnp.where(lane < dv, even, odd).astype(BF16)


def _mla_attn_decode(qcat, cache_c, cache_pe, c_new, pe_new, wuk_bf, wuv_bf):
    db, nq, wq = qcat.shape
    past, lora = cache_c.shape[1], cache_c.shape[2]
    n_heads = wq // LANES
    wv = wuv_bf.shape[1]
    dv = wv // n_heads
    n_rows = n_heads * nq
    tk = _row_tile(past, 512)
    stream = lambda n: pl.BlockSpec((1, nq, n), lambda b, j: (b, 0, 0))
    return pl.pallas_call(
        functools.partial(_mla_decode_kernel, past=past, tk=tk, nq=nq, dv=dv),
        grid=(db, past // tk),
        in_specs=[stream(wq),
                  pl.BlockSpec((1, tk, lora), lambda b, j: (b, j, 0)),
                  pl.BlockSpec((1, tk, LANES), lambda b, j: (b, j, 0)),
                  stream(lora), stream(LANES),
                  _const_spec(wuk_bf.shape), _const_spec(wuv_bf.shape)],
        out_specs=stream(wv),
        out_shape=jax.ShapeDtypeStruct((db, nq, wv), BF16),
        scratch_shapes=[pltpu.VMEM((n_rows, wq), BF16), pltpu.VMEM((n_rows, 1), F32),
                        pltpu.VMEM((n_rows, 1), F32), pltpu.VMEM((n_rows, wv), F32)],
        compiler_params=_params("parallel", "arbitrary"),
        name="mla_attn_decode",
    )(qcat, cache_c, cache_pe, c_new, pe_new, wuk_bf, wuv_bf)


def _silu(x):
    return x * jax.nn.sigmoid(x)


def _out_ffn_kernel(x_ref, o_ref, wo_ref, g_ref, wgu_ref, wd_ref, out_ref, *, f, n_chunks):
    x1 = x_ref[...] + _dot(o_ref[...], wo_ref[...])
    h = _rms(x1, g_ref[...]).astype(BF16)
    fc = f // n_chunks
    acc = x1
    for c in range(n_chunks):
        gate = _dot(h, wgu_ref[:, c * fc:(c + 1) * fc])
        up = _dot(h, wgu_ref[:, f + c * fc:f + (c + 1) * fc])
        acc = acc + _dot((_silu(gate) * up).astype(BF16), wd_ref[c * fc:(c + 1) * fc, :])
    out_ref[...] = acc


def _out_ffn(x, o, wo_bf, g, wgu_bf, wd_bf):
    n, d = x.shape
    f = wd_bf.shape[0]
    tm = _row_tile(n, 512)
    n_chunks = 2 if f % (2 * LANES) == 0 else 1
    row = lambda w: pl.BlockSpec((tm, w), lambda t: (t, 0))
    return pl.pallas_call(
        functools.partial(_out_ffn_kernel, f=f, n_chunks=n_chunks),
        grid=(n // tm,),
        in_specs=[row(d), row(o.shape[1]), _const_spec(wo_bf.shape), _const_spec((1, d)),
                  _const_spec(wgu_bf.shape), _const_spec(wd_bf.shape)],
        out_specs=row(d),
        out_shape=jax.ShapeDtypeStruct((n, d), F32),
        compiler_params=_params("parallel"),
        name="attn_out_ffn",
    )(x, o, wo_bf, g, wgu_bf, wd_bf)


def _mla_proj_kernel(x_ref, g_ref, waq_ref, wakv_ref, wpe_ref, gq_ref, gkv_ref, wuq_ref, wuqs_ref,
                     wuk_ref, wuv_ref, c0_ref, s0_ref, c64_ref, s64_ref,
                     ckv_ref, kpe_ref, pe64_ref, q_ref, k_ref, vtx_ref, *, rope, dv, scale):
    tm = x_ref.shape[1]
    h = _rms(x_ref[0], g_ref[...]).astype(BF16)
    c_q = _rms(_dot(h, waq_ref[...]), gq_ref[...]).astype(BF16)
    c_kv = _rms(_dot(h, wakv_ref[...]), gkv_ref[...])
    ckv_ref[0] = c_kv
    c_kv_bf = c_kv.astype(BF16)
    pe = _dot(h, wpe_ref[...])
    kpe0 = pe[:, 0:LANES] * c0_ref[...] + pe[:, LANES:2 * LANES] * s0_ref[...]
    kpe_ref[0] = kpe0[:, :rope]
    c64 = c64_ref[...]
    s64 = s64_ref[...]
    pe64 = pe[:, 2 * LANES:3 * LANES] * c64 + pe[:, 3 * LANES:] * s64
    pe64_ref[0] = pe64
    q = _dot(c_q, wuq_ref[...])
    q_sw = _dot(c_q, wuqs_ref[...])
    k = _dot(c_kv_bf, wuk_ref[...])
    for hh in range(q.shape[1] // LANES):
        cols = slice(hh * LANES, (hh + 1) * LANES)
        q_ref[0, :, cols] = ((q[:, cols] * c64 + q_sw[:, cols] * s64) * scale).astype(BF16)
        k_ref[0, :, cols] = (k[:, cols] + pe64).astype(BF16)
    vt = _dot(c_kv_bf, wuv_ref[...]).T.astype(BF16)
    ones = _ones_row_block(tm)
    for hh in range(vt.shape[0] // dv):
        r0 = hh * (dv + BF16_ROWS)
        vtx_ref[0, r0:r0 + dv, :] = vt[hh * dv:(hh + 1) * dv, :]
        vtx_ref[0, r0 + dv:r0 + dv + BF16_ROWS, :] = ones


def _mla_proj(x, g, w, tabs, rope, dv, scale):
    bx, tx, d = x.shape
    tm = _row_tile(tx, 256)
    lora = w["wakv"].shape[1]
    wq, wv = w["wuq"].shape[1], w["wuv"].shape[1]
    vrows = (wv // dv) * (dv + BF16_ROWS)
    row = lambda n: pl.BlockSpec((1, tm, n), lambda b, t: (b, t, 0))
    tab = pl.BlockSpec((tm, LANES), lambda b, t: (t, 0))
    weights = [w[k] for k in ("waq", "wakv", "wpe", "gq", "gkv", "wuq", "wuqs", "wuk", "wuv")]
    return pl.pallas_call(
        functools.partial(_mla_proj_kernel, rope=rope, dv=dv, scale=scale),
        grid=(bx, tx // tm),
        in_specs=[row(d), _const_spec((1, d))] + [_const_spec(a.shape) for a in weights] + [tab] * 4,
        out_specs=[row(lora), row(rope), row(LANES), row(wq), row(wq),
                   pl.BlockSpec((1, vrows, tm), lambda b, t: (b, 0, t))],
        out_shape=[jax.ShapeDtypeStruct((bx, tx, lora), F32),
                   jax.ShapeDtypeStruct((bx, tx, rope), F32),
                   jax.ShapeDtypeStruct((bx, tx, LANES), F32),
                   jax.ShapeDtypeStruct((bx, tx, wq), BF16),
                   jax.ShapeDtypeStruct((bx, tx, wq), BF16),
                   jax.ShapeDtypeStruct((bx, vrows, tx), BF16)],
        compiler_params=_params("parallel", "parallel"),
        name="mla_proj",
    )(x, g, *weights, *tabs)


def _route_kernel(x_ref, o_ref, wo_ref, g_ref, router_ref, x3_ref, h_ref, route_ref, *, n_exp):
    x3 = x_ref[...] + _dot(o_ref[...], wo_ref[...])
    h = _rms(x3, g_ref[...])
    lane = lax.broadcasted_iota(jnp.int32, route_ref.shape, 1).astype(F32)
    logits = jnp.dot(h, router_ref[...], preferred_element_type=F32,
                     precision=lax.Precision.HIGHEST)
    logits = jnp.where(lane < n_exp, logits, -jnp.inf)
    v1 = jnp.max(logits, axis=-1, keepdims=True)
    i1 = jnp.min(jnp.where(logits == v1, lane, float(LANES)), axis=-1, keepdims=True)
    rest = jnp.where(lane == i1, -jnp.inf, logits)
    v2 = jnp.max(rest, axis=-1, keepdims=True)
    i2 = jnp.min(jnp.where(rest == v2, lane, float(LANES)), axis=-1, keepdims=True)
    e2 = jnp.exp(v2 - v1)
    denom = 1.0 + e2
    route = jnp.zeros(route_ref.shape, F32)
    for n, val in enumerate((i1, i2, 1.0 / denom, e2 / denom)):
        route = jnp.where(lane == n, val, route)
    x3_ref[...] = x3
    h_ref[...] = h
    route_ref[...] = route


def _moe_route(x, o, wo_bf, g, router_pad, n_exp):
    n, d = x.shape
    tm = _row_tile(n, 512)
    row = lambda w: pl.BlockSpec((tm, w), lambda t: (t, 0))
    return pl.pallas_call(
        functools.partial(_route_kernel, n_exp=n_exp),
        grid=(n // tm,),
        in_specs=[row(d), row(o.shape[1]), _const_spec(wo_bf.shape), _const_spec((1, d)),
                  _const_spec(router_pad.shape)],
        out_specs=[row(d), row(d), row(LANES)],
        out_shape=[jax.ShapeDtypeStruct((n, d), F32), jax.ShapeDtypeStruct((n, d), F32),
                   jax.ShapeDtypeStruct((n, LANES), F32)],
        compiler_params=_params("parallel"),
        name="moe_route",
    )(x, o, wo_bf, g, router_pad)


def _gather_rows_start(idx_ref, src_hbm, dst, sem, n_rows):
    def body(t, carry):
        pltpu.make_async_copy(src_hbm.at[pl.ds(idx_ref[0, 0, t], 1)], dst.at[pl.ds(t, 1)], sem).start()
        return carry
    lax.fori_loop(0, n_rows, body, 0, unroll=8)


def _gather_rows_wait(src_hbm, dst, sem, n_rows):
    pltpu.make_async_copy(src_hbm.at[pl.ds(0, n_rows)], dst, sem).wait()


def _experts_kernel(te_ref, nu_ref, idx_ref, idx_next_ref, h_hbm, wg_ref, wu_ref, wd_ref, ys_ref,
                    hbuf, hb, sems, *, tm):
    r = pl.program_id(0)
    c = pl.program_id(1)
    n_used = nu_ref[0]
    slot = r % 2

    @pl.when((c == 0) & (r < n_used))
    def _():
        @pl.when(r == 0)
        def _():
            _gather_rows_start(idx_ref, h_hbm, hbuf.at[0], sems.at[0], tm)

        @pl.when(r + 1 < n_used)
        def _():
            _gather_rows_start(idx_next_ref, h_hbm, hbuf.at[1 - slot], sems.at[1 - slot], tm)

        _gather_rows_wait(h_hbm, hbuf.at[slot], sems.at[slot], tm)
        hb[...] = hbuf[slot].astype(BF16)

    @pl.when(r < n_used)
    def _():
        h = hb[...]
        act = (_silu(_dot(h, wg_ref[0])) * _dot(h, wu_ref[0])).astype(BF16)
        y = _dot(act, wd_ref[0])

        @pl.when(c == 0)
        def _():
            ys_ref[...] = y

        @pl.when(c > 0)
        def _():
            ys_ref[...] += y

    @pl.when((c == 0) & (r >= n_used))
    def _():
        ys_ref[...] = jnp.zeros(ys_ref.shape, F32)


def _moe_experts(h, src_tiles, tile_expert, n_used, wgu_bf, wd_bf, tm):
    n_tiles = src_tiles.shape[0]
    d = h.shape[1]
    f = wd_bf.shape[1]
    fc = max(c for c in range(LANES, EXPERT_CHUNK_CAP + 1, LANES) if f % c == 0)
    n_fc = f // fc
    chunk = lambda r, c, nu: jnp.where(r < nu[0], c, 0)
    grid_spec = pltpu.PrefetchScalarGridSpec(
        num_scalar_prefetch=2,
        grid=(n_tiles, n_fc),
        in_specs=[pl.BlockSpec((1, 1, tm), lambda r, c, te, nu: (r, 0, 0), memory_space=pltpu.SMEM),
                  pl.BlockSpec((1, 1, tm), lambda r, c, te, nu: (jnp.minimum(r + 1, n_tiles - 1), 0, 0),
                               memory_space=pltpu.SMEM),
                  pl.BlockSpec(memory_space=pl.ANY),
                  pl.BlockSpec((1, d, fc), lambda r, c, te, nu: (te[r], 0, chunk(r, c, nu))),
                  pl.BlockSpec((1, d, fc), lambda r, c, te, nu: (te[r], 0, n_fc + chunk(r, c, nu))),
                  pl.BlockSpec((1, fc, d), lambda r, c, te, nu: (te[r], chunk(r, c, nu), 0))],
        out_specs=pl.BlockSpec((tm, d), lambda r, c, te, nu: (r, 0)),
        scratch_shapes=[pltpu.VMEM((2, tm, d), F32), pltpu.VMEM((tm, d), BF16),
                        pltpu.SemaphoreType.DMA((2,))])
    return pl.pallas_call(
        functools.partial(_experts_kernel, tm=tm),
        grid_spec=grid_spec,
        out_shape=jax.ShapeDtypeStruct((n_tiles * tm, d), F32),
        compiler_params=_params("arbitrary", "arbitrary"),
        name="moe_experts",
    )(tile_expert, n_used, src_tiles, src_tiles, h, wgu_bf, wgu_bf, wd_bf)


def _combine_kernel(idx1_ref, idx2_ref, x3_ref, route_ref, ys_hbm, gfin_ref, out_ref, ybuf, sems, *, tm):
    _gather_rows_start(idx1_ref, ys_hbm, ybuf.at[0], sems.at[0], tm)
    _gather_rows_start(idx2_ref, ys_hbm, ybuf.at[1], sems.at[1], tm)
    route = route_ref[...]
    lane = lax.broadcasted_iota(jnp.int32, route.shape, 1)
    gate = lambda n: jnp.sum(jnp.where(lane == n, route, 0.0), axis=-1, keepdims=True)
    g1, g2 = gate(2), gate(3)
    _gather_rows_wait(ys_hbm, ybuf.at[0], sems.at[0], tm)
    _gather_rows_wait(ys_hbm, ybuf.at[1], sems.at[1], tm)
    out_ref[...] = _rms(x3_ref[...] + (g1 * ybuf[0] + g2 * ybuf[1]), gfin_ref[...])


def _moe_combine(x3, route, ys, slot1, slot2, g_final, tm):
    n, d = x3.shape
    idx = pl.BlockSpec((1, 1, tm), lambda t: (t, 0, 0), memory_space=pltpu.SMEM)
    row = lambda w: pl.BlockSpec((tm, w), lambda t: (t, 0))
    return pl.pallas_call(
        functools.partial(_combine_kernel, tm=tm),
        grid=(n // tm,),
        in_specs=[idx, idx, row(d), row(LANES), pl.BlockSpec(memory_space=pl.ANY), _const_spec((1, d))],
        out_specs=row(d),
        out_shape=jax.ShapeDtypeStruct((n, d), F32),
        scratch_shapes=[pltpu.VMEM((2, tm, d), F32), pltpu.SemaphoreType.DMA((2,))],
        compiler_params=_params("arbitrary"),
        name="moe_combine",
    )(slot1, slot2, x3, route, ys, g_final)


def _routing_tables(route, n_exp, tm):
    n = route.shape[0]
    experts = route[:, :2].astype(jnp.int32).reshape(-1)
    onehot = (experts[:, None] == jnp.arange(n_exp, dtype=jnp.int32)[None, :]).astype(jnp.int32)
    csum = jnp.cumsum(onehot, axis=0)
    rank = jnp.sum(csum * onehot, axis=1) - 1
    counts = csum[-1]
    padded = ((counts + tm - 1) // tm) * tm
    ends = jnp.cumsum(padded)
    starts = ends - padded
    slots = jnp.sum(starts[None, :] * onehot, axis=1) + rank
    n_tiles = (2 * n) // tm + n_exp
    src = jnp.zeros((n_tiles * tm,), jnp.int32).at[slots].set(jnp.arange(2 * n, dtype=jnp.int32) // 2)
    tile_start = jnp.arange(n_tiles, dtype=jnp.int32) * tm
    tile_expert = jnp.minimum(jnp.sum((tile_start[:, None] >= ends[None, :]).astype(jnp.int32), axis=1),
                              n_exp - 1)
    n_used = (ends[-1] // tm).reshape(1)
    slots = slots.reshape(n, 2)
    return slots[:, 0], slots[:, 1], src, tile_expert, n_used


def _out_moe(x, o, wo_bf, g, router_pad, n_exp, wgu_bf, wd_bf, g_final):
    n, d = x.shape
    tm = _row_tile(n, 512)
    x3, h, route = _moe_route(x, o, wo_bf, g, router_pad, n_exp)
    slot1, slot2, src, tile_expert, n_used = _routing_tables(route, n_exp, tm)
    ys = _moe_experts(h, src.reshape(-1, 1, tm), tile_expert, n_used, wgu_bf, wd_bf, tm)
    return _moe_combine(x3, route, ys, slot1.reshape(-1, 1, tm), slot2.reshape(-1, 1, tm), g_final, tm)


def _lane_block(x, start):
    return jnp.pad(x, ((0, 0), (start, LANES - start - x.shape[1])))


def _swap_halves(x):
    half = x.shape[-1] // 2
    return jnp.concatenate([x[..., half:], x[..., :half]], axis=-1)


def _prep_mla_weights(w_a, g_q, g_kv, w_uq, w_ukv, q_lora, kv_lora, n_heads, nope, rope):
    w_pe = w_a[:, q_lora + kv_lora:]
    w_pe_sw = _swap_halves(w_pe)
    wpe = jnp.concatenate([_lane_block(w_pe, 0), _lane_block(w_pe_sw, 0),
                           _lane_block(w_pe, nope), _lane_block(w_pe_sw, nope)], axis=1)
    uq = w_uq.reshape(q_lora, n_heads, nope + rope)
    uq_nope, uq_pe = uq[..., :nope], uq[..., nope:]
    tail = jnp.zeros((q_lora, n_heads, LANES - nope - rope), w_uq.dtype)
    wuq = jnp.concatenate([uq_nope, uq_pe, tail], axis=-1).reshape(q_lora, n_heads * LANES)
    wuqs = jnp.concatenate([jnp.zeros_like(uq_nope), _swap_halves(uq_pe), tail],
                           axis=-1).reshape(q_lora, n_heads * LANES)
    ukv = w_ukv.reshape(kv_lora, n_heads, -1)
    uk = ukv[..., :nope]
    wuk = jnp.concatenate([uk, jnp.zeros((kv_lora, n_heads, LANES - nope), uk.dtype)],
                          axis=-1).reshape(kv_lora, n_heads * LANES)
    wuv = ukv[..., nope:].reshape(kv_lora, -1)
    bf = lambda a: a.astype(BF16)
    return {"waq": bf(w_a[:, :q_lora]), "wakv": bf(w_a[:, q_lora:q_lora + kv_lora]), "wpe": bf(wpe),
            "gq": g_q[None, :], "gkv": g_kv[None, :], "wuq": bf(wuq), "wuqs": bf(wuqs),
            "wuk": bf(wuk), "wuv": bf(wuv)}


def _rope_tables(pos, nope, rope):
    half = rope // 2
    freqs = jnp.power(ROPE_THETA, -jnp.arange(half, dtype=F32) * 2.0 / rope)
    ang = pos.astype(F32)[:, None] * freqs[None, :]
    cos = jnp.concatenate([jnp.cos(ang)] * 2, axis=1)
    sin = jnp.concatenate([-jnp.sin(ang), jnp.sin(ang)], axis=1)
    c64 = _lane_block(cos, nope) + _lane_block(jnp.ones((pos.shape[0], nope), F32), 0)
    return _lane_block(cos, 0), _lane_block(sin, 0), c64, _lane_block(sin, nope)


def _bf16_split3(x):
    hi = x.astype(BF16).astype(F32)
    mid = (x - hi).astype(BF16).astype(F32)
    lo = (x - hi - mid).astype(BF16).astype(F32)
    return hi, mid, lo


def kernel(x_prompt, x_sample, cache_diff_k, cache_diff_v, cache_mla_ckv, cache_mla_kpe, norm_mix, norm_ffn, norm_final, diff_w_qkv, diff_lambda, diff_subln, diff_w_o, mla_w_a, mla_norm_q, mla_norm_kv, mla_w_uq, mla_w_ukv, mla_w_o, ffn_w_gu, ffn_w_down, moe_router, moe_w_gu, moe_w_down):
    b, t, d = x_prompt.shape
    db, nq, _ = x_sample.shape
    past = cache_diff_k.shape[2]
    n_da_heads, dv = cache_diff_k.shape[3], diff_subln.shape[-1]
    da_width = n_da_heads * dv
    q_lora, kv_lora = mla_norm_q.shape[-1], mla_norm_kv.shape[-1]
    rope = mla_w_a.shape[-1] - q_lora - kv_lora
    mla_width = mla_w_o.shape[1]
    n_exp = moe_router.shape[-1]
    assert norm_mix.shape[0] == 2, "one differential-attention layer followed by one MLA layer"

    uq_cols, ukv_cols = mla_w_uq.shape[-1], mla_w_ukv.shape[-1]
    n_mla_heads = (uq_cols - (ukv_cols - mla_width)) // rope
    nope = uq_cols // n_mla_heads - rope
    mla_v = mla_width // n_mla_heads
    assert n_mla_heads * (nope + mla_v) == ukv_cols and n_mla_heads % 2 == 0
    assert nope + rope <= LANES and 2 * mla_v == LANES

    bf = lambda a: a.astype(BF16)
    lam_init = 0.8 - 0.6 * math.exp(-0.3 * 0)
    da_scale = (dv // 2) ** -0.5 * LOG2E
    mla_scale = (nope + rope) ** -0.5 * LOG2E
    slopes2 = jnp.exp2(-8.0 * jnp.arange(1, n_da_heads + 1, dtype=F32) / n_da_heads) * LOG2E
    slope_parts = jnp.stack(_bf16_split3(slopes2) + (slopes2,), axis=1).reshape(-1)
    slope_rows = jnp.repeat(slopes2, 2 * nq)[:, None]

    w_qkv, w_o0 = bf(diff_w_qkv[0]), bf(diff_w_o[0])
    w_gu0, w_d0 = bf(ffn_w_gu[0]), bf(ffn_w_down[0])
    mla_w = _prep_mla_weights(mla_w_a[0], mla_norm_q[0], mla_norm_kv[0], mla_w_uq[0], mla_w_ukv[0],
                              q_lora, kv_lora, n_mla_heads, nope, rope)
    w_o1 = bf(mla_w_o[0])
    router_pad = jnp.pad(moe_router[0], ((0, 0), (0, LANES - n_exp)))
    moe_gu, moe_d = bf(moe_w_gu.reshape(moe_w_gu.shape[1:])), bf(moe_w_down.reshape(moe_w_down.shape[1:]))
    g_mix, g_ffn, g_fin = norm_mix[:, None, :], norm_ffn[:, None, :], norm_final[None, :]
    lam, subln = diff_lambda[0], diff_subln[0][None, :]

    def trunk(x, pos, attn0, attn1):
        bx, tx, _ = x.shape
        x2d = x.reshape(bx * tx, d)
        q, k, v, kx, vtx = _qkv_proj(x, g_mix[0], w_qkv, dv, da_scale)
        o = attn0(q, k, v, kx, vtx)
        x2 = _out_ffn(x2d, o.reshape(bx * tx, da_width), w_o0, g_ffn[0], w_gu0, w_d0)
        ckv, kpe, pe64, qcat, kcat, vtx1 = _mla_proj(x2.reshape(bx, tx, d), g_mix[1], mla_w,
                                                    _rope_tables(pos, nope, rope), rope, mla_v,
                                                    mla_scale)
        o = attn1(ckv, pe64, qcat, kcat, vtx1)
        y = _out_moe(x2, o.reshape(bx * tx, mla_width), w_o1, g_ffn[1], router_pad, n_exp,
                     moe_gu, moe_d, g_fin)
        return y, k, v, ckv, kpe

    y_p, k_p, v_p, ckv_p, kpe_p = trunk(
        x_prompt, jnp.arange(t, dtype=jnp.int32),
        lambda q, k, v, kx, vtx: _diff_attn_prompt(q, kx, vtx, lam, subln, slope_parts, lam_init),
        lambda ckv, pe64, qcat, kcat, vtx1: _mla_attn_prompt(qcat, kcat, vtx1, mla_v))

    n_s = db * nq
    cache_k = cache_diff_k.reshape(cache_diff_k.shape[1:])
    cache_v = cache_diff_v.reshape(cache_diff_v.shape[1:])
    cache_pe = jnp.pad(cache_mla_kpe[0], ((0, 0), (0, 0), (nope, LANES - nope - rope)))
    pos_s = past + jnp.tile(jnp.arange(nq, dtype=jnp.int32), db)

    def attn0_s(q, k, v, kx, vtx):
        r = lambda a: a.reshape(db, nq, da_width)
        return _diff_attn_decode(r(q), cache_k, cache_v, r(k), r(v), lam, subln, slope_rows, lam_init)

    def attn1_s(ckv, pe64, qcat, kcat, vtx1):
        return _mla_attn_decode(qcat.reshape(db, nq, -1), cache_mla_ckv[0], cache_pe,
                                ckv.reshape(db, nq, kv_lora), pe64.reshape(db, nq, LANES),
                                mla_w["wuk"], mla_w["wuv"])

    y_s, k_s, v_s, ckv_s, kpe_s = trunk(x_sample.reshape(1, n_s, d), pos_s, attn0_s, attn1_s)

    return (y_p.reshape(b, t, d), y_s.reshape(db, nq, d),
            k_p.reshape(1, b, t, n_da_heads, dv), v_p.reshape(1, b, t, n_da_heads, dv),
            ckv_p.reshape(1, b, t, kv_lora), kpe_p.reshape(1, b, t, rope),
            k_s.reshape(1, db, nq, n_da_heads, dv), v_s.reshape(1, db, nq, n_da_heads, dv),
            ckv_s.reshape(1, db, nq, kv_lora), kpe_s.reshape(1, db, nq, rope))
```

```python
import functools
import math

import jax
import jax.numpy as jnp
from jax import lax
from jax.experimental import pallas as pl
from jax.experimental.pallas import tpu as pltpu

NORM_EPS = 1e-6
NEG_INF = -1e30
CHUNK = 64
ROPE_THETA = 10000.0
LOG2E = math.log2(math.e)
LANES = 128
MXU_DIM = 256
BF16_ROWS = 16
EXPERT_CHUNK_CAP = 2048
FLASH_Q_BLOCK = 1024
VMEM_LIMIT_BYTES = 56 * 2**20
BF16 = jnp.bfloat16
F32 = jnp.float32


def _params(*sem):
    return pltpu.CompilerParams(dimension_semantics=sem, vmem_limit_bytes=VMEM_LIMIT_BYTES)


def _const_spec(shape):
    nd = len(shape)
    return pl.BlockSpec(shape, lambda *_: (0,) * nd, pipeline_mode=pl.Buffered(1))


def _rms(x, g):
    ms = jnp.mean(x * x, axis=-1, keepdims=True)
    return x * lax.rsqrt(ms + NORM_EPS) * g


def _dot(a, b):
    return jnp.dot(a, b, preferred_element_type=F32)


def _dot_nt(a, b):
    return lax.dot_general(a, b, (((1,), (1,)), ((), ())), preferred_element_type=F32)


def _row_tile(t, cap):
    tm = min(t, cap)
    assert t % tm == 0
    return tm


def _ones_row_block(n_cols):
    sub = lax.broadcasted_iota(jnp.int32, (BF16_ROWS, n_cols), 0)
    return jnp.where(sub == 0, 1.0, 0.0).astype(BF16)


def _qkv_kernel(x_ref, g_ref, w_ref, q_ref, k_ref, v_ref, kx_ref, vtx_ref, *, width, dv, scale):
    tm = x_ref.shape[1]
    h = _rms(x_ref[0], g_ref[...]).astype(BF16)
    qkv = _dot(h, w_ref[...])
    q_ref[0] = (qkv[:, :width] * scale).astype(BF16)
    k = qkv[:, width:2 * width]
    v = qkv[:, 2 * width:]
    k_ref[0] = k
    v_ref[0] = v
    k_bf = k.astype(BF16)
    vt = v.T.astype(BF16)
    pos = lax.broadcasted_iota(jnp.int32, (tm, LANES), 0) + pl.program_id(1) * tm
    lane = lax.broadcasted_iota(jnp.int32, (tm, LANES), 1)
    pos_feat = jnp.where(lane < 3, pos // LANES, jnp.where(lane < 6, pos % LANES, 0))
    pos_feat = pos_feat.astype(F32).astype(BF16)
    ones = _ones_row_block(tm)
    for hh in range(width // dv):
        kx_ref[0, :, 2 * dv * hh:2 * dv * hh + dv] = k_bf[:, hh * dv:(hh + 1) * dv]
        kx_ref[0, :, 2 * dv * hh + dv:2 * dv * (hh + 1)] = pos_feat
        r0 = hh * (dv + BF16_ROWS)
        vtx_ref[0, r0:r0 + dv, :] = vt[hh * dv:(hh + 1) * dv, :]
        vtx_ref[0, r0 + dv:r0 + dv + BF16_ROWS, :] = ones


def _qkv_proj(x, g, w_bf, dv, scale):
    bx, tx, d = x.shape
    width = w_bf.shape[1] // 3
    n_heads = width // dv
    assert dv == LANES and tx <= LANES * MXU_DIM
    tm = _row_tile(tx, 512)
    row = lambda n: pl.BlockSpec((1, tm, n), lambda b, t: (b, t, 0))
    vrows = n_heads * (dv + BF16_ROWS)
    return pl.pallas_call(
        functools.partial(_qkv_kernel, width=width, dv=dv, scale=scale),
        grid=(bx, tx // tm),
        in_specs=[row(d), _const_spec((1, d)), _const_spec(w_bf.shape)],
        out_specs=[row(width), row(width), row(width), row(2 * width),
                   pl.BlockSpec((1, vrows, tm), lambda b, t: (b, 0, t))],
        out_shape=[jax.ShapeDtypeStruct((bx, tx, width), BF16),
                   jax.ShapeDtypeStruct((bx, tx, width), F32),
                   jax.ShapeDtypeStruct((bx, tx, width), F32),
                   jax.ShapeDtypeStruct((bx, tx, 2 * width), BF16),
                   jax.ShapeDtypeStruct((bx, vrows, tx), BF16)],
        compiler_params=_params("parallel", "parallel"),
        name="qkv_proj",
    )(x, g, w_bf)


def _flash_streams(i, tq, key_op, query_op, value_op, diag_tile, z_a, z_b, cm_a, cm_b, m, acc,
                   consume_first):
    n_streams = 2
    tk = tq // 2

    units = [(s, slice(n * MXU_DIM, (n + 1) * MXU_DIM))
             for s in range(n_streams) for n in range(tq // MXU_DIM)]

    def produce(blk, z, cm, extra, s, cols):
        off = pl.multiple_of(blk * tk, tk)
        scores = _dot_nt(key_op(s, off), query_op(s, cols))
        if extra is not None:
            scores = scores + extra[:, cols]
        z[s, :, cols] = scores
        cm[s, :, cols] = jnp.max(scores, axis=0, keepdims=True)

    def consume(blk, z, cm, s, cols):
        off = pl.multiple_of(blk * tk, tk)
        m_old = m[s, :, cols]
        m_new = jnp.maximum(m_old, cm[s, :, cols])
        p = jnp.exp2(z[s, :, cols] - m_new).astype(BF16)
        acc[s, :, cols] = jnp.exp2(m_old - m_new) * acc[s, :, cols] + _dot(value_op(s, off), p)
        m[s, :, cols] = m_new

    def step(p_blk, p_z, p_cm, extra, c_blk, c_z, c_cm):
        for s, cols in units:
            if consume_first:
                consume(c_blk, c_z, c_cm, s, cols)
            produce(p_blk, p_z, p_cm, extra, s, cols)
            if not consume_first:
                consume(c_blk, c_z, c_cm, s, cols)

    m[...] = jnp.full(m.shape, NEG_INF, F32)
    acc[...] = jnp.zeros(acc.shape, F32)
    first_tile = diag_tile(0)
    for s, cols in units:
        produce(2 * i, z_a, cm_a, first_tile, s, cols)

    def pair(u):
        step(2 * u, z_b, cm_b, None, jnp.where(u == 0, 2 * i, 2 * u - 1), z_a, cm_a)
        step(2 * u + 1, z_a, cm_a, None, 2 * u, z_b, cm_b)

    odd = i % 2

    @pl.when(odd == 1)
    def _():
        pair(0)

    def body(v, carry):
        pair(odd + 2 * v)
        pair(odd + 2 * v + 1)
        return carry

    lax.fori_loop(0, i // 2, body, 0)

    last_tile = diag_tile(1)
    last_a = jnp.where(i == 0, 0, 2 * i - 1)
    late_units = [(s, cols) for s, cols in units if cols.stop > tk]
    for s, cols in units:
        if cols.stop > tk:
            produce(2 * i + 1, z_b, cm_b, last_tile, s, cols)
        consume(last_a, z_a, cm_a, s, cols)
    for s, cols in late_units:
        consume(2 * i + 1, z_b, cm_b, s, cols)


def _local_positions(tq, n):
    tk = tq // 2
    kl = lax.broadcasted_iota(jnp.int32, (tk, tq), 0) + n * tk
    ql = lax.broadcasted_iota(jnp.int32, (tk, tq), 1)
    return kl, ql, (kl // CHUNK) <= (ql // CHUNK)


def _lambda_full(lam_ref, lam_init):
    lf = lam_ref[...]
    s01 = jnp.sum(lf[0:1, :] * lf[1:2, :], axis=-1, keepdims=True)
    s23 = jnp.sum(lf[2:3, :] * lf[3:4, :], axis=-1, keepdims=True)
    return jnp.exp(s01) - jnp.exp(s23) + lam_init


def _diff_flash_kernel(slope_ref, q_ref, kx_ref, vtx_ref, lam_ref, subln_ref, o_ref,
                       qx, z_a, z_b, cm_a, cm_b, m, acc, *, tq, dv, lam_init):
    h = pl.program_id(1)
    i = pl.program_id(2)
    s_hi, s_mid, s_lo, slope2 = (slope_ref[4 * h + n] for n in range(4))
    q = q_ref[0]
    lane = lax.broadcasted_iota(jnp.int32, q.shape, 1)
    zero = jnp.zeros_like(q)
    slope_feat = jnp.zeros(q.shape, F32)
    for n, val in enumerate((LANES * s_hi, LANES * s_mid, LANES * s_lo, s_hi, s_mid, s_lo)):
        slope_feat = jnp.where(lane == n, val, slope_feat)
    slope_feat = slope_feat.astype(BF16)
    qx[0, :, 0:dv] = jnp.where(lane < dv // 2, q, zero)
    qx[1, :, 0:dv] = jnp.where(lane >= dv // 2, q, zero)
    qx[0, :, dv:2 * dv] = slope_feat
    qx[1, :, dv:2 * dv] = slope_feat

    def diag_tile(n):
        kl, ql, visible = _local_positions(tq, n)
        return jnp.where(visible, (-2.0 * slope2) * jnp.maximum(kl - ql, 0).astype(F32), NEG_INF)

    tk = tq // 2
    _flash_streams(i, tq,
                   key_op=lambda s, off: kx_ref[0, pl.ds(off, tk), :],
                   query_op=lambda s, cols: qx[s, cols, :],
                   value_op=lambda s, off: vtx_ref[0, :, pl.ds(off, tk)],
                   diag_tile=diag_tile, z_a=z_a, z_b=z_b, cm_a=cm_a, cm_b=cm_b, m=m, acc=acc,
                   consume_first=True)

    lam = _lambda_full(lam_ref, lam_init)
    o_t = (acc[0, 0:dv, :] / acc[0, dv:dv + 1, :]
           - lam * (acc[1, 0:dv, :] / acc[1, dv:dv + 1, :]))
    o = _rms(o_t.T, subln_ref[...]) * (1.0 - lam_init)
    o_ref[0] = o.astype(BF16)


def _flash_scratch(tq, q_feat, acc_rows):
    return ([pltpu.VMEM((2, tq, q_feat), BF16)] if q_feat else []) + [
        pltpu.VMEM((2, tq // 2, tq), F32), pltpu.VMEM((2, tq // 2, tq), F32),
        pltpu.VMEM((2, 1, tq), F32), pltpu.VMEM((2, 1, tq), F32),
        pltpu.VMEM((2, 1, tq), F32), pltpu.VMEM((2, acc_rows, tq), F32)]


def _diff_attn_prompt(q, kx, vtx, lam, subln, slope_parts, lam_init):
    b, t, width = q.shape
    dv = subln.shape[-1]
    n_heads = width // dv
    tq = _row_tile(t, FLASH_Q_BLOCK)
    assert tq % (2 * MXU_DIM) == 0
    vrows = dv + BF16_ROWS
    return pl.pallas_call(
        functools.partial(_diff_flash_kernel, tq=tq, dv=dv, lam_init=lam_init),
        grid=(b, n_heads, t // tq),
        in_specs=[pl.BlockSpec(memory_space=pltpu.SMEM),
                  pl.BlockSpec((1, tq, dv), lambda b_, h, i: (b_, i, h)),
                  pl.BlockSpec((1, t, 2 * dv), lambda b_, h, i: (b_, 0, h)),
                  pl.BlockSpec((1, vrows, t), lambda b_, h, i: (b_, h, 0)),
                  pl.BlockSpec(lam.shape, lambda b_, h, i: (0, 0)),
                  pl.BlockSpec(subln.shape, lambda b_, h, i: (0, 0))],
        out_specs=pl.BlockSpec((1, tq, dv), lambda b_, h, i: (b_, i, h)),
        out_shape=jax.ShapeDtypeStruct((b, t, width), BF16),
        scratch_shapes=_flash_scratch(tq, 2 * dv, vrows),
        compiler_params=_params("parallel", "parallel", "arbitrary"),
        name="diff_attn_prompt",
    )(slope_parts, q, kx, vtx, lam, subln)


def _mla_flash_kernel(q_ref, k_ref, vtx_ref, o_ref, z_a, z_b, cm_a, cm_b, m, acc, *, tq, dv):
    i = pl.program_id(2)
    vrows = dv + BF16_ROWS
    tk = tq // 2
    _flash_streams(i, tq,
                   key_op=lambda s, off: k_ref[0, pl.ds(off, tk), s * LANES:(s + 1) * LANES],
                   query_op=lambda s, cols: q_ref[0, cols, s * LANES:(s + 1) * LANES],
                   value_op=lambda s, off: vtx_ref[0, s * vrows:(s + 1) * vrows, pl.ds(off, tk)],
                   diag_tile=lambda n: jnp.where(_local_positions(tq, n)[2], 0.0, NEG_INF),
                   z_a=z_a, z_b=z_b, cm_a=cm_a, cm_b=cm_b, m=m, acc=acc, consume_first=False)
    o_t = jnp.concatenate([acc[s, 0:dv, :] / acc[s, dv:dv + 1, :] for s in range(2)], axis=0)
    o_ref[0] = o_t.T.astype(BF16)


def _mla_attn_prompt(qcat, kcat, vtx, dv):
    b, t, wq = qcat.shape
    n_pairs = wq // (2 * LANES)
    tq = _row_tile(t, FLASH_Q_BLOCK)
    assert tq % (2 * MXU_DIM) == 0
    vrows = dv + BF16_ROWS
    return pl.pallas_call(
        functools.partial(_mla_flash_kernel, tq=tq, dv=dv),
        grid=(b, n_pairs, t // tq),
        in_specs=[pl.BlockSpec((1, tq, 2 * LANES), lambda b_, h, i: (b_, i, h)),
                  pl.BlockSpec((1, t, 2 * LANES), lambda b_, h, i: (b_, 0, h)),
                  pl.BlockSpec((1, 2 * vrows, t), lambda b_, h, i: (b_, h, 0))],
        out_specs=pl.BlockSpec((1, tq, 2 * dv), lambda b_, h, i: (b_, i, h)),
        out_shape=jax.ShapeDtypeStruct((b, t, n_pairs * 2 * dv), BF16),
        scratch_shapes=_flash_scratch(tq, 0, vrows),
        compiler_params=_params("parallel", "parallel", "arbitrary"),
        name="mla_attn_prompt",
    )(qcat, kcat, vtx)


def _block_diag_queries(q, n_groups, group_width):
    qt = jnp.concatenate([q] * n_groups, axis=0)
    r = lax.broadcasted_iota(jnp.int32, qt.shape, 0)
    c = lax.broadcasted_iota(jnp.int32, qt.shape, 1)
    return jnp.where((c // group_width) == (r // q.shape[0]), qt, jnp.zeros_like(qt))


def _decode_update(z, v_bf, m_ref, l_ref, acc_ref):
    m_old = m_ref[...]
    m_new = jnp.maximum(m_old, jnp.max(z, axis=-1, keepdims=True))
    alpha = jnp.exp2(m_old - m_new)
    p = jnp.exp2(z - m_new)
    l_ref[...] = alpha * l_ref[...] + jnp.sum(p, axis=-1, keepdims=True)
    acc_ref[...] = alpha * acc_ref[...] + _dot(p.astype(BF16), v_bf)
    m_ref[...] = m_new


def _decode_positions(n_rows, n_keys, nq, past, key_start, n_valid):
    r = lax.broadcasted_iota(jnp.int32, (n_rows, n_keys), 0)
    c = lax.broadcasted_iota(jnp.int32, (n_rows, n_keys), 1)
    q_pos = past + r % nq
    k_pos = key_start + c
    visible = ((k_pos // CHUNK) <= (q_pos // CHUNK)) & (c < n_valid)
    return q_pos, k_pos, visible


def _pad_rows(x, n):
    return jnp.concatenate([x, jnp.zeros((n - x.shape[0], x.shape[1]), x.dtype)], axis=0)


def _diff_decode_kernel(slope_ref, q_ref, ck_ref, cv_ref, kn_ref, vn_ref, lam_ref, subln_ref,
                        o_ref, qrows, qbd, bias, m, l, acc, *, past, tk, nq, dv, lam_init):
    j = pl.program_id(1)
    n_rows = qrows.shape[0]
    n_heads = n_rows // (2 * nq)
    n_cols = tk * n_heads

    @pl.when(j == 0)
    def _():
        q = q_ref[0]
        qbd[...] = _block_diag_queries(q, 2 * n_heads, dv // 2)
        lane = lax.broadcasted_iota(jnp.int32, (nq, dv), 1)
        for hh in range(n_heads):
            q_h = q[:, hh * dv:(hh + 1) * dv]
            r0 = hh * 2 * nq
            qrows[r0:r0 + nq, :] = jnp.where(lane < dv // 2, q_h, jnp.zeros_like(q_h))
            qrows[r0 + nq:r0 + 2 * nq, :] = jnp.where(lane >= dv // 2, q_h, jnp.zeros_like(q_h))
        r = lax.broadcasted_iota(jnp.int32, (n_rows, n_cols), 0)
        c = lax.broadcasted_iota(jnp.int32, (n_rows, n_cols), 1)
        own_head = (c % n_heads) == (r // (2 * nq))
        rel = (c // n_heads - r % nq).astype(F32)
        bias[...] = jnp.where(own_head, slope_ref[...] * rel, NEG_INF)
        m[...] = jnp.full(m.shape, NEG_INF, F32)
        l[...] = jnp.zeros(l.shape, F32)
        acc[...] = jnp.zeros(acc.shape, F32)

    def update(z, pv):
        m_old = m[...]
        m_new = jnp.maximum(m_old, jnp.max(z, axis=-1, keepdims=True))
        alpha = jnp.exp2(m_old - m_new)
        p = jnp.exp2(z - m_new)
        l[...] = alpha * l[...] + jnp.sum(p, axis=-1, keepdims=True)
        acc[...] = alpha * acc[...] + pv(p.astype(BF16))
        m[...] = m_new

    k2 = ck_ref[0].reshape(n_cols, dv).astype(BF16)
    v2 = cv_ref[0].reshape(n_cols, dv).astype(BF16)
    shift = slope_ref[...] * (past - j * tk).astype(F32)
    update(_dot_nt(qrows[...], k2) + bias[...] - shift, lambda p: _dot(p, v2))

    @pl.when(j == pl.num_programs(1) - 1)
    def _():
        k_new = _pad_rows(kn_ref[0], LANES).astype(BF16)
        v_new = _pad_rows(vn_ref[0], LANES).astype(BF16)
        q_pos, k_pos, visible = _decode_positions(n_rows, LANES, nq, past, past, nq)
        dist = jnp.abs(q_pos - k_pos).astype(F32)
        z = jnp.where(visible, _dot_nt(qbd[...], k_new) - slope_ref[...] * dist, NEG_INF)

        def pv_new(p):
            full = _dot(p, v_new)
            return jnp.concatenate([full[hh * 2 * nq:(hh + 1) * 2 * nq, hh * dv:(hh + 1) * dv]
                                    for hh in range(n_heads)], axis=0)

        update(z, pv_new)
        lam = _lambda_full(lam_ref, lam_init)
        for hh in range(n_heads):
            r0 = hh * 2 * nq
            o1 = acc[r0:r0 + nq, :] / l[r0:r0 + nq, :]
            o2 = acc[r0 + nq:r0 + 2 * nq, :] / l[r0 + nq:r0 + 2 * nq, :]
            o = _rms(o1 - lam * o2, subln_ref[...]) * (1.0 - lam_init)
            o_ref[0, :, hh * dv:(hh + 1) * dv] = o.astype(BF16)


def _diff_attn_decode(q, cache_k, cache_v, k_new, v_new, lam, subln, slope_rows, lam_init):
    db, nq, width = q.shape
    past, n_heads, dv = cache_k.shape[1:]
    n_rows = n_heads * 2 * nq
    tk = _row_tile(past, 512)
    stream = lambda n: pl.BlockSpec((1, nq, n), lambda b, j: (b, 0, 0))
    cache = pl.BlockSpec((1, tk, n_heads, dv), lambda b, j: (b, j, 0, 0))
    return pl.pallas_call(
        functools.partial(_diff_decode_kernel, past=past, tk=tk, nq=nq, dv=dv, lam_init=lam_init),
        grid=(db, past // tk),
        in_specs=[_const_spec(slope_rows.shape), stream(width), cache, cache, stream(width),
                  stream(width), _const_spec(lam.shape), _const_spec(subln.shape)],
        out_specs=stream(width),
        out_shape=jax.ShapeDtypeStruct((db, nq, width), BF16),
        scratch_shapes=[pltpu.VMEM((n_rows, dv), BF16), pltpu.VMEM((n_rows, width), BF16),
                        pltpu.VMEM((n_rows, tk * n_heads), F32), pltpu.VMEM((n_rows, 1), F32),
                        pltpu.VMEM((n_rows, 1), F32), pltpu.VMEM((n_rows, dv), F32)],
        compiler_params=_params("parallel", "arbitrary"),
        name="diff_attn_decode",
    )(slope_rows, q, cache_k, cache_v, k_new, v_new, lam, subln)


def _mla_decode_kernel(q_ref, cc_ref, cpe_ref, cn_ref, pen_ref, wuk_ref, wuv_ref, o_ref,
                       qbd, m, l, acc, *, past, tk, nq, dv):
    j = pl.program_id(1)
    n_rows = qbd.shape[0]
    n_heads = n_rows // nq

    @pl.when(j == 0)
    def _():
        qbd[...] = _block_diag_queries(q_ref[0], n_heads, LANES)
        m[...] = jnp.full(m.shape, NEG_INF, F32)
        l[...] = jnp.zeros(l.shape, F32)
        acc[...] = jnp.zeros(acc.shape, F32)

    def step(c_f32, pe_f32, key_start, n_valid):
        n_keys = c_f32.shape[0]
        c_bf = c_f32.astype(BF16)
        kcat = (_dot(c_bf, wuk_ref[...]) + jnp.concatenate([pe_f32] * n_heads, axis=1)).astype(BF16)
        v_bf = _dot(c_bf, wuv_ref[...]).astype(BF16)
        s = _dot_nt(qbd[...], kcat)
        _, _, visible = _decode_positions(n_rows, n_keys, nq, past, key_start, n_valid)
        _decode_update(jnp.where(visible, s, NEG_INF), v_bf, m, l, acc)

    step(cc_ref[0], cpe_ref[0], j * tk, tk)

    @pl.when(j == pl.num_programs(1) - 1)
    def _():
        step(_pad_rows(cn_ref[0], LANES), _pad_rows(pen_ref[0], LANES), past, nq)
        lane = lax.broadcasted_iota(jnp.int32, (nq, 2 * dv), 1)
        for pair in range(n_heads // 2):
            r0 = pair * 2 * nq
            cols = slice(pair * 2 * dv, (pair + 1) * 2 * dv)
            even = acc[r0:r0 + nq, cols] / l[r0:r0 + nq, :]
            odd = acc[r0 + nq:r0 + 2 * nq, cols] / l[r0 + nq:r0 + 2 * nq, :]
            o_ref[0, :, cols] = jnp.where(lane < dv, even, odd).astype(BF16)


def _mla_attn_decode(qcat, cache_c, cache_pe, c_new, pe_new, wuk_bf, wuv_bf):
    db, nq, wq = qcat.shape
    past, lora = cache_c.shape[1], cache_c.shape[2]
    n_heads = wq // LANES
    wv = wuv_bf.shape[1]
    dv = wv // n_heads
    n_rows = n_heads * nq
    tk = _row_tile(past, 512)
    stream = lambda n: pl.BlockSpec((1, nq, n), lambda b, j: (b, 0, 0))
    return pl.pallas_call(
        functools.partial(_mla_decode_kernel, past=past, tk=tk, nq=nq, dv=dv),
        grid=(db, past // tk),
        in_specs=[stream(wq),
                  pl.BlockSpec((1, tk, lora), lambda b, j: (b, j, 0)),
                  pl.BlockSpec((1, tk, LANES), lambda b, j: (b, j, 0)),
                  stream(lora), stream(LANES),
                  _const_spec(wuk_bf.shape), _const_spec(wuv_bf.shape)],
        out_specs=stream(wv),
        out_shape=jax.ShapeDtypeStruct((db, nq, wv), BF16),
        scratch_shapes=[pltpu.VMEM((n_rows, wq), BF16), pltpu.VMEM((n_rows, 1), F32),
                        pltpu.VMEM((n_rows, 1), F32), pltpu.VMEM((n_rows, wv), F32)],
        compiler_params=_params("parallel", "arbitrary"),
        name="mla_attn_decode",
    )(qcat, cache_c, cache_pe, c_new, pe_new, wuk_bf, wuv_bf)


def _silu(x):
    return x * jax.nn.sigmoid(x)


def _out_ffn_kernel(x_ref, o_ref, wo_ref, g_ref, wgu_ref, wd_ref, out_ref, *, f, n_chunks):
    x1 = x_ref[...] + _dot(o_ref[...], wo_ref[...])
    h = _rms(x1, g_ref[...]).astype(BF16)
    fc = f // n_chunks
    acc = x1
    for c in range(n_chunks):
        gate = _dot(h, wgu_ref[:, c * fc:(c + 1) * fc])
        up = _dot(h, wgu_ref[:, f + c * fc:f + (c + 1) * fc])
        acc = acc + _dot((_silu(gate) * up).astype(BF16), wd_ref[c * fc:(c + 1) * fc, :])
    out_ref[...] = acc


def _out_ffn(x, o, wo_bf, g, wgu_bf, wd_bf):
    n, d = x.shape
    f = wd_bf.shape[0]
    tm = _row_tile(n, 512)
    n_chunks = 2 if f % (2 * LANES) == 0 else 1
    row = lambda w: pl.BlockSpec((tm, w), lambda t: (t, 0))
    return pl.pallas_call(
        functools.partial(_out_ffn_kernel, f=f, n_chunks=n_chunks),
        grid=(n // tm,),
        in_specs=[row(d), row(o.shape[1]), _const_spec(wo_bf.shape), _const_spec((1, d)),
                  _const_spec(wgu_bf.shape), _const_spec(wd_bf.shape)],
        out_specs=row(d),
        out_shape=jax.ShapeDtypeStruct((n, d), F32),
        compiler_params=_params("parallel"),
        name="attn_out_ffn",
    )(x, o, wo_bf, g, wgu_bf, wd_bf)


def _mla_proj_kernel(x_ref, g_ref, waq_ref, wakv_ref, wpe_ref, gq_ref, gkv_ref, wuq_ref, wuqs_ref,
                     wuk_ref, wuv_ref, c0_ref, s0_ref, c64_ref, s64_ref,
                     ckv_ref, kpe_ref, pe64_ref, q_ref, k_ref, vtx_ref, *, rope, dv, scale):
    tm = x_ref.shape[1]
    h = _rms(x_ref[0], g_ref[...]).astype(BF16)
    c_q = _rms(_dot(h, waq_ref[...]), gq_ref[...]).astype(BF16)
    c_kv = _rms(_dot(h, wakv_ref[...]), gkv_ref[...])
    ckv_ref[0] = c_kv
    c_kv_bf = c_kv.astype(BF16)
    pe = _dot(h, wpe_ref[...])
    kpe0 = pe[:, 0:LANES] * c0_ref[...] + pe[:, LANES:2 * LANES] * s0_ref[...]
    kpe_ref[0] = kpe0[:, :rope]
    c64 = c64_ref[...]
    s64 = s64_ref[...]
    pe64 = pe[:, 2 * LANES:3 * LANES] * c64 + pe[:, 3 * LANES:] * s64
    pe64_ref[0] = pe64
    q = _dot(c_q, wuq_ref[...])
    q_sw = _dot(c_q, wuqs_ref[...])
    k = _dot(c_kv_bf, wuk_ref[...])
    for hh in range(q.shape[1] // LANES):
        cols = slice(hh * LANES, (hh + 1) * LANES)
        q_ref[0, :, cols] = ((q[:, cols] * c64 + q_sw[:, cols] * s64) * scale).astype(BF16)
        k_ref[0, :, cols] = (k[:, cols] + pe64).astype(BF16)
    vt = _dot(c_kv_bf, wuv_ref[...]).T.astype(BF16)
    ones = _ones_row_block(tm)
    for hh in range(vt.shape[0] // dv):
        r0 = hh * (dv + BF16_ROWS)
        vtx_ref[0, r0:r0 + dv, :] = vt[hh * dv:(hh + 1) * dv, :]
        vtx_ref[0, r0 + dv:r0 + dv + BF16_ROWS, :] = ones


def _mla_proj(x, g, w, tabs, rope, dv, scale):
    bx, tx, d = x.shape
    tm = _row_tile(tx, 256)
    lora = w["wakv"].shape[1]
    wq, wv = w["wuq"].shape[1], w["wuv"].shape[1]
    vrows = (wv // dv) * (dv + BF16_ROWS)
    row = lambda n: pl.BlockSpec((1, tm, n), lambda b, t: (b, t, 0))
    tab = pl.BlockSpec((tm, LANES), lambda b, t: (t, 0))
    weights = [w[k] for k in ("waq", "wakv", "wpe", "gq", "gkv", "wuq", "wuqs", "wuk", "wuv")]
    return pl.pallas_call(
        functools.partial(_mla_proj_kernel, rope=rope, dv=dv, scale=scale),
        grid=(bx, tx // tm),
        in_specs=[row(d), _const_spec((1, d))] + [_const_spec(a.shape) for a in weights] + [tab] * 4,
        out_specs=[row(lora), row(rope), row(LANES), row(wq), row(wq),
                   pl.BlockSpec((1, vrows, tm), lambda b, t: (b, 0, t))],
        out_shape=[jax.ShapeDtypeStruct((bx, tx, lora), F32),
                   jax.ShapeDtypeStruct((bx, tx, rope), F32),
                   jax.ShapeDtypeStruct((bx, tx, LANES), F32),
                   jax.ShapeDtypeStruct((bx, tx, wq), BF16),
                   jax.ShapeDtypeStruct((bx, tx, wq), BF16),
                   jax.ShapeDtypeStruct((bx, vrows, tx), BF16)],
        compiler_params=_params("parallel", "parallel"),
        name="mla_proj",
    )(x, g, *weights, *tabs)


def _route_kernel(x_ref, o_ref, wo_ref, g_ref, router_ref, x3_ref, h_ref, route_ref, *, n_exp):
    x3 = x_ref[...] + _dot(o_ref[...], wo_ref[...])
    h = _rms(x3, g_ref[...])
    lane = lax.broadcasted_iota(jnp.int32, route_ref.shape, 1).astype(F32)
    logits = jnp.dot(h, router_ref[...], preferred_element_type=F32,
                     precision=lax.Precision.HIGHEST)
    logits = jnp.where(lane < n_exp, logits, -jnp.inf)
    v1 = jnp.max(logits, axis=-1, keepdims=True)
    i1 = jnp.min(jnp.where(logits == v1, lane, float(LANES)), axis=-1, keepdims=True)
    rest = jnp.where(lane == i1, -jnp.inf, logits)
    v2 = jnp.max(rest, axis=-1, keepdims=True)
    i2 = jnp.min(jnp.where(rest == v2, lane, float(LANES)), axis=-1, keepdims=True)
    e2 = jnp.exp(v2 - v1)
    denom = 1.0 + e2
    route = jnp.zeros(route_ref.shape, F32)
    for n, val in enumerate((i1, i2, 1.0 / denom, e2 / denom)):
        route = jnp.where(lane == n, val, route)
    x3_ref[...] = x3
    h_ref[...] = h
    route_ref[...] = route


def _moe_route(x, o, wo_bf, g, router_pad, n_exp):
    n, d = x.shape
    tm = _row_tile(n, 512)
    row = lambda w: pl.BlockSpec((tm, w), lambda t: (t, 0))
    return pl.pallas_call(
        functools.partial(_route_kernel, n_exp=n_exp),
        grid=(n // tm,),
        in_specs=[row(d), row(o.shape[1]), _const_spec(wo_bf.shape), _const_spec((1, d)),
                  _const_spec(router_pad.shape)],
        out_specs=[row(d), row(d), row(LANES)],
        out_shape=[jax.ShapeDtypeStruct((n, d), F32), jax.ShapeDtypeStruct((n, d), F32),
                   jax.ShapeDtypeStruct((n, LANES), F32)],
        compiler_params=_params("parallel"),
        name="moe_route",
    )(x, o, wo_bf, g, router_pad)


def _gather_rows_start(idx_ref, src_hbm, dst, sem, n_rows):
    def body(t, carry):
        pltpu.make_async_copy(src_hbm.at[pl.ds(idx_ref[0, 0, t], 1)], dst.at[pl.ds(t, 1)], sem).start()
        return carry
    lax.fori_loop(0, n_rows, body, 0, unroll=8)


def _gather_rows_wait(src_hbm, dst, sem, n_rows):
    pltpu.make_async_copy(src_hbm.at[pl.ds(0, n_rows)], dst, sem).wait()


def _experts_kernel(te_ref, nu_ref, idx_ref, idx_next_ref, h_hbm, wg_ref, wu_ref, wd_ref, ys_ref,
                    hbuf, hb, sems, *, tm):
    r = pl.program_id(0)
    c = pl.program_id(1)
    n_used = nu_ref[0]
    slot = r % 2

    @pl.when((c == 0) & (r < n_used))
    def _():
        @pl.when(r == 0)
        def _():
            _gather_rows_start(idx_ref, h_hbm, hbuf.at[0], sems.at[0], tm)

        @pl.when(r + 1 < n_used)
        def _():
            _gather_rows_start(idx_next_ref, h_hbm, hbuf.at[1 - slot], sems.at[1 - slot], tm)

        _gather_rows_wait(h_hbm, hbuf.at[slot], sems.at[slot], tm)
        hb[...] = hbuf[slot].astype(BF16)

    @pl.when(r < n_used)
    def _():
        h = hb[...]
        act = (_silu(_dot(h, wg_ref[0])) * _dot(h, wu_ref[0])).astype(BF16)
        y = _dot(act, wd_ref[0])

        @pl.when(c == 0)
        def _():
            ys_ref[...] = y

        @pl.when(c > 0)
        def _():
            ys_ref[...] += y

    @pl.when((c == 0) & (r >= n_used))
    def _():
        ys_ref[...] = jnp.zeros(ys_ref.shape, F32)


def _moe_experts(h, src_tiles, tile_expert, n_used, wgu_bf, wd_bf, tm):
    n_tiles = src_tiles.shape[0]
    d = h.shape[1]
    f = wd_bf.shape[1]
    fc = max(c for c in range(LANES, EXPERT_CHUNK_CAP + 1, LANES) if f % c == 0)
    n_fc = f // fc
    chunk = lambda r, c, nu: jnp.where(r < nu[0], c, 0)
    grid_spec = pltpu.PrefetchScalarGridSpec(
        num_scalar_prefetch=2,
        grid=(n_tiles, n_fc),
        in_specs=[pl.BlockSpec((1, 1, tm), lambda r, c, te, nu: (r, 0, 0), memory_space=pltpu.SMEM),
                  pl.BlockSpec((1, 1, tm), lambda r, c, te, nu: (jnp.minimum(r + 1, n_tiles - 1), 0, 0),
                               memory_space=pltpu.SMEM),
                  pl.BlockSpec(memory_space=pl.ANY),
                  pl.BlockSpec((1, d, fc), lambda r, c, te, nu: (te[r], 0, chunk(r, c, nu))),
                  pl.BlockSpec((1, d, fc), lambda r, c, te, nu: (te[r], 0, n_fc + chunk(r, c, nu))),
                  pl.BlockSpec((1, fc, d), lambda r, c, te, nu: (te[r], chunk(r, c, nu), 0))],
        out_specs=pl.BlockSpec((tm, d), lambda r, c, te, nu: (r, 0)),
        scratch_shapes=[pltpu.VMEM((2, tm, d), F32), pltpu.VMEM((tm, d), BF16),
                        pltpu.SemaphoreType.DMA((2,))])
    return pl.pallas_call(
        functools.partial(_experts_kernel, tm=tm),
        grid_spec=grid_spec,
        out_shape=jax.ShapeDtypeStruct((n_tiles * tm, d), F32),
        compiler_params=_params("arbitrary", "arbitrary"),
        name="moe_experts",
    )(tile_expert, n_used, src_tiles, src_tiles, h, wgu_bf, wgu_bf, wd_bf)


def _combine_kernel(idx1_ref, idx2_ref, x3_ref, route_ref, ys_hbm, gfin_ref, out_ref, ybuf, sems, *, tm):
    _gather_rows_start(idx1_ref, ys_hbm, ybuf.at[0], sems.at[0], tm)
    _gather_rows_start(idx2_ref, ys_hbm, ybuf.at[1], sems.at[1], tm)
    route = route_ref[...]
    lane = lax.broadcasted_iota(jnp.int32, route.shape, 1)
    gate = lambda n: jnp.sum(jnp.where(lane == n, route, 0.0), axis=-1, keepdims=True)
    g1, g2 = gate(2), gate(3)
    _gather_rows_wait(ys_hbm, ybuf.at[0], sems.at[0], tm)
    _gather_rows_wait(ys_hbm, ybuf.at[1], sems.at[1], tm)
    out_ref[...] = _rms(x3_ref[...] + (g1 * ybuf[0] + g2 * ybuf[1]), gfin_ref[...])


def _moe_combine(x3, route, ys, slot1, slot2, g_final, tm):
    n, d = x3.shape
    idx = pl.BlockSpec((1, 1, tm), lambda t: (t, 0, 0), memory_space=pltpu.SMEM)
    row = lambda w: pl.BlockSpec((tm, w), lambda t: (t, 0))
    return pl.pallas_call(
        functools.partial(_combine_kernel, tm=tm),
        grid=(n // tm,),
        in_specs=[idx, idx, row(d), row(LANES), pl.BlockSpec(memory_space=pl.ANY), _const_spec((1, d))],
        out_specs=row(d),
        out_shape=jax.ShapeDtypeStruct((n, d), F32),
        scratch_shapes=[pltpu.VMEM((2, tm, d), F32), pltpu.SemaphoreType.DMA((2,))],
        compiler_params=_params("arbitrary"),
        name="moe_combine",
    )(slot1, slot2, x3, route, ys, g_final)


def _routing_tables(route, n_exp, tm):
    n = route.shape[0]
    experts = route[:, :2].astype(jnp.int32).reshape(-1)
    onehot = (experts[:, None] == jnp.arange(n_exp, dtype=jnp.int32)[None, :]).astype(jnp.int32)
    csum = jnp.cumsum(onehot, axis=0)
    rank = jnp.sum(csum * onehot, axis=1) - 1
    counts = csum[-1]
    padded = ((counts + tm - 1) // tm) * tm
    ends = jnp.cumsum(padded)
    starts = ends - padded
    slots = jnp.sum(starts[None, :] * onehot, axis=1) + rank
    n_tiles = (2 * n) // tm + n_exp
    src = jnp.zeros((n_tiles * tm,), jnp.int32).at[slots].set(jnp.arange(2 * n, dtype=jnp.int32) // 2)
    tile_start = jnp.arange(n_tiles, dtype=jnp.int32) * tm
    tile_expert = jnp.minimum(jnp.sum((tile_start[:, None] >= ends[None, :]).astype(jnp.int32), axis=1),
                              n_exp - 1)
    n_used = (ends[-1] // tm).reshape(1)
    slots = slots.reshape(n, 2)
    return slots[:, 0], slots[:, 1], src, tile_expert, n_used


def _out_moe(x, o, wo_bf, g, router_pad, n_exp, wgu_bf, wd_bf, g_final):
    n, d = x.shape
    tm = _row_tile(n, 512)
    x3, h, route = _moe_route(x, o, wo_bf, g, router_pad, n_exp)
    slot1, slot2, src, tile_expert, n_used = _routing_tables(route, n_exp, tm)
    ys = _moe_experts(h, src.reshape(-1, 1, tm), tile_expert, n_used, wgu_bf, wd_bf, tm)
    return _moe_combine(x3, route, ys, slot1.reshape(-1, 1, tm), slot2.reshape(-1, 1, tm), g_final, tm)


def _lane_block(x, start):
    return jnp.pad(x, ((0, 0), (start, LANES - start - x.shape[1])))


def _swap_halves(x):
    half = x.shape[-1] // 2
    return jnp.concatenate([x[..., half:], x[..., :half]], axis=-1)


def _prep_mla_weights(w_a, g_q, g_kv, w_uq, w_ukv, q_lora, kv_lora, n_heads, nope, rope):
    w_pe = w_a[:, q_lora + kv_lora:]
    w_pe_sw = _swap_halves(w_pe)
    wpe = jnp.concatenate([_lane_block(w_pe, 0), _lane_block(w_pe_sw, 0),
                           _lane_block(w_pe, nope), _lane_block(w_pe_sw, nope)], axis=1)
    uq = w_uq.reshape(q_lora, n_heads, nope + rope)
    uq_nope, uq_pe = uq[..., :nope], uq[..., nope:]
    tail = jnp.zeros((q_lora, n_heads, LANES - nope - rope), w_uq.dtype)
    wuq = jnp.concatenate([uq_nope, uq_pe, tail], axis=-1).reshape(q_lora, n_heads * LANES)
    wuqs = jnp.concatenate([jnp.zeros_like(uq_nope), _swap_halves(uq_pe), tail],
                           axis=-1).reshape(q_lora, n_heads * LANES)
    ukv = w_ukv.reshape(kv_lora, n_heads, -1)
    uk = ukv[..., :nope]
    wuk = jnp.concatenate([uk, jnp.zeros((kv_lora, n_heads, LANES - nope), uk.dtype)],
                          axis=-1).reshape(kv_lora, n_heads * LANES)
    wuv = ukv[..., nope:].reshape(kv_lora, -1)
    bf = lambda a: a.astype(BF16)
    return {"waq": bf(w_a[:, :q_lora]), "wakv": bf(w_a[:, q_lora:q_lora + kv_lora]), "wpe": bf(wpe),
            "gq": g_q[None, :], "gkv": g_kv[None, :], "wuq": bf(wuq), "wuqs": bf(wuqs),
            "wuk": bf(wuk), "wuv": bf(wuv)}


def _rope_tables(pos, nope, rope):
    half = rope // 2
    freqs = jnp.power(ROPE_THETA, -jnp.arange(half, dtype=F32) * 2.0 / rope)
    ang = pos.astype(F32)[:, None] * freqs[None, :]
    cos = jnp.concatenate([jnp.cos(ang)] * 2, axis=1)
    sin = jnp.concatenate([-jnp.sin(ang), jnp.sin(ang)], axis=1)
    c64 = _lane_block(cos, nope) + _lane_block(jnp.ones((pos.shape[0], nope), F32), 0)
    return _lane_block(cos, 0), _lane_block(sin, 0), c64, _lane_block(sin, nope)


def _bf16_split3(x):
    hi = x.astype(BF16).astype(F32)
    mid = (x - hi).astype(BF16).astype(F32)
    lo = (x - hi - mid).astype(BF16).astype(F32)
    return hi, mid, lo


def kernel(x_prompt, x_sample, cache_diff_k, cache_diff_v, cache_mla_ckv, cache_mla_kpe, norm_mix, norm_ffn, norm_final, diff_w_qkv, diff_lambda, diff_subln, diff_w_o, mla_w_a, mla_norm_q, mla_norm_kv, mla_w_uq, mla_w_ukv, mla_w_o, ffn_w_gu, ffn_w_down, moe_router, moe_w_gu, moe_w_down):
    b, t, d = x_prompt.shape
    db, nq, _ = x_sample.shape
    past = cache_diff_k.shape[2]
    n_da_heads, dv = cache_diff_k.shape[3], diff_subln.shape[-1]
    da_width = n_da_heads * dv
    q_lora, kv_lora = mla_norm_q.shape[-1], mla_norm_kv.shape[-1]
    rope = mla_w_a.shape[-1] - q_lora - kv_lora
    mla_width = mla_w_o.shape[1]
    n_exp = moe_router.shape[-1]
    assert norm_mix.shape[0] == 2, "one differential-attention layer followed by one MLA layer"

    uq_cols, ukv_cols = mla_w_uq.shape[-1], mla_w_ukv.shape[-1]
    n_mla_heads = (uq_cols - (ukv_cols - mla_width)) // rope
    nope = uq_cols // n_mla_heads - rope
    mla_v = mla_width // n_mla_heads
    assert n_mla_heads * (nope + mla_v) == ukv_cols and n_mla_heads % 2 == 0
    assert nope + rope <= LANES and 2 * mla_v == LANES

    bf = lambda a: a.astype(BF16)
    lam_init = 0.8 - 0.6 * math.exp(-0.3 * 0)
    da_scale = (dv // 2) ** -0.5 * LOG2E
    mla_scale = (nope + rope) ** -0.5 * LOG2E
    slopes2 = jnp.exp2(-8.0 * jnp.arange(1, n_da_heads + 1, dtype=F32) / n_da_heads) * LOG2E
    slope_parts = jnp.stack(_bf16_split3(slopes2) + (slopes2,), axis=1).reshape(-1)
    slope_rows = jnp.repeat(slopes2, 2 * nq)[:, None]

    w_qkv, w_o0 = bf(diff_w_qkv[0]), bf(diff_w_o[0])
    w_gu0, w_d0 = bf(ffn_w_gu[0]), bf(ffn_w_down[0])
    mla_w = _prep_mla_weights(mla_w_a[0], mla_norm_q[0], mla_norm_kv[0], mla_w_uq[0], mla_w_ukv[0],
                              q_lora, kv_lora, n_mla_heads, nope, rope)
    w_o1 = bf(mla_w_o[0])
    router_pad = jnp.pad(moe_router[0], ((0, 0), (0, LANES - n_exp)))
    moe_gu, moe_d = bf(moe_w_gu.reshape(moe_w_gu.shape[1:])), bf(moe_w_down.reshape(moe_w_down.shape[1:]))
    g_mix, g_ffn, g_fin = norm_mix[:, None, :], norm_ffn[:, None, :], norm_final[None, :]
    lam, subln = diff_lambda[0], diff_subln[0][None, :]

    def trunk(x, pos, attn0, attn1):
        bx, tx, _ = x.shape
        x2d = x.reshape(bx * tx, d)
        q, k, v, kx, vtx = _qkv_proj(x, g_mix[0], w_qkv, dv, da_scale)
        o = attn0(q, k, v, kx, vtx)
        x2 = _out_ffn(x2d, o.reshape(bx * tx, da_width), w_o0, g_ffn[0], w_gu0, w_d0)
        ckv, kpe, pe64, qcat, kcat, vtx1 = _mla_proj(x2.reshape(bx, tx, d), g_mix[1], mla_w,
                                                    _rope_tables(pos, nope, rope), rope, mla_v,
                                                    mla_scale)
        o = attn1(ckv, pe64, qcat, kcat, vtx1)
        y = _out_moe(x2, o.reshape(bx * tx, mla_width), w_o1, g_ffn[1], router_pad, n_exp,
                     moe_gu, moe_d, g_fin)
        return y, k, v, ckv, kpe

    y_p, k_p, v_p, ckv_p, kpe_p = trunk(
        x_prompt, jnp.arange(t, dtype=jnp.int32),
        lambda q, k, v, kx, vtx: _diff_attn_prompt(q, kx, vtx, lam, subln, slope_parts, lam_init),
        lambda ckv, pe64, qcat, kcat, vtx1: _mla_attn_prompt(qcat, kcat, vtx1, mla_v))

    n_s = db * nq
    cache_k = cache_diff_k.reshape(cache_diff_k.shape[1:])
    cache_v = cache_diff_v.reshape(cache_diff_v.shape[1:])
    cache_pe = jnp.pad(cache_mla_kpe[0], ((0, 0), (0, 0), (nope, LANES - nope - rope)))
    pos_s = past + jnp.tile(jnp.arange(nq, dtype=jnp.int32), db)

    def attn0_s(q, k, v, kx, vtx):
        r = lambda a: a.reshape(db, nq, da_width)
        return _diff_attn_decode(r(q), cache_k, cache_v, r(k), r(v), lam, subln, slope_rows, lam_init)

    def attn1_s(ckv, pe64, qcat, kcat, vtx1):
        return _mla_attn_decode(qcat.reshape(db, nq, -1), cache_mla_ckv[0], cache_pe,
                                ckv.reshape(db, nq, kv_lora), pe64.reshape(db, nq, LANES),
                                mla_w["wuk"], mla_w["wuv"])

    y_s, k_s, v_s, ckv_s, kpe_s = trunk(x_sample.reshape(1, n_s, d), pos_s, attn0_s, attn1_s)

    return (y_p.reshape(b, t, d), y_s.reshape(db, nq, d),
            k_p.reshape(1, b, t, n_da_heads, dv), v_p.reshape(1, b, t, n_da_heads, dv),
            ckv_p.reshape(1, b, t, kv_lora), kpe_p.reshape(1, b, t, rope),
            k_s.reshape(1, db, nq, n_da_heads, dv), v_s.reshape(1, db, nq, n_da_heads, dv),
            ckv_s.reshape(1, db, nq, kv_lora), kpe_s.reshape(1, db, nq, rope))
```

```python
import functools
import math

import jax
import jax.numpy as jnp
from jax import lax
from jax.experimental import pallas as pl
from jax.experimental.pallas import tpu as pltpu

NORM_EPS = 1e-6
NEG_INF = -1e30
CHUNK = 64
ROPE_THETA = 10000.0
LOG2E = math.log2(math.e)
LANES = 128
MXU_DIM = 256
BF16_ROWS = 16
EXPERT_CHUNK_CAP = 2048
FLASH_Q_BLOCK = 1024
VMEM_LIMIT_BYTES = 56 * 2**20
BF16 = jnp.bfloat16
F32 = jnp.float32


def _params(*sem):
    return pltpu.CompilerParams(dimension_semantics=sem, vmem_limit_bytes=VMEM_LIMIT_BYTES)


def _const_spec(shape):
    nd = len(shape)
    return pl.BlockSpec(shape, lambda *_: (0,) * nd, pipeline_mode=pl.Buffered(1))


def _rms(x, g):
    ms = jnp.mean(x * x, axis=-1, keepdims=True)
    return x * lax.rsqrt(ms + NORM_EPS) * g


def _dot(a, b):
    return jnp.dot(a, b, preferred_element_type=F32)


def _dot_nt(a, b):
    return lax.dot_general(a, b, (((1,), (1,)), ((), ())), preferred_element_type=F32)


def _row_tile(t, cap):
    tm = min(t, cap)
    assert t % tm == 0
    return tm


def _ones_row_block(n_cols):
    sub = lax.broadcasted_iota(jnp.int32, (BF16_ROWS, n_cols), 0)
    return jnp.where(sub == 0, 1.0, 0.0).astype(BF16)


def _qkv_kernel(x_ref, g_ref, w_ref, q_ref, k_ref, v_ref, kx_ref, vtx_ref, *, width, dv, scale):
    tm = x_ref.shape[1]
    h = _rms(x_ref[0], g_ref[...]).astype(BF16)
    qkv = _dot(h, w_ref[...])
    q_ref[0] = (qkv[:, :width] * scale).astype(BF16)
    k = qkv[:, width:2 * width]
    v = qkv[:, 2 * width:]
    k_ref[0] = k
    v_ref[0] = v
    k_bf = k.astype(BF16)
    vt = v.T.astype(BF16)
    pos = lax.broadcasted_iota(jnp.int32, (tm, LANES), 0) + pl.program_id(1) * tm
    lane = lax.broadcasted_iota(jnp.int32, (tm, LANES), 1)
    pos_feat = jnp.where(lane < 3, pos // LANES, jnp.where(lane < 6, pos % LANES, 0))
    pos_feat = pos_feat.astype(F32).astype(BF16)
    ones = _ones_row_block(tm)
    for hh in range(width // dv):
        kx_ref[0, :, 2 * dv * hh:2 * dv * hh + dv] = k_bf[:, hh * dv:(hh + 1) * dv]
        kx_ref[0, :, 2 * dv * hh + dv:2 * dv * (hh + 1)] = pos_feat
        r0 = hh * (dv + BF16_ROWS)
        vtx_ref[0, r0:r0 + dv, :] = vt[hh * dv:(hh + 1) * dv, :]
        vtx_ref[0, r0 + dv:r0 + dv + BF16_ROWS, :] = ones


def _qkv_proj(x, g, w_bf, dv, scale):
    bx, tx, d = x.shape
    width = w_bf.shape[1] // 3
    n_heads = width // dv
    assert dv == LANES and tx <= LANES * MXU_DIM
    tm = _row_tile(tx, 512)
    row = lambda n: pl.BlockSpec((1, tm, n), lambda b, t: (b, t, 0))
    vrows = n_heads * (dv + BF16_ROWS)
    return pl.pallas_call(
        functools.partial(_qkv_kernel, width=width, dv=dv, scale=scale),
        grid=(bx, tx // tm),
        in_specs=[row(d), _const_spec((1, d)), _const_spec(w_bf.shape)],
        out_specs=[row(width), row(width), row(width), row(2 * width),
                   pl.BlockSpec((1, vrows, tm), lambda b, t: (b, 0, t))],
        out_shape=[jax.ShapeDtypeStruct((bx, tx, width), BF16),
                   jax.ShapeDtypeStruct((bx, tx, width), F32),
                   jax.ShapeDtypeStruct((bx, tx, width), F32),
                   jax.ShapeDtypeStruct((bx, tx, 2 * width), BF16),
                   jax.ShapeDtypeStruct((bx, vrows, tx), BF16)],
        compiler_params=_params("parallel", "parallel"),
        name="qkv_proj",
    )(x, g, w_bf)


def _flash_streams(i, tq, key_op, query_op, value_op, diag_tile, z_a, z_b, cm_a, cm_b, m, acc,
                   consume_first):
    n_streams = 2
    tk = tq // 2

    units = [(s, slice(n * MXU_DIM, (n + 1) * MXU_DIM))
             for s in range(n_streams) for n in range(tq // MXU_DIM)]

    def produce(blk, z, cm, extra, s, cols):
        off = pl.multiple_of(blk * tk, tk)
        scores = _dot_nt(key_op(s, off), query_op(s, cols))
        if extra is not None:
            scores = scores + extra[:, cols]
        z[s, :, cols] = scores
        cm[s, :, cols] = jnp.max(scores, axis=0, keepdims=True)

    def consume(blk, z, cm, s, cols):
        off = pl.multiple_of(blk * tk, tk)
        m_old = m[s, :, cols]
        m_new = jnp.maximum(m_old, cm[s, :, cols])
        p = jnp.exp2(z[s, :, cols] - m_new).astype(BF16)
        acc[s, :, cols] = jnp.exp2(m_old - m_new) * acc[s, :, cols] + _dot(value_op(s, off), p)
        m[s, :, cols] = m_new

    def step(p_blk, p_z, p_cm, extra, c_blk, c_z, c_cm):
        for s, cols in units:
            if consume_first:
                consume(c_blk, c_z, c_cm, s, cols)
            produce(p_blk, p_z, p_cm, extra, s, cols)
            if not consume_first:
                consume(c_blk, c_z, c_cm, s, cols)

    m[...] = jnp.full(m.shape, NEG_INF, F32)
    acc[...] = jnp.zeros(acc.shape, F32)
    first_tile = diag_tile(0)
    for s, cols in units:
        produce(2 * i, z_a, cm_a, first_tile, s, cols)

    def pair(u):
        step(2 * u, z_b, cm_b, None, jnp.where(u == 0, 2 * i, 2 * u - 1), z_a, cm_a)
        step(2 * u + 1, z_a, cm_a, None, 2 * u, z_b, cm_b)

    odd = i % 2

    @pl.when(odd == 1)
    def _():
        pair(0)

    def body(v, carry):
        pair(odd + 2 * v)
        pair(odd + 2 * v + 1)
        return carry

    lax.fori_loop(0, i // 2, body, 0)

    last_tile = diag_tile(1)
    last_a = jnp.where(i == 0, 0, 2 * i - 1)
    late_units = [(s, cols) for s, cols in units if cols.stop > tk]
    for s, cols in units:
        if cols.stop > tk:
            produce(2 * i + 1, z_b, cm_b, last_tile, s, cols)
        consume(last_a, z_a, cm_a, s, cols)
    for s, cols in late_units:
        consume(2 * i + 1, z_b, cm_b, s, cols)


def _local_positions(tq, n):
    tk = tq // 2
    kl = lax.broadcasted_iota(jnp.int32, (tk, tq), 0) + n * tk
    ql = lax.broadcasted_iota(jnp.int32, (tk, tq), 1)
    return kl, ql, (kl // CHUNK) <= (ql // CHUNK)


def _lambda_full(lam_ref, lam_init):
    lf = lam_ref[...]
    s01 = jnp.sum(lf[0:1, :] * lf[1:2, :], axis=-1, keepdims=True)
    s23 = jnp.sum(lf[2:3, :] * lf[3:4, :], axis=-1, keepdims=True)
    return jnp.exp(s01) - jnp.exp(s23) + lam_init


def _diff_flash_kernel(slope_ref, q_ref, kx_ref, vtx_ref, lam_ref, subln_ref, o_ref,
                       qx, z_a, z_b, cm_a, cm_b, m, acc, *, tq, dv, lam_init):
    h = pl.program_id(1)
    i = pl.program_id(2)
    s_hi, s_mid, s_lo, slope2 = (slope_ref[4 * h + n] for n in range(4))
    q = q_ref[0]
    lane = lax.broadcasted_iota(jnp.int32, q.shape, 1)
    zero = jnp.zeros_like(q)
    slope_feat = jnp.zeros(q.shape, F32)
    for n, val in enumerate((LANES * s_hi, LANES * s_mid, LANES * s_lo, s_hi, s_mid, s_lo)):
        slope_feat = jnp.where(lane == n, val, slope_feat)
    slope_feat = slope_feat.astype(BF16)
    qx[0, :, 0:dv] = jnp.where(lane < dv // 2, q, zero)
    qx[1, :, 0:dv] = jnp.where(lane >= dv // 2, q, zero)
    qx[0, :, dv:2 * dv] = slope_feat
    qx[1, :, dv:2 * dv] = slope_feat

    def diag_tile(n):
        kl, ql, visible = _local_positions(tq, n)
        return jnp.where(visible, (-2.0 * slope2) * jnp.maximum(kl - ql, 0).astype(F32), NEG_INF)

    tk = tq // 2
    _flash_streams(i, tq,
                   key_op=lambda s, off: kx_ref[0, pl.ds(off, tk), :],
                   query_op=lambda s, cols: qx[s, cols, :],
                   value_op=lambda s, off: vtx_ref[0, :, pl.ds(off, tk)],
                   diag_tile=diag_tile, z_a=z_a, z_b=z_b, cm_a=cm_a, cm_b=cm_b, m=m, acc=acc,
                   consume_first=True)

    lam = _lambda_full(lam_ref, lam_init)
    o_t = (acc[0, 0:dv, :] / acc[0, dv:dv + 1, :]
           - lam * (acc[1, 0:dv, :] / acc[1, dv:dv + 1, :]))
    o = _rms(o_t.T, subln_ref[...]) * (1.0 - lam_init)
    o_ref[0] = o.astype(BF16)


def _flash_scratch(tq, q_feat, acc_rows):
    return ([pltpu.VMEM((2, tq, q_feat), BF16)] if q_feat else []) + [
        pltpu.VMEM((2, tq // 2, tq), F32), pltpu.VMEM((2, tq // 2, tq), F32),
        pltpu.VMEM((2, 1, tq), F32), pltpu.VMEM((2, 1, tq), F32),
        pltpu.VMEM((2, 1, tq), F32), pltpu.VMEM((2, acc_rows, tq), F32)]


def _diff_attn_prompt(q, kx, vtx, lam, subln, slope_parts, lam_init):
    b, t, width = q.shape
    dv = subln.shape[-1]
    n_heads = width // dv
    tq = _row_tile(t, FLASH_Q_BLOCK)
    assert tq % (2 * MXU_DIM) == 0
    vrows = dv + BF16_ROWS
    return pl.pallas_call(
        functools.partial(_diff_flash_kernel, tq=tq, dv=dv, lam_init=lam_init),
        grid=(b, n_heads, t // tq),
        in_specs=[pl.BlockSpec(memory_space=pltpu.SMEM),
                  pl.BlockSpec((1, tq, dv), lambda b_, h, i: (b_, i, h)),
                  pl.BlockSpec((1, t, 2 * dv), lambda b_, h, i: (b_, 0, h)),
                  pl.BlockSpec((1, vrows, t), lambda b_, h, i: (b_, h, 0)),
                  pl.BlockSpec(lam.shape, lambda b_, h, i: (0, 0)),
                  pl.BlockSpec(subln.shape, lambda b_, h, i: (0, 0))],
        out_specs=pl.BlockSpec((1, tq, dv), lambda b_, h, i: (b_, i, h)),
        out_shape=jax.ShapeDtypeStruct((b, t, width), BF16),
        scratch_shapes=_flash_scratch(tq, 2 * dv, vrows),
        compiler_params=_params("parallel", "parallel", "arbitrary"),
        name="diff_attn_prompt",
    )(slope_parts, q, kx, vtx, lam, subln)


def _mla_flash_kernel(q_ref, k_ref, vtx_ref, o_ref, z_a, z_b, cm_a, cm_b, m, acc, *, tq, dv):
    i = pl.program_id(2)
    vrows = dv + BF16_ROWS
    tk = tq // 2
    _flash_streams(i, tq,
                   key_op=lambda s, off: k_ref[0, pl.ds(off, tk), s * LANES:(s + 1) * LANES],
                   query_op=lambda s, cols: q_ref[0, cols, s * LANES:(s + 1) * LANES],
                   value_op=lambda s, off: vtx_ref[0, s * vrows:(s + 1) * vrows, pl.ds(off, tk)],
                   diag_tile=lambda n: jnp.where(_local_positions(tq, n)[2], 0.0, NEG_INF),
                   z_a=z_a, z_b=z_b, cm_a=cm_a, cm_b=cm_b, m=m, acc=acc, consume_first=False)
    o_t = jnp.concatenate([acc[s, 0:dv, :] / acc[s, dv:dv + 1, :] for s in range(2)], axis=0)
    o_ref[0] = o_t.T.astype(BF16)


def _mla_attn_prompt(qcat, kcat, vtx, dv):
    b, t, wq = qcat.shape
    n_pairs = wq // (2 * LANES)
    tq = _row_tile(t, FLASH_Q_BLOCK)
    assert tq % (2 * MXU_DIM) == 0
    vrows = dv + BF16_ROWS
    return pl.pallas_call(
        functools.partial(_mla_flash_kernel, tq=tq, dv=dv),
        grid=(b, n_pairs, t // tq),
        in_specs=[pl.BlockSpec((1, tq, 2 * LANES), lambda b_, h, i: (b_, i, h)),
                  pl.BlockSpec((1, t, 2 * LANES), lambda b_, h, i: (b_, 0, h)),
                  pl.BlockSpec((1, 2 * vrows, t), lambda b_, h, i: (b_, h, 0))],
        out_specs=pl.BlockSpec((1, tq, 2 * dv), lambda b_, h, i: (b_, i, h)),
        out_shape=jax.ShapeDtypeStruct((b, t, n_pairs * 2 * dv), BF16),
        scratch_shapes=_flash_scratch(tq, 0, vrows),
        compiler_params=_params("parallel", "parallel", "arbitrary"),
        name="mla_attn_prompt",
    )(qcat, kcat, vtx)


def _block_diag_queries(q, n_groups, group_width):
    qt = jnp.concatenate([q] * n_groups, axis=0)
    r = lax.broadcasted_iota(jnp.int32, qt.shape, 0)
    c = lax.broadcasted_iota(jnp.int32, qt.shape, 1)
    return jnp.where((c // group_width) == (r // q.shape[0]), qt, jnp.zeros_like(qt))


def _decode_update(z, v_bf, m_ref, l_ref, acc_ref):
    m_old = m_ref[...]
    m_new = jnp.maximum(m_old, jnp.max(z, axis=-1, keepdims=True))
    alpha = jnp.exp2(m_old - m_new)
    p = jnp.exp2(z - m_new)
    l_ref[...] = alpha * l_ref[...] + jnp.sum(p, axis=-1, keepdims=True)
    acc_ref[...] = alpha * acc_ref[...] + _dot(p.astype(BF16), v_bf)
    m_ref[...] = m_new


def _decode_positions(n_rows, n_keys, nq, past, key_start, n_valid):
    r = lax.broadcasted_iota(jnp.int32, (n_rows, n_keys), 0)
    c = lax.broadcasted_iota(jnp.int32, (n_rows, n_keys), 1)
    q_pos = past + r % nq
    k_pos = key_start + c
    visible = ((k_pos // CHUNK) <= (q_pos // CHUNK)) & (c < n_valid)
    return q_pos, k_pos, visible


def _pad_rows(x, n):
    return jnp.concatenate([x, jnp.zeros((n - x.shape[0], x.shape[1]), x.dtype)], axis=0)


def _diff_decode_kernel(slope_ref, q_ref, ck_ref, cv_ref, kn_ref, vn_ref, lam_ref, subln_ref,
                        o_ref, qrows, qbd, bias, m, l, acc, *, past, tk, nq, dv, lam_init):
    j = pl.program_id(1)
    n_rows = qrows.shape[0]
    n_heads = n_rows // (2 * nq)
    n_cols = tk * n_heads

    @pl.when(j == 0)
    def _():
        q = q_ref[0]
        qbd[...] = _block_diag_queries(q, 2 * n_heads, dv // 2)
        lane = lax.broadcasted_iota(jnp.int32, (nq, dv), 1)
        for hh in range(n_heads):
            q_h = q[:, hh * dv:(hh + 1) * dv]
            r0 = hh * 2 * nq
            qrows[r0:r0 + nq, :] = jnp.where(lane < dv // 2, q_h, jnp.zeros_like(q_h))
            qrows[r0 + nq:r0 + 2 * nq, :] = jnp.where(lane >= dv // 2, q_h, jnp.zeros_like(q_h))
        r = lax.broadcasted_iota(jnp.int32, (n_rows, n_cols), 0)
        c = lax.broadcasted_iota(jnp.int32, (n_rows, n_cols), 1)
        own_head = (c % n_heads) == (r // (2 * nq))
        rel = (c // n_heads - r % nq).astype(F32)
        bias[...] = jnp.where(own_head, slope_ref[...] * rel, NEG_INF)
        m[...] = jnp.full(m.shape, NEG_INF, F32)
        l[...] = jnp.zeros(l.shape, F32)
        acc[...] = jnp.zeros(acc.shape, F32)

    def update(z, pv):
        m_old = m[...]
        m_new = jnp.maximum(m_old, jnp.max(z, axis=-1, keepdims=True))
        alpha = jnp.exp2(m_old - m_new)
        p = jnp.exp2(z - m_new)
        l[...] = alpha * l[...] + jnp.sum(p, axis=-1, keepdims=True)
        acc[...] = alpha * acc[...] + pv(p.astype(BF16))
        m[...] = m_new

    k2 = ck_ref[0].reshape(n_cols, dv).astype(BF16)
    v2 = cv_ref[0].reshape(n_cols, dv).astype(BF16)
    shift = slope_ref[...] * (past - j * tk).astype(F32)
    update(_dot_nt(qrows[...], k2) + bias[...] - shift, lambda p: _dot(p, v2))

    @pl.when(j == pl.num_programs(1) - 1)
    def _():
        k_new = _pad_rows(kn_ref[0], LANES).astype(BF16)
        v_new = _pad_rows(vn_ref[0], LANES).astype(BF16)
        q_pos, k_pos, visible = _decode_positions(n_rows, LANES, nq, past, past, nq)
        dist = jnp.abs(q_pos - k_pos).astype(F32)
        z = jnp.where(visible, _dot_nt(qbd[...], k_new) - slope_ref[...] * dist, NEG_INF)

        def pv_new(p):
            full = _dot(p, v_new)
            return jnp.concatenate([full[hh * 2 * nq:(hh + 1) * 2 * nq, hh * dv:(hh + 1) * dv]
                                    for hh in range(n_heads)], axis=0)

        update(z, pv_new)
        lam = _lambda_full(lam_ref, lam_init)
        for hh in range(n_heads):
            r0 = hh * 2 * nq
            o1 = acc[r0:r0 + nq, :] / l[r0:r0 + nq, :]
            o2 = acc[r0 + nq:r0 + 2 * nq, :] / l[r0 + nq:r0 + 2 * nq, :]
            o = _rms(o1 - lam * o2, subln_ref[...]) * (1.0 - lam_init)
            o_ref[0, :, hh * dv:(hh + 1) * dv] = o.astype(BF16)


def _diff_attn_decode(q, cache_k, cache_v, k_new, v_new, lam, subln, slope_rows, lam_init):
    db, nq, width = q.shape
    past, n_heads, dv = cache_k.shape[1:]
    n_rows = n_heads * 2 * nq
    tk = _row_tile(past, 512)
    stream = lambda n: pl.BlockSpec((1, nq, n), lambda b, j: (b, 0, 0))
    cache = pl.BlockSpec((1, tk, n_heads, dv), lambda b, j: (b, j, 0, 0))
    return pl.pallas_call(
        functools.partial(_diff_decode_kernel, past=past, tk=tk, nq=nq, dv=dv, lam_init=lam_init),
        grid=(db, past // tk),
        in_specs=[_const_spec(slope_rows.shape), stream(width), cache, cache, stream(width),
                  stream(width), _const_spec(lam.shape), _const_spec(subln.shape)],
        out_specs=stream(width),
        out_shape=jax.ShapeDtypeStruct((db, nq, width), BF16),
        scratch_shapes=[pltpu.VMEM((n_rows, dv), BF16), pltpu.VMEM((n_rows, width), BF16),
                        pltpu.VMEM((n_rows, tk * n_heads), F32), pltpu.VMEM((n_rows, 1), F32),
                        pltpu.VMEM((n_rows, 1), F32), pltpu.VMEM((n_rows, dv), F32)],
        compiler_params=_params("parallel", "arbitrary"),
        name="diff_attn_decode",
    )(slope_rows, q, cache_k, cache_v, k_new, v_new, lam, subln)


def _mla_decode_kernel(q_ref, cc_ref, cpe_ref, cn_ref, pen_ref, wuk_ref, wuv_ref, o_ref,
                       q_lat, q_pe, m, l, acc, *, past, tk, nq, dv):
    j = pl.program_id(1)
    n_rows = q_lat.shape[0]
    n_heads = n_rows // nq

    @pl.when(j == 0)
    def _():
        qbd = _block_diag_queries(q_ref[0], n_heads, LANES)
        q_lat[...] = _dot_nt(qbd, wuk_ref[...]).astype(BF16)
        q_h = qbd[:, 0:LANES].astype(F32)
        for hh in range(1, n_heads):
            q_h = q_h + qbd[:, hh * LANES:(hh + 1) * LANES].astype(F32)
        q_pe[...] = q_h.astype(BF16)
        m[...] = jnp.full(m.shape, NEG_INF, F32)
        l[...] = jnp.zeros(l.shape, F32)
        acc[...] = jnp.zeros(acc.shape, F32)

    def step(c_f32, pe_f32, key_start, n_valid):
        n_keys = c_f32.shape[0]
        c_bf = c_f32.astype(BF16)
        s = _dot_nt(q_lat[...], c_bf) + _dot_nt(q_pe[...], pe_f32.astype(BF16))
        _, _, visible = _decode_positions(n_rows, n_keys, nq, past, key_start, n_valid)
        _decode_update(jnp.where(visible, s, NEG_INF), c_bf, m, l, acc)

    step(cc_ref[0], cpe_ref[0], j * tk, tk)

    @pl.when(j == pl.num_programs(1) - 1)
    def _():
        step(_pad_rows(cn_ref[0], LANES), _pad_rows(pen_ref[0], LANES), past, nq)
        o_full = _dot((acc[...] / l[...]).astype(BF16), wuv_ref[...])
        lane = lax.broadcasted_iota(jnp.int32, (nq, 2 * dv), 1)
        for pair in range(n_heads // 2):
            r0 = pair * 2 * nq
            cols = slice(pair * 2 * dv, (pair + 1) * 2 * dv)
            o_ref[0, :, cols] = jnp.where(lane < dv, o_full[r0:r0 + nq, cols],
                                          o_full[r0 + nq:r0 + 2 * nq, cols]).astype(BF16)


def _mla_attn_decode(qcat, cache_c, cache_pe, c_new, pe_new, wuk_bf, wuv_bf):
    db, nq, wq = qcat.shape
    past, lora = cache_c.shape[1], cache_c.shape[2]
    n_heads = wq // LANES
    wv = wuv_bf.shape[1]
    dv = wv // n_heads
    n_rows = n_heads * nq
    tk = _row_tile(past, 512)
    stream = lambda n: pl.BlockSpec((1, nq, n), lambda b, j: (b, 0, 0))
    return pl.pallas_call(
        functools.partial(_mla_decode_kernel, past=past, tk=tk, nq=nq, dv=dv),
        grid=(db, past // tk),
        in_specs=[stream(wq),
                  pl.BlockSpec((1, tk, lora), lambda b, j: (b, j, 0)),
                  pl.BlockSpec((1, tk, LANES), lambda b, j: (b, j, 0)),
                  stream(lora), stream(LANES),
                  _const_spec(wuk_bf.shape), _const_spec(wuv_bf.shape)],
        out_specs=stream(wv),
        out_shape=jax.ShapeDtypeStruct((db, nq, wv), BF16),
        scratch_shapes=[pltpu.VMEM((n_rows, lora), BF16), pltpu.VMEM((n_rows, LANES), BF16),
                        pltpu.VMEM((n_rows, 1), F32), pltpu.VMEM((n_rows, 1), F32),
                        pltpu.VMEM((n_rows, lora), F32)],
        compiler_params=_params("parallel", "arbitrary"),
        name="mla_attn_decode",
    )(qcat, cache_c, cache_pe, c_new, pe_new, wuk_bf, wuv_bf)


def _silu(x):
    return x * jax.nn.sigmoid(x)


def _out_ffn_kernel(x_ref, o_ref, wo_ref, g_ref, wgu_ref, wd_ref, out_ref, *, f, n_chunks):
    x1 = x_ref[...] + _dot(o_ref[...], wo_ref[...])
    h = _rms(x1, g_ref[...]).astype(BF16)
    fc = f // n_chunks
    acc = x1
    for c in range(n_chunks):
        gate = _dot(h, wgu_ref[:, c * fc:(c + 1) * fc])
        up = _dot(h, wgu_ref[:, f + c * fc:f + (c + 1) * fc])
        acc = acc + _dot((_silu(gate) * up).astype(BF16), wd_ref[c * fc:(c + 1) * fc, :])
    out_ref[...] = acc


def _out_ffn(x, o, wo_bf, g, wgu_bf, wd_bf):
    n, d = x.shape
    f = wd_bf.shape[0]
    tm = _row_tile(n, 512)
    n_chunks = 2 if f % (2 * LANES) == 0 else 1
    row = lambda w: pl.BlockSpec((tm, w), lambda t: (t, 0))
    return pl.pallas_call(
        functools.partial(_out_ffn_kernel, f=f, n_chunks=n_chunks),
        grid=(n // tm,),
        in_specs=[row(d), row(o.shape[1]), _const_spec(wo_bf.shape), _const_spec((1, d)),
                  _const_spec(wgu_bf.shape), _const_spec(wd_bf.shape)],
        out_specs=row(d),
        out_shape=jax.ShapeDtypeStruct((n, d), F32),
        compiler_params=_params("parallel"),
        name="attn_out_ffn",
    )(x, o, wo_bf, g, wgu_bf, wd_bf)


def _mla_proj_kernel(x_ref, g_ref, waq_ref, wakv_ref, wpe_ref, gq_ref, gkv_ref, wuq_ref, wuqs_ref,
                     wuk_ref, wuv_ref, c0_ref, s0_ref, c64_ref, s64_ref,
                     ckv_ref, kpe_ref, pe64_ref, q_ref, k_ref, vtx_ref, *, rope, dv, scale):
    tm = x_ref.shape[1]
    h = _rms(x_ref[0], g_ref[...]).astype(BF16)
    c_q = _rms(_dot(h, waq_ref[...]), gq_ref[...]).astype(BF16)
    c_kv = _rms(_dot(h, wakv_ref[...]), gkv_ref[...])
    ckv_ref[0] = c_kv
    c_kv_bf = c_kv.astype(BF16)
    pe = _dot(h, wpe_ref[...])
    kpe0 = pe[:, 0:LANES] * c0_ref[...] + pe[:, LANES:2 * LANES] * s0_ref[...]
    kpe_ref[0] = kpe0[:, :rope]
    c64 = c64_ref[...]
    s64 = s64_ref[...]
    pe64 = pe[:, 2 * LANES:3 * LANES] * c64 + pe[:, 3 * LANES:] * s64
    pe64_ref[0] = pe64
    q = _dot(c_q, wuq_ref[...])
    q_sw = _dot(c_q, wuqs_ref[...])
    k = _dot(c_kv_bf, wuk_ref[...])
    for hh in range(q.shape[1] // LANES):
        cols = slice(hh * LANES, (hh + 1) * LANES)
        q_ref[0, :, cols] = ((q[:, cols] * c64 + q_sw[:, cols] * s64) * scale).astype(BF16)
        k_ref[0, :, cols] = (k[:, cols] + pe64).astype(BF16)
    vt = _dot(c_kv_bf, wuv_ref[...]).T.astype(BF16)
    ones = _ones_row_block(tm)
    for hh in range(vt.shape[0] // dv):
        r0 = hh * (dv + BF16_ROWS)
        vtx_ref[0, r0:r0 + dv, :] = vt[hh * dv:(hh + 1) * dv, :]
        vtx_ref[0, r0 + dv:r0 + dv + BF16_ROWS, :] = ones


def _mla_proj(x, g, w, tabs, rope, dv, scale):
    bx, tx, d = x.shape
    tm = _row_tile(tx, 256)
    lora = w["wakv"].shape[1]
    wq, wv = w["wuq"].shape[1], w["wuv"].shape[1]
    vrows = (wv // dv) * (dv + BF16_ROWS)
    row = lambda n: pl.BlockSpec((1, tm, n), lambda b, t: (b, t, 0))
    tab = pl.BlockSpec((tm, LANES), lambda b, t: (t, 0))
    weights = [w[k] for k in ("waq", "wakv", "wpe", "gq", "gkv", "wuq", "wuqs", "wuk", "wuv")]
    return pl.pallas_call(
        functools.partial(_mla_proj_kernel, rope=rope, dv=dv, scale=scale),
        grid=(bx, tx // tm),
        in_specs=[row(d), _const_spec((1, d))] + [_const_spec(a.shape) for a in weights] + [tab] * 4,
        out_specs=[row(lora), row(rope), row(LANES), row(wq), row(wq),
                   pl.BlockSpec((1, vrows, tm), lambda b, t: (b, 0, t))],
        out_shape=[jax.ShapeDtypeStruct((bx, tx, lora), F32),
                   jax.ShapeDtypeStruct((bx, tx, rope), F32),
                   jax.ShapeDtypeStruct((bx, tx, LANES), F32),
                   jax.ShapeDtypeStruct((bx, tx, wq), BF16),
                   jax.ShapeDtypeStruct((bx, tx, wq), BF16),
                   jax.ShapeDtypeStruct((bx, vrows, tx), BF16)],
        compiler_params=_params("parallel", "parallel"),
        name="mla_proj",
    )(x, g, *weights, *tabs)


def _route_kernel(x_ref, o_ref, wo_ref, g_ref, router_ref, x3_ref, h_ref, route_ref, *, n_exp):
    x3 = x_ref[...] + _dot(o_ref[...], wo_ref[...])
    h = _rms(x3, g_ref[...])
    lane = lax.broadcasted_iota(jnp.int32, route_ref.shape, 1).astype(F32)
    logits = jnp.dot(h, router_ref[...], preferred_element_type=F32,
                     precision=lax.Precision.HIGHEST)
    logits = jnp.where(lane < n_exp, logits, -jnp.inf)
    v1 = jnp.max(logits, axis=-1, keepdims=True)
    i1 = jnp.min(jnp.where(logits == v1, lane, float(LANES)), axis=-1, keepdims=True)
    rest = jnp.where(lane == i1, -jnp.inf, logits)
    v2 = jnp.max(rest, axis=-1, keepdims=True)
    i2 = jnp.min(jnp.where(rest == v2, lane, float(LANES)), axis=-1, keepdims=True)
    e2 = jnp.exp(v2 - v1)
    denom = 1.0 + e2
    route = jnp.zeros(route_ref.shape, F32)
    for n, val in enumerate((i1, i2, 1.0 / denom, e2 / denom)):
        route = jnp.where(lane == n, val, route)
    x3_ref[...] = x3
    h_ref[...] = h
    route_ref[...] = route


def _moe_route(x, o, wo_bf, g, router_pad, n_exp):
    n, d = x.shape
    tm = _row_tile(n, 512)
    row = lambda w: pl.BlockSpec((tm, w), lambda t: (t, 0))
    return pl.pallas_call(
        functools.partial(_route_kernel, n_exp=n_exp),
        grid=(n // tm,),
        in_specs=[row(d), row(o.shape[1]), _const_spec(wo_bf.shape), _const_spec((1, d)),
                  _const_spec(router_pad.shape)],
        out_specs=[row(d), row(d), row(LANES)],
        out_shape=[jax.ShapeDtypeStruct((n, d), F32), jax.ShapeDtypeStruct((n, d), F32),
                   jax.ShapeDtypeStruct((n, LANES), F32)],
        compiler_params=_params("parallel"),
        name="moe_route",
    )(x, o, wo_bf, g, router_pad)


def _gather_rows_start(idx_ref, src_hbm, dst, sem, n_rows):
    def body(t, carry):
        pltpu.make_async_copy(src_hbm.at[pl.ds(idx_ref[0, 0, t], 1)], dst.at[pl.ds(t, 1)], sem).start()
        return carry
    lax.fori_loop(0, n_rows, body, 0, unroll=8)


def _gather_rows_wait(src_hbm, dst, sem, n_rows):
    pltpu.make_async_copy(src_hbm.at[pl.ds(0, n_rows)], dst, sem).wait()


def _experts_kernel(te_ref, nu_ref, idx_ref, idx_next_ref, h_hbm, wg_ref, wu_ref, wd_ref, ys_ref,
                    hbuf, hb, sems, *, tm):
    r = pl.program_id(0)
    c = pl.program_id(1)
    n_used = nu_ref[0]
    slot = r % 2

    @pl.when((c == 0) & (r < n_used))
    def _():
        @pl.when(r == 0)
        def _():
            _gather_rows_start(idx_ref, h_hbm, hbuf.at[0], sems.at[0], tm)

        @pl.when(r + 1 < n_used)
        def _():
            _gather_rows_start(idx_next_ref, h_hbm, hbuf.at[1 - slot], sems.at[1 - slot], tm)

        _gather_rows_wait(h_hbm, hbuf.at[slot], sems.at[slot], tm)
        hb[...] = hbuf[slot].astype(BF16)

    @pl.when(r < n_used)
    def _():
        h = hb[...]
        act = (_silu(_dot(h, wg_ref[0])) * _dot(h, wu_ref[0])).astype(BF16)
        y = _dot(act, wd_ref[0])

        @pl.when(c == 0)
        def _():
            ys_ref[...] = y

        @pl.when(c > 0)
        def _():
            ys_ref[...] += y

    @pl.when((c == 0) & (r >= n_used))
    def _():
        ys_ref[...] = jnp.zeros(ys_ref.shape, F32)


def _moe_experts(h, src_tiles, tile_expert, n_used, wgu_bf, wd_bf, tm):
    n_tiles = src_tiles.shape[0]
    d = h.shape[1]
    f = wd_bf.shape[1]
    fc = max(c for c in range(LANES, EXPERT_CHUNK_CAP + 1, LANES) if f % c == 0)
    n_fc = f // fc
    chunk = lambda r, c, nu: jnp.where(r < nu[0], c, 0)
    grid_spec = pltpu.PrefetchScalarGridSpec(
        num_scalar_prefetch=2,
        grid=(n_tiles, n_fc),
        in_specs=[pl.BlockSpec((1, 1, tm), lambda r, c, te, nu: (r, 0, 0), memory_space=pltpu.SMEM),
                  pl.BlockSpec((1, 1, tm), lambda r, c, te, nu: (jnp.minimum(r + 1, n_tiles - 1), 0, 0),
                               memory_space=pltpu.SMEM),
                  pl.BlockSpec(memory_space=pl.ANY),
                  pl.BlockSpec((1, d, fc), lambda r, c, te, nu: (te[r], 0, chunk(r, c, nu))),
                  pl.BlockSpec((1, d, fc), lambda r, c, te, nu: (te[r], 0, n_fc + chunk(r, c, nu))),
                  pl.BlockSpec((1, fc, d), lambda r, c, te, nu: (te[r], chunk(r, c, nu), 0))],
        out_specs=pl.BlockSpec((tm, d), lambda r, c, te, nu: (r, 0)),
        scratch_shapes=[pltpu.VMEM((2, tm, d), F32), pltpu.VMEM((tm, d), BF16),
                        pltpu.SemaphoreType.DMA((2,))])
    return pl.pallas_call(
        functools.partial(_experts_kernel, tm=tm),
        grid_spec=grid_spec,
        out_shape=jax.ShapeDtypeStruct((n_tiles * tm, d), F32),
        compiler_params=_params("arbitrary", "arbitrary"),
        name="moe_experts",
    )(tile_expert, n_used, src_tiles, src_tiles, h, wgu_bf, wgu_bf, wd_bf)


def _combine_kernel(idx1_ref, idx2_ref, x3_ref, route_ref, ys_hbm, gfin_ref, out_ref, ybuf, sems, *, tm):
    _gather_rows_start(idx1_ref, ys_hbm, ybuf.at[0], sems.at[0], tm)
    _gather_rows_start(idx2_ref, ys_hbm, ybuf.at[1], sems.at[1], tm)
    route = route_ref[...]
    lane = lax.broadcasted_iota(jnp.int32, route.shape, 1)
    gate = lambda n: jnp.sum(jnp.where(lane == n, route, 0.0), axis=-1, keepdims=True)
    g1, g2 = gate(2), gate(3)
    _gather_rows_wait(ys_hbm, ybuf.at[0], sems.at[0], tm)
    _gather_rows_wait(ys_hbm, ybuf.at[1], sems.at[1], tm)
    out_ref[...] = _rms(x3_ref[...] + (g1 * ybuf[0] + g2 * ybuf[1]), gfin_ref[...])


def _moe_combine(x3, route, ys, slot1, slot2, g_final, tm):
    n, d = x3.shape
    idx = pl.BlockSpec((1, 1, tm), lambda t: (t, 0, 0), memory_space=pltpu.SMEM)
    row = lambda w: pl.BlockSpec((tm, w), lambda t: (t, 0))
    return pl.pallas_call(
        functools.partial(_combine_kernel, tm=tm),
        grid=(n // tm,),
        in_specs=[idx, idx, row(d), row(LANES), pl.BlockSpec(memory_space=pl.ANY), _const_spec((1, d))],
        out_specs=row(d),
        out_shape=jax.ShapeDtypeStruct((n, d), F32),
        scratch_shapes=[pltpu.VMEM((2, tm, d), F32), pltpu.SemaphoreType.DMA((2,))],
        compiler_params=_params("arbitrary"),
        name="moe_combine",
    )(slot1, slot2, x3, route, ys, g_final)


def _routing_tables(route, n_exp, tm):
    n = route.shape[0]
    experts = route[:, :2].astype(jnp.int32).reshape(-1)
    onehot = (experts[:, None] == jnp.arange(n_exp, dtype=jnp.int32)[None, :]).astype(jnp.int32)
    csum = jnp.cumsum(onehot, axis=0)
    rank = jnp.sum(csum * onehot, axis=1) - 1
    counts = csum[-1]
    padded = ((counts + tm - 1) // tm) * tm
    ends = jnp.cumsum(padded)
    starts = ends - padded
    slots = jnp.sum(starts[None, :] * onehot, axis=1) + rank
    n_tiles = (2 * n) // tm + n_exp
    src = jnp.zeros((n_tiles * tm,), jnp.int32).at[slots].set(jnp.arange(2 * n, dtype=jnp.int32) // 2)
    tile_start = jnp.arange(n_tiles, dtype=jnp.int32) * tm
    tile_expert = jnp.minimum(jnp.sum((tile_start[:, None] >= ends[None, :]).astype(jnp.int32), axis=1),
                              n_exp - 1)
    n_used = (ends[-1] // tm).reshape(1)
    slots = slots.reshape(n, 2)
    return slots[:, 0], slots[:, 1], src, tile_expert, n_used


def _out_moe(x, o, wo_bf, g, router_pad, n_exp, wgu_bf, wd_bf, g_final):
    n, d = x.shape
    tm = _row_tile(n, 512)
    x3, h, route = _moe_route(x, o, wo_bf, g, router_pad, n_exp)
    slot1, slot2, src, tile_expert, n_used = _routing_tables(route, n_exp, tm)
    ys = _moe_experts(h, src.reshape(-1, 1, tm), tile_expert, n_used, wgu_bf, wd_bf, tm)
    return _moe_combine(x3, route, ys, slot1.reshape(-1, 1, tm), slot2.reshape(-1, 1, tm), g_final, tm)


def _lane_block(x, start):
    return jnp.pad(x, ((0, 0), (start, LANES - start - x.shape[1])))


def _swap_halves(x):
    half = x.shape[-1] // 2
    return jnp.concatenate([x[..., half:], x[..., :half]], axis=-1)


def _prep_mla_weights(w_a, g_q, g_kv, w_uq, w_ukv, q_lora, kv_lora, n_heads, nope, rope):
    w_pe = w_a[:, q_lora + kv_lora:]
    w_pe_sw = _swap_halves(w_pe)
    wpe = jnp.concatenate([_lane_block(w_pe, 0), _lane_block(w_pe_sw, 0),
                           _lane_block(w_pe, nope), _lane_block(w_pe_sw, nope)], axis=1)
    uq = w_uq.reshape(q_lora, n_heads, nope + rope)
    uq_nope, uq_pe = uq[..., :nope], uq[..., nope:]
    tail = jnp.zeros((q_lora, n_heads, LANES - nope - rope), w_uq.dtype)
    wuq = jnp.concatenate([uq_nope, uq_pe, tail], axis=-1).reshape(q_lora, n_heads * LANES)
    wuqs = jnp.concatenate([jnp.zeros_like(uq_nope), _swap_halves(uq_pe), tail],
                           axis=-1).reshape(q_lora, n_heads * LANES)
    ukv = w_ukv.reshape(kv_lora, n_heads, -1)
    uk = ukv[..., :nope]
    wuk = jnp.concatenate([uk, jnp.zeros((kv_lora, n_heads, LANES - nope), uk.dtype)],
                          axis=-1).reshape(kv_lora, n_heads * LANES)
    wuv = ukv[..., nope:].reshape(kv_lora, -1)
    bf = lambda a: a.astype(BF16)
    return {"waq": bf(w_a[:, :q_lora]), "wakv": bf(w_a[:, q_lora:q_lora + kv_lora]), "wpe": bf(wpe),
            "gq": g_q[None, :], "gkv": g_kv[None, :], "wuq": bf(wuq), "wuqs": bf(wuqs),
            "wuk": bf(wuk), "wuv": bf(wuv)}


def _rope_tables(pos, nope, rope):
    half = rope // 2
    freqs = jnp.power(ROPE_THETA, -jnp.arange(half, dtype=F32) * 2.0 / rope)
    ang = pos.astype(F32)[:, None] * freqs[None, :]
    cos = jnp.concatenate([jnp.cos(ang)] * 2, axis=1)
    sin = jnp.concatenate([-jnp.sin(ang), jnp.sin(ang)], axis=1)
    c64 = _lane_block(cos, nope) + _lane_block(jnp.ones((pos.shape[0], nope), F32), 0)
    return _lane_block(cos, 0), _lane_block(sin, 0), c64, _lane_block(sin, nope)


def _bf16_split3(x):
    hi = x.astype(BF16).astype(F32)
    mid = (x - hi).astype(BF16).astype(F32)
    lo = (x - hi - mid).astype(BF16).astype(F32)
    return hi, mid, lo


def kernel(x_prompt, x_sample, cache_diff_k, cache_diff_v, cache_mla_ckv, cache_mla_kpe, norm_mix, norm_ffn, norm_final, diff_w_qkv, diff_lambda, diff_subln, diff_w_o, mla_w_a, mla_norm_q, mla_norm_kv, mla_w_uq, mla_w_ukv, mla_w_o, ffn_w_gu, ffn_w_down, moe_router, moe_w_gu, moe_w_down):
    b, t, d = x_prompt.shape
    db, nq, _ = x_sample.shape
    past = cache_diff_k.shape[2]
    n_da_heads, dv = cache_diff_k.shape[3], diff_subln.shape[-1]
    da_width = n_da_heads * dv
    q_lora, kv_lora = mla_norm_q.shape[-1], mla_norm_kv.shape[-1]
    rope = mla_w_a.shape[-1] - q_lora - kv_lora
    mla_width = mla_w_o.shape[1]
    n_exp = moe_router.shape[-1]
    assert norm_mix.shape[0] == 2, "one differential-attention layer followed by one MLA layer"

    uq_cols, ukv_cols = mla_w_uq.shape[-1], mla_w_ukv.shape[-1]
    n_mla_heads = (uq_cols - (ukv_cols - mla_width)) // rope
    nope = uq_cols // n_mla_heads - rope
    mla_v = mla_width // n_mla_heads
    assert n_mla_heads * (nope + mla_v) == ukv_cols and n_mla_heads % 2 == 0
    assert nope + rope <= LANES and 2 * mla_v == LANES

    bf = lambda a: a.astype(BF16)
    lam_init = 0.8 - 0.6 * math.exp(-0.3 * 0)
    da_scale = (dv // 2) ** -0.5 * LOG2E
    mla_scale = (nope + rope) ** -0.5 * LOG2E
    slopes2 = jnp.exp2(-8.0 * jnp.arange(1, n_da_heads + 1, dtype=F32) / n_da_heads) * LOG2E
    slope_parts = jnp.stack(_bf16_split3(slopes2) + (slopes2,), axis=1).reshape(-1)
    slope_rows = jnp.repeat(slopes2, 2 * nq)[:, None]

    w_qkv, w_o0 = bf(diff_w_qkv[0]), bf(diff_w_o[0])
    w_gu0, w_d0 = bf(ffn_w_gu[0]), bf(ffn_w_down[0])
    mla_w = _prep_mla_weights(mla_w_a[0], mla_norm_q[0], mla_norm_kv[0], mla_w_uq[0], mla_w_ukv[0],
                              q_lora, kv_lora, n_mla_heads, nope, rope)
    w_o1 = bf(mla_w_o[0])
    router_pad = jnp.pad(moe_router[0], ((0, 0), (0, LANES - n_exp)))
    moe_gu, moe_d = bf(moe_w_gu.reshape(moe_w_gu.shape[1:])), bf(moe_w_down.reshape(moe_w_down.shape[1:]))
    g_mix, g_ffn, g_fin = norm_mix[:, None, :], norm_ffn[:, None, :], norm_final[None, :]
    lam, subln = diff_lambda[0], diff_subln[0][None, :]

    def trunk(x, pos, attn0, attn1):
        bx, tx, _ = x.shape
        x2d = x.reshape(bx * tx, d)
        q, k, v, kx, vtx = _qkv_proj(x, g_mix[0], w_qkv, dv, da_scale)
        o = attn0(q, k, v, kx, vtx)
        x2 = _out_ffn(x2d, o.reshape(bx * tx, da_width), w_o0, g_ffn[0], w_gu0, w_d0)
        ckv, kpe, pe64, qcat, kcat, vtx1 = _mla_proj(x2.reshape(bx, tx, d), g_mix[1], mla_w,
                                                    _rope_tables(pos, nope, rope), rope, mla_v,
                                                    mla_scale)
        o = attn1(ckv, pe64, qcat, kcat, vtx1)
        y = _out_moe(x2, o.reshape(bx * tx, mla_width), w_o1, g_ffn[1], router_pad, n_exp,
                     moe_gu, moe_d, g_fin)
        return y, k, v, ckv, kpe

    y_p, k_p, v_p, ckv_p, kpe_p = trunk(
        x_prompt, jnp.arange(t, dtype=jnp.int32),
        lambda q, k, v, kx, vtx: _diff_attn_prompt(q, kx, vtx, lam, subln, slope_parts, lam_init),
        lambda ckv, pe64, qcat, kcat, vtx1: _mla_attn_prompt(qcat, kcat, vtx1, mla_v))

    n_s = db * nq
    cache_k = cache_diff_k.reshape(cache_diff_k.shape[1:])
    cache_v = cache_diff_v.reshape(cache_diff_v.shape[1:])
    cache_pe = jnp.pad(cache_mla_kpe[0], ((0, 0), (0, 0), (nope, LANES - nope - rope)))
    pos_s = past + jnp.tile(jnp.arange(nq, dtype=jnp.int32), db)

    def attn0_s(q, k, v, kx, vtx):
        r = lambda a: a.reshape(db, nq, da_width)
        return _diff_attn_decode(r(q), cache_k, cache_v, r(k), r(v), lam, subln, slope_rows, lam_init)

    def attn1_s(ckv, pe64, qcat, kcat, vtx1):
        return _mla_attn_decode(qcat.reshape(db, nq, -1), cache_mla_ckv[0], cache_pe,
                                ckv.reshape(db, nq, kv_lora), pe64.reshape(db, nq, LANES),
                                mla_w["wuk"], mla_w["wuv"])

    y_s, k_s, v_s, ckv_s, kpe_s = trunk(x_sample.reshape(1, n_s, d), pos_s, attn0_s, attn1_s)

    return (y_p.reshape(b, t, d), y_s.reshape(db, nq, d),
            k_p.reshape(1, b, t, n_da_heads, dv), v_p.reshape(1, b, t, n_da_heads, dv),
            ckv_p.reshape(1, b, t, kv_lora), kpe_p.reshape(1, b, t, rope),
            k_s.reshape(1, db, nq, n_da_heads, dv), v_s.reshape(1, db, nq, n_da_heads, dv),
            ckv_s.reshape(1, db, nq, kv_lora), kpe_s.reshape(1, db, nq, rope))
```

```python
import functools
import math

import jax
import jax.numpy as jnp
from jax import lax
from jax.experimental import pallas as pl
from jax.experimental.pallas import tpu as pltpu

NORM_EPS = 1e-6
NEG_INF = -1e30
CHUNK = 64
ROPE_THETA = 10000.0
LOG2E = math.log2(math.e)
LANES = 128
MXU_DIM = 256
BF16_ROWS = 16
EXPERT_CHUNK_CAP = 2048
FLASH_Q_BLOCK = 1024
VMEM_LIMIT_BYTES = 56 * 2**20
BF16 = jnp.bfloat16
F32 = jnp.float32


def _params(*sem):
    return pltpu.CompilerParams(dimension_semantics=sem, vmem_limit_bytes=VMEM_LIMIT_BYTES)


def _const_spec(shape):
    nd = len(shape)
    return pl.BlockSpec(shape, lambda *_: (0,) * nd, pipeline_mode=pl.Buffered(1))


def _rms(x, g):
    ms = jnp.mean(x * x, axis=-1, keepdims=True)
    return x * lax.rsqrt(ms + NORM_EPS) * g


def _dot(a, b):
    return jnp.dot(a, b, preferred_element_type=F32)


def _dot_nt(a, b):
    return lax.dot_general(a, b, (((1,), (1,)), ((), ())), preferred_element_type=F32)


def _row_tile(t, cap):
    tm = min(t, cap)
    assert t % tm == 0
    return tm


def _ones_row_block(n_cols):
    sub = lax.broadcasted_iota(jnp.int32, (BF16_ROWS, n_cols), 0)
    return jnp.where(sub == 0, 1.0, 0.0).astype(BF16)


def _qkv_kernel(x_ref, g_ref, w_ref, q_ref, k_ref, v_ref, kx_ref, vtx_ref, *, width, dv, scale):
    tm = x_ref.shape[1]
    h = _rms(x_ref[0], g_ref[...]).astype(BF16)
    qkv = _dot(h, w_ref[...])
    q_ref[0] = (qkv[:, :width] * scale).astype(BF16)
    k = qkv[:, width:2 * width]
    v = qkv[:, 2 * width:]
    k_ref[0] = k
    v_ref[0] = v
    k_bf = k.astype(BF16)
    vt = v.T.astype(BF16)
    pos = lax.broadcasted_iota(jnp.int32, (tm, LANES), 0) + pl.program_id(1) * tm
    lane = lax.broadcasted_iota(jnp.int32, (tm, LANES), 1)
    pos_feat = jnp.where(lane < 3, pos // LANES, jnp.where(lane < 6, pos % LANES, 0))
    pos_feat = pos_feat.astype(F32).astype(BF16)
    ones = _ones_row_block(tm)
    for hh in range(width // dv):
        kx_ref[0, :, 2 * dv * hh:2 * dv * hh + dv] = k_bf[:, hh * dv:(hh + 1) * dv]
        kx_ref[0, :, 2 * dv * hh + dv:2 * dv * (hh + 1)] = pos_feat
        r0 = hh * (dv + BF16_ROWS)
        vtx_ref[0, r0:r0 + dv, :] = vt[hh * dv:(hh + 1) * dv, :]
        vtx_ref[0, r0 + dv:r0 + dv + BF16_ROWS, :] = ones


def _qkv_proj(x, g, w_bf, dv, scale):
    bx, tx, d = x.shape
    width = w_bf.shape[1] // 3
    n_heads = width // dv
    assert dv == LANES and tx <= LANES * MXU_DIM
    tm = _row_tile(tx, 512)
    row = lambda n: pl.BlockSpec((1, tm, n), lambda b, t: (b, t, 0))
    vrows = n_heads * (dv + BF16_ROWS)
    return pl.pallas_call(
        functools.partial(_qkv_kernel, width=width, dv=dv, scale=scale),
        grid=(bx, tx // tm),
        in_specs=[row(d), _const_spec((1, d)), _const_spec(w_bf.shape)],
        out_specs=[row(width), row(width), row(width), row(2 * width),
                   pl.BlockSpec((1, vrows, tm), lambda b, t: (b, 0, t))],
        out_shape=[jax.ShapeDtypeStruct((bx, tx, width), BF16),
                   jax.ShapeDtypeStruct((bx, tx, width), F32),
                   jax.ShapeDtypeStruct((bx, tx, width), F32),
                   jax.ShapeDtypeStruct((bx, tx, 2 * width), BF16),
                   jax.ShapeDtypeStruct((bx, vrows, tx), BF16)],
        compiler_params=_params("parallel", "parallel"),
        name="qkv_proj",
    )(x, g, w_bf)


def _flash_streams(i, tq, key_op, query_op, value_op, diag_tile, z_a, z_b, cm_a, cm_b, m, acc,
                   consume_first):
    n_streams = 2
    tk = tq // 2

    units = [(s, slice(n * MXU_DIM, (n + 1) * MXU_DIM))
             for s in range(n_streams) for n in range(tq // MXU_DIM)]

    def produce(blk, z, cm, extra, s, cols):
        off = pl.multiple_of(blk * tk, tk)
        scores = _dot_nt(key_op(s, off), query_op(s, cols))
        if extra is not None:
            scores = scores + extra[:, cols]
        z[s, :, cols] = scores
        cm[s, :, cols] = jnp.max(scores, axis=0, keepdims=True)

    def consume(blk, z, cm, s, cols):
        off = pl.multiple_of(blk * tk, tk)
        m_old = m[s, :, cols]
        m_new = jnp.maximum(m_old, cm[s, :, cols])
        p = jnp.exp2(z[s, :, cols] - m_new).astype(BF16)
        acc[s, :, cols] = jnp.exp2(m_old - m_new) * acc[s, :, cols] + _dot(value_op(s, off), p)
        m[s, :, cols] = m_new

    def step(p_blk, p_z, p_cm, extra, c_blk, c_z, c_cm):
        for s, cols in units:
            if consume_first:
                consume(c_blk, c_z, c_cm, s, cols)
            produce(p_blk, p_z, p_cm, extra, s, cols)
            if not consume_first:
                consume(c_blk, c_z, c_cm, s, cols)

    m[...] = jnp.full(m.shape, NEG_INF, F32)
    acc[...] = jnp.zeros(acc.shape, F32)
    first_tile = diag_tile(0)
    for s, cols in units:
        produce(2 * i, z_a, cm_a, first_tile, s, cols)

    def pair(u):
        step(2 * u, z_b, cm_b, None, jnp.where(u == 0, 2 * i, 2 * u - 1), z_a, cm_a)
        step(2 * u + 1, z_a, cm_a, None, 2 * u, z_b, cm_b)

    odd = i % 2

    @pl.when(odd == 1)
    def _():
        pair(0)

    def body(v, carry):
        pair(odd + 2 * v)
        pair(odd + 2 * v + 1)
        return carry

    lax.fori_loop(0, i // 2, body, 0)

    last_tile = diag_tile(1)
    last_a = jnp.where(i == 0, 0, 2 * i - 1)
    late_units = [(s, cols) for s, cols in units if cols.stop > tk]
    for s, cols in units:
        if cols.stop > tk:
            produce(2 * i + 1, z_b, cm_b, last_tile, s, cols)
        consume(last_a, z_a, cm_a, s, cols)
    for s, cols in late_units:
        consume(2 * i + 1, z_b, cm_b, s, cols)


def _local_positions(tq, n):
    tk = tq // 2
    kl = lax.broadcasted_iota(jnp.int32, (tk, tq), 0) + n * tk
    ql = lax.broadcasted_iota(jnp.int32, (tk, tq), 1)
    return kl, ql, (kl // CHUNK) <= (ql // CHUNK)


def _lambda_full(lam_ref, lam_init):
    lf = lam_ref[...]
    s01 = jnp.sum(lf[0:1, :] * lf[1:2, :], axis=-1, keepdims=True)
    s23 = jnp.sum(lf[2:3, :] * lf[3:4, :], axis=-1, keepdims=True)
    return jnp.exp(s01) - jnp.exp(s23) + lam_init


def _diff_flash_kernel(slope_ref, q_ref, kx_ref, vtx_ref, lam_ref, subln_ref, o_ref,
                       qx, z_a, z_b, cm_a, cm_b, m, acc, *, tq, dv, lam_init):
    h = pl.program_id(1)
    i = pl.program_id(2)
    s_hi, s_mid, s_lo, slope2 = (slope_ref[4 * h + n] for n in range(4))
    q = q_ref[0]
    lane = lax.broadcasted_iota(jnp.int32, q.shape, 1)
    zero = jnp.zeros_like(q)
    slope_feat = jnp.zeros(q.shape, F32)
    for n, val in enumerate((LANES * s_hi, LANES * s_mid, LANES * s_lo, s_hi, s_mid, s_lo)):
        slope_feat = jnp.where(lane == n, val, slope_feat)
    slope_feat = slope_feat.astype(BF16)
    qx[0, :, 0:dv] = jnp.where(lane < dv // 2, q, zero)
    qx[1, :, 0:dv] = jnp.where(lane >= dv // 2, q, zero)
    qx[0, :, dv:2 * dv] = slope_feat
    qx[1, :, dv:2 * dv] = slope_feat

    def diag_tile(n):
        kl, ql, visible = _local_positions(tq, n)
        return jnp.where(visible, (-2.0 * slope2) * jnp.maximum(kl - ql, 0).astype(F32), NEG_INF)

    tk = tq // 2
    _flash_streams(i, tq,
                   key_op=lambda s, off: kx_ref[0, pl.ds(off, tk), :],
                   query_op=lambda s, cols: qx[s, cols, :],
                   value_op=lambda s, off: vtx_ref[0, :, pl.ds(off, tk)],
                   diag_tile=diag_tile, z_a=z_a, z_b=z_b, cm_a=cm_a, cm_b=cm_b, m=m, acc=acc,
                   consume_first=True)

    lam = _lambda_full(lam_ref, lam_init)
    o_t = (acc[0, 0:dv, :] / acc[0, dv:dv + 1, :]
           - lam * (acc[1, 0:dv, :] / acc[1, dv:dv + 1, :]))
    o = _rms(o_t.T, subln_ref[...]) * (1.0 - lam_init)
    o_ref[0] = o.astype(BF16)


def _flash_scratch(tq, q_feat, acc_rows):
    return ([pltpu.VMEM((2, tq, q_feat), BF16)] if q_feat else []) + [
        pltpu.VMEM((2, tq // 2, tq), F32), pltpu.VMEM((2, tq // 2, tq), F32),
        pltpu.VMEM((2, 1, tq), F32), pltpu.VMEM((2, 1, tq), F32),
        pltpu.VMEM((2, 1, tq), F32), pltpu.VMEM((2, acc_rows, tq), F32)]


def _diff_attn_prompt(q, kx, vtx, lam, subln, slope_parts, lam_init):
    b, t, width = q.shape
    dv = subln.shape[-1]
    n_heads = width // dv
    tq = _row_tile(t, FLASH_Q_BLOCK)
    assert tq % (2 * MXU_DIM) == 0
    vrows = dv + BF16_ROWS
    return pl.pallas_call(
        functools.partial(_diff_flash_kernel, tq=tq, dv=dv, lam_init=lam_init),
        grid=(b, n_heads, t // tq),
        in_specs=[pl.BlockSpec(memory_space=pltpu.SMEM),
                  pl.BlockSpec((1, tq, dv), lambda b_, h, i: (b_, i, h)),
                  pl.BlockSpec((1, t, 2 * dv), lambda b_, h, i: (b_, 0, h)),
                  pl.BlockSpec((1, vrows, t), lambda b_, h, i: (b_, h, 0)),
                  pl.BlockSpec(lam.shape, lambda b_, h, i: (0, 0)),
                  pl.BlockSpec(subln.shape, lambda b_, h, i: (0, 0))],
        out_specs=pl.BlockSpec((1, tq, dv), lambda b_, h, i: (b_, i, h)),
        out_shape=jax.ShapeDtypeStruct((b, t, width), BF16),
        scratch_shapes=_flash_scratch(tq, 2 * dv, vrows),
        compiler_params=_params("parallel", "parallel", "arbitrary"),
        name="diff_attn_prompt",
    )(slope_parts, q, kx, vtx, lam, subln)


def _mla_flash_kernel(q_ref, k_ref, vtx_ref, o_ref, z_a, z_b, cm_a, cm_b, m, acc, *, tq, dv):
    i = pl.program_id(2)
    vrows = dv + BF16_ROWS
    tk = tq // 2
    _flash_streams(i, tq,
                   key_op=lambda s, off: k_ref[0, pl.ds(off, tk), s * LANES:(s + 1) * LANES],
                   query_op=lambda s, cols: q_ref[0, cols, s * LANES:(s + 1) * LANES],
                   value_op=lambda s, off: vtx_ref[0, s * vrows:(s + 1) * vrows, pl.ds(off, tk)],
                   diag_tile=lambda n: jnp.where(_local_positions(tq, n)[2], 0.0, NEG_INF),
                   z_a=z_a, z_b=z_b, cm_a=cm_a, cm_b=cm_b, m=m, acc=acc, consume_first=False)
    o_t = jnp.concatenate([acc[s, 0:dv, :] / acc[s, dv:dv + 1, :] for s in range(2)], axis=0)
    o_ref[0] = o_t.T.astype(BF16)


def _mla_attn_prompt(qcat, kcat, vtx, dv):
    b, t, wq = qcat.shape
    n_pairs = wq // (2 * LANES)
    tq = _row_tile(t, FLASH_Q_BLOCK)
    assert tq % (2 * MXU_DIM) == 0
    vrows = dv + BF16_ROWS
    return pl.pallas_call(
        functools.partial(_mla_flash_kernel, tq=tq, dv=dv),
        grid=(b, n_pairs, t // tq),
        in_specs=[pl.BlockSpec((1, tq, 2 * LANES), lambda b_, h, i: (b_, i, h)),
                  pl.BlockSpec((1, t, 2 * LANES), lambda b_, h, i: (b_, 0, h)),
                  pl.BlockSpec((1, 2 * vrows, t), lambda b_, h, i: (b_, h, 0))],
        out_specs=pl.BlockSpec((1, tq, 2 * dv), lambda b_, h, i: (b_, i, h)),
        out_shape=jax.ShapeDtypeStruct((b, t, n_pairs * 2 * dv), BF16),
        scratch_shapes=_flash_scratch(tq, 0, vrows),
        compiler_params=_params("parallel", "parallel", "arbitrary"),
        name="mla_attn_prompt",
    )(qcat, kcat, vtx)


def _block_diag_queries(q, n_groups, group_width):
    qt = jnp.concatenate([q] * n_groups, axis=0)
    r = lax.broadcasted_iota(jnp.int32, qt.shape, 0)
    c = lax.broadcasted_iota(jnp.int32, qt.shape, 1)
    return jnp.where((c // group_width) == (r // q.shape[0]), qt, jnp.zeros_like(qt))


def _decode_update(z, v_bf, m_ref, l_ref, acc_ref):
    m_old = m_ref[...]
    m_new = jnp.maximum(m_old, jnp.max(z, axis=-1, keepdims=True))
    alpha = jnp.exp2(m_old - m_new)
    p = jnp.exp2(z - m_new)
    l_ref[...] = alpha * l_ref[...] + jnp.sum(p, axis=-1, keepdims=True)
    acc_ref[...] = alpha * acc_ref[...] + _dot(p.astype(BF16), v_bf)
    m_ref[...] = m_new


def _decode_positions(n_rows, n_keys, nq, past, key_start, n_valid):
    r = lax.broadcasted_iota(jnp.int32, (n_rows, n_keys), 0)
    c = lax.broadcasted_iota(jnp.int32, (n_rows, n_keys), 1)
    q_pos = past + r % nq
    k_pos = key_start + c
    visible = ((k_pos // CHUNK) <= (q_pos // CHUNK)) & (c < n_valid)
    return q_pos, k_pos, visible


def _pad_rows(x, n):
    return jnp.concatenate([x, jnp.zeros((n - x.shape[0], x.shape[1]), x.dtype)], axis=0)


def _diff_decode_kernel(slope_ref, q_ref, ck_ref, cv_ref, kn_ref, vn_ref, lam_ref, subln_ref,
                        o_ref, qrows, qbd, bias, m, l, acc, *, past, tk, nq, dv, lam_init):
    j = pl.program_id(1)
    n_rows = qrows.shape[0]
    n_heads = n_rows // (2 * nq)
    n_cols = tk * n_heads

    @pl.when(j == 0)
    def _():
        q = q_ref[0]
        qbd[...] = _block_diag_queries(q, 2 * n_heads, dv // 2)
        lane = lax.broadcasted_iota(jnp.int32, (nq, dv), 1)
        for hh in range(n_heads):
            q_h = q[:, hh * dv:(hh + 1) * dv]
            r0 = hh * 2 * nq
            qrows[r0:r0 + nq, :] = jnp.where(lane < dv // 2, q_h, jnp.zeros_like(q_h))
            qrows[r0 + nq:r0 + 2 * nq, :] = jnp.where(lane >= dv // 2, q_h, jnp.zeros_like(q_h))
        r = lax.broadcasted_iota(jnp.int32, (n_rows, n_cols), 0)
        c = lax.broadcasted_iota(jnp.int32, (n_rows, n_cols), 1)
        own_head = (c % n_heads) == (r // (2 * nq))
        rel = (c // n_heads - r % nq).astype(F32)
        bias[...] = jnp.where(own_head, slope_ref[...] * rel, NEG_INF)
        m[...] = jnp.full(m.shape, NEG_INF, F32)
        l[...] = jnp.zeros(l.shape, F32)
        acc[...] = jnp.zeros(acc.shape, F32)

    def update(z, pv):
        m_old = m[...]
        m_new = jnp.maximum(m_old, jnp.max(z, axis=-1, keepdims=True))
        alpha = jnp.exp2(m_old - m_new)
        p = jnp.exp2(z - m_new)
        l[...] = alpha * l[...] + jnp.sum(p, axis=-1, keepdims=True)
        acc[...] = alpha * acc[...] + pv(p.astype(BF16))
        m[...] = m_new

    k2 = ck_ref[0].reshape(n_cols, dv).astype(BF16)
    v2 = cv_ref[0].reshape(n_cols, dv).astype(BF16)
    shift = slope_ref[...] * (past - j * tk).astype(F32)
    update(_dot_nt(qrows[...], k2) + bias[...] - shift, lambda p: _dot(p, v2))

    @pl.when(j == pl.num_programs(1) - 1)
    def _():
        k_new = _pad_rows(kn_ref[0], LANES).astype(BF16)
        v_new = _pad_rows(vn_ref[0], LANES).astype(BF16)
        q_pos, k_pos, visible = _decode_positions(n_rows, LANES, nq, past, past, nq)
        dist = jnp.abs(q_pos - k_pos).astype(F32)
        z = jnp.where(visible, _dot_nt(qbd[...], k_new) - slope_ref[...] * dist, NEG_INF)

        def pv_new(p):
            full = _dot(p, v_new)
            return jnp.concatenate([full[hh * 2 * nq:(hh + 1) * 2 * nq, hh * dv:(hh + 1) * dv]
                                    for hh in range(n_heads)], axis=0)

        update(z, pv_new)
        lam = _lambda_full(lam_ref, lam_init)
        for hh in range(n_heads):
            r0 = hh * 2 * nq
            o1 = acc[r0:r0 + nq, :] / l[r0:r0 + nq, :]
            o2 = acc[r0 + nq:r0 + 2 * nq, :] / l[r0 + nq:r0 + 2 * nq, :]
            o = _rms(o1 - lam * o2, subln_ref[...]) * (1.0 - lam_init)
            o_ref[0, :, hh * dv:(hh + 1) * dv] = o.astype(BF16)


def _diff_attn_decode(q, cache_k, cache_v, k_new, v_new, lam, subln, slope_rows, lam_init):
    db, nq, width = q.shape
    past, n_heads, dv = cache_k.shape[1:]
    n_rows = n_heads * 2 * nq
    tk = _row_tile(past, 512)
    stream = lambda n: pl.BlockSpec((1, nq, n), lambda b, j: (b, 0, 0))
    cache = pl.BlockSpec((1, tk, n_heads, dv), lambda b, j: (b, j, 0, 0))
    return pl.pallas_call(
        functools.partial(_diff_decode_kernel, past=past, tk=tk, nq=nq, dv=dv, lam_init=lam_init),
        grid=(db, past // tk),
        in_specs=[_const_spec(slope_rows.shape), stream(width), cache, cache, stream(width),
                  stream(width), _const_spec(lam.shape), _const_spec(subln.shape)],
        out_specs=stream(width),
        out_shape=jax.ShapeDtypeStruct((db, nq, width), BF16),
        scratch_shapes=[pltpu.VMEM((n_rows, dv), BF16), pltpu.VMEM((n_rows, width), BF16),
                        pltpu.VMEM((n_rows, tk * n_heads), F32), pltpu.VMEM((n_rows, 1), F32),
                        pltpu.VMEM((n_rows, 1), F32), pltpu.VMEM((n_rows, dv), F32)],
        compiler_params=_params("parallel", "arbitrary"),
        name="diff_attn_decode",
    )(slope_rows, q, cache_k, cache_v, k_new, v_new, lam, subln)


def _mla_decode_kernel(q_ref, cc_ref, cpe_ref, cn_ref, pen_ref, wuk_ref, wuv_ref, o_ref,
                       q_lat, q_pe, m, l, acc, *, past, tk, nq, dv, nope, rope):
    j = pl.program_id(1)
    n_rows = q_lat.shape[0]
    n_heads = n_rows // nq

    @pl.when(j == 0)
    def _():
        qbd = _block_diag_queries(q_ref[0], n_heads, LANES)
        q_lat[...] = _dot_nt(qbd, wuk_ref[...]).astype(BF16)
        q_h = qbd[:, 0:LANES].astype(F32)
        for hh in range(1, n_heads):
            q_h = q_h + qbd[:, hh * LANES:(hh + 1) * LANES].astype(F32)
        q_pe[...] = q_h[:, nope:nope + rope].astype(BF16)
        m[...] = jnp.full(m.shape, NEG_INF, F32)
        l[...] = jnp.zeros(l.shape, F32)
        acc[...] = jnp.zeros(acc.shape, F32)

    def step(c_f32, pe_f32, key_start, n_valid):
        n_keys = c_f32.shape[0]
        c_bf = c_f32.astype(BF16)
        s = _dot_nt(q_lat[...], c_bf) + _dot_nt(q_pe[...], pe_f32.astype(BF16))
        _, _, visible = _decode_positions(n_rows, n_keys, nq, past, key_start, n_valid)
        _decode_update(jnp.where(visible, s, NEG_INF), c_bf, m, l, acc)

    step(cc_ref[0], cpe_ref[0], j * tk, tk)

    @pl.when(j == pl.num_programs(1) - 1)
    def _():
        step(_pad_rows(cn_ref[0], LANES), _pad_rows(pen_ref[0][:, nope:nope + rope], LANES), past, nq)
        o_full = _dot((acc[...] / l[...]).astype(BF16), wuv_ref[...])
        lane = lax.broadcasted_iota(jnp.int32, (nq, 2 * dv), 1)
        for pair in range(n_heads // 2):
            r0 = pair * 2 * nq
            cols = slice(pair * 2 * dv, (pair + 1) * 2 * dv)
            o_ref[0, :, cols] = jnp.where(lane < dv, o_full[r0:r0 + nq, cols],
                                          o_full[r0 + nq:r0 + 2 * nq, cols]).astype(BF16)


def _mla_attn_decode(qcat, cache_c, cache_pe, c_new, pe_new, wuk_bf, wuv_bf, nope):
    db, nq, wq = qcat.shape
    past, lora = cache_c.shape[1], cache_c.shape[2]
    n_heads = wq // LANES
    wv = wuv_bf.shape[1]
    dv = wv // n_heads
    n_rows = n_heads * nq
    tk = _row_tile(past, 512)
    stream = lambda n: pl.BlockSpec((1, nq, n), lambda b, j: (b, 0, 0))
    return pl.pallas_call(
        functools.partial(_mla_decode_kernel, past=past, tk=tk, nq=nq, dv=dv, nope=nope,
                          rope=cache_pe.shape[2]),
        grid=(db, past // tk),
        in_specs=[stream(wq),
                  pl.BlockSpec((1, tk, lora), lambda b, j: (b, j, 0)),
                  pl.BlockSpec((1, tk, cache_pe.shape[2]), lambda b, j: (b, j, 0)),
                  stream(lora), stream(LANES),
                  _const_spec(wuk_bf.shape), _const_spec(wuv_bf.shape)],
        out_specs=stream(wv),
        out_shape=jax.ShapeDtypeStruct((db, nq, wv), BF16),
        scratch_shapes=[pltpu.VMEM((n_rows, lora), BF16), pltpu.VMEM((n_rows, cache_pe.shape[2]), BF16),
                        pltpu.VMEM((n_rows, 1), F32), pltpu.VMEM((n_rows, 1), F32),
                        pltpu.VMEM((n_rows, lora), F32)],
        compiler_params=_params("parallel", "arbitrary"),
        name="mla_attn_decode",
    )(qcat, cache_c, cache_pe, c_new, pe_new, wuk_bf, wuv_bf)


def _silu(x):
    return x * jax.nn.sigmoid(x)


def _out_ffn_kernel(x_ref, o_ref, wo_ref, g_ref, wgu_ref, wd_ref, out_ref, *, f, bounds):
    x1 = x_ref[...] + _dot(o_ref[...], wo_ref[...])
    h = _rms(x1, g_ref[...]).astype(BF16)
    acc = x1
    for c0, c1 in zip(bounds[:-1], bounds[1:]):
        gate = _dot(h, wgu_ref[:, c0:c1])
        up = _dot(h, wgu_ref[:, f + c0:f + c1])
        acc = acc + _dot((_silu(gate) * up).astype(BF16), wd_ref[c0:c1, :])
    out_ref[...] = acc


def _out_ffn(x, o, wo_bf, g, wgu_bf, wd_bf):
    n, d = x.shape
    f = wd_bf.shape[0]
    tm = _row_tile(n, 512)
    mid = (f // (2 * MXU_DIM) + (f // MXU_DIM) % 2) * MXU_DIM
    bounds = (0, mid, f) if 0 < mid < f else (0, f)
    row = lambda w: pl.BlockSpec((tm, w), lambda t: (t, 0))
    return pl.pallas_call(
        functools.partial(_out_ffn_kernel, f=f, bounds=bounds),
        grid=(n // tm,),
        in_specs=[row(d), row(o.shape[1]), _const_spec(wo_bf.shape), _const_spec((1, d)),
                  _const_spec(wgu_bf.shape), _const_spec(wd_bf.shape)],
        out_specs=row(d),
        out_shape=jax.ShapeDtypeStruct((n, d), F32),
        compiler_params=_params("parallel"),
        name="attn_out_ffn",
    )(x, o, wo_bf, g, wgu_bf, wd_bf)


def _mla_proj_kernel(x_ref, g_ref, waq_ref, wakv_ref, wpe_ref, gq_ref, gkv_ref, wuq_ref, wuqs_ref,
                     wuk_ref, wuv_ref, c0_ref, s0_ref, c64_ref, s64_ref,
                     ckv_ref, kpe_ref, pe64_ref, q_ref, k_ref, vtx_ref, *, rope, dv, scale):
    tm = x_ref.shape[1]
    h = _rms(x_ref[0], g_ref[...]).astype(BF16)
    c_q = _rms(_dot(h, waq_ref[...]), gq_ref[...]).astype(BF16)
    c_kv = _rms(_dot(h, wakv_ref[...]), gkv_ref[...])
    ckv_ref[0] = c_kv
    c_kv_bf = c_kv.astype(BF16)
    pe = _dot(h, wpe_ref[...])
    kpe0 = pe[:, 0:LANES] * c0_ref[...] + pe[:, LANES:2 * LANES] * s0_ref[...]
    kpe_ref[0] = kpe0[:, :rope]
    c64 = c64_ref[...]
    s64 = s64_ref[...]
    pe64 = pe[:, 2 * LANES:3 * LANES] * c64 + pe[:, 3 * LANES:] * s64
    pe64_ref[0] = pe64
    q = _dot(c_q, wuq_ref[...])
    q_sw = _dot(c_q, wuqs_ref[...])
    k = _dot(c_kv_bf, wuk_ref[...])
    for hh in range(q.shape[1] // LANES):
        cols = slice(hh * LANES, (hh + 1) * LANES)
        q_ref[0, :, cols] = ((q[:, cols] * c64 + q_sw[:, cols] * s64) * scale).astype(BF16)
        k_ref[0, :, cols] = (k[:, cols] + pe64).astype(BF16)
    vt = _dot(c_kv_bf, wuv_ref[...]).T.astype(BF16)
    ones = _ones_row_block(tm)
    for hh in range(vt.shape[0] // dv):
        r0 = hh * (dv + BF16_ROWS)
        vtx_ref[0, r0:r0 + dv, :] = vt[hh * dv:(hh + 1) * dv, :]
        vtx_ref[0, r0 + dv:r0 + dv + BF16_ROWS, :] = ones


def _mla_proj(x, g, w, tabs, rope, dv, scale):
    bx, tx, d = x.shape
    tm = _row_tile(tx, 256)
    lora = w["wakv"].shape[1]
    wq, wv = w["wuq"].shape[1], w["wuv"].shape[1]
    vrows = (wv // dv) * (dv + BF16_ROWS)
    row = lambda n: pl.BlockSpec((1, tm, n), lambda b, t: (b, t, 0))
    tab = pl.BlockSpec((tm, LANES), lambda b, t: (t, 0))
    weights = [w[k] for k in ("waq", "wakv", "wpe", "gq", "gkv", "wuq", "wuqs", "wuk", "wuv")]
    return pl.pallas_call(
        functools.partial(_mla_proj_kernel, rope=rope, dv=dv, scale=scale),
        grid=(bx, tx // tm),
        in_specs=[row(d), _const_spec((1, d))] + [_const_spec(a.shape) for a in weights] + [tab] * 4,
        out_specs=[row(lora), row(rope), row(LANES), row(wq), row(wq),
                   pl.BlockSpec((1, vrows, tm), lambda b, t: (b, 0, t))],
        out_shape=[jax.ShapeDtypeStruct((bx, tx, lora), F32),
                   jax.ShapeDtypeStruct((bx, tx, rope), F32),
                   jax.ShapeDtypeStruct((bx, tx, LANES), F32),
                   jax.ShapeDtypeStruct((bx, tx, wq), BF16),
                   jax.ShapeDtypeStruct((bx, tx, wq), BF16),
                   jax.ShapeDtypeStruct((bx, vrows, tx), BF16)],
        compiler_params=_params("parallel", "parallel"),
        name="mla_proj",
    )(x, g, *weights, *tabs)


def _route_kernel(x_ref, o_ref, wo_ref, g_ref, router_ref, x3_ref, h_ref, route_ref, *, n_exp):
    x3 = x_ref[...] + _dot(o_ref[...], wo_ref[...])
    h = _rms(x3, g_ref[...])
    lane = lax.broadcasted_iota(jnp.int32, route_ref.shape, 1).astype(F32)
    logits = jnp.dot(h, router_ref[...], preferred_element_type=F32,
                     precision=lax.Precision.HIGHEST)
    logits = jnp.where(lane < n_exp, logits, -jnp.inf)
    v1 = jnp.max(logits, axis=-1, keepdims=True)
    i1 = jnp.min(jnp.where(logits == v1, lane, float(LANES)), axis=-1, keepdims=True)
    rest = jnp.where(lane == i1, -jnp.inf, logits)
    v2 = jnp.max(rest, axis=-1, keepdims=True)
    i2 = jnp.min(jnp.where(rest == v2, lane, float(LANES)), axis=-1, keepdims=True)
    e2 = jnp.exp(v2 - v1)
    denom = 1.0 + e2
    route = jnp.zeros(route_ref.shape, F32)
    for n, val in enumerate((i1, i2, 1.0 / denom, e2 / denom)):
        route = jnp.where(lane == n, val, route)
    x3_ref[...] = x3
    h_ref[...] = h
    route_ref[...] = route


def _moe_route(x, o, wo_bf, g, router_pad, n_exp):
    n, d = x.shape
    tm = _row_tile(n, 512)
    row = lambda w: pl.BlockSpec((tm, w), lambda t: (t, 0))
    return pl.pallas_call(
        functools.partial(_route_kernel, n_exp=n_exp),
        grid=(n // tm,),
        in_specs=[row(d), row(o.shape[1]), _const_spec(wo_bf.shape), _const_spec((1, d)),
                  _const_spec(router_pad.shape)],
        out_specs=[row(d), row(d), row(LANES)],
        out_shape=[jax.ShapeDtypeStruct((n, d), F32), jax.ShapeDtypeStruct((n, d), F32),
                   jax.ShapeDtypeStruct((n, LANES), F32)],
        compiler_params=_params("parallel"),
        name="moe_route",
    )(x, o, wo_bf, g, router_pad)


def _gather_rows_start(idx_ref, src_hbm, dst, sem, n_rows):
    def body(t, carry):
        pltpu.make_async_copy(src_hbm.at[pl.ds(idx_ref[0, 0, t], 1)], dst.at[pl.ds(t, 1)], sem).start()
        return carry
    lax.fori_loop(0, n_rows, body, 0, unroll=8)


def _gather_rows_wait(src_hbm, dst, sem, n_rows):
    pltpu.make_async_copy(src_hbm.at[pl.ds(0, n_rows)], dst, sem).wait()


def _experts_kernel(te_ref, nu_ref, idx_ref, idx_next_ref, h_hbm, wg_ref, wu_ref, wd_ref, ys_ref,
                    hbuf, hb, sems, *, tm):
    r = pl.program_id(0)
    c = pl.program_id(1)
    n_used = nu_ref[0]
    slot = r % 2

    @pl.when((c == 0) & (r < n_used))
    def _():
        @pl.when(r == 0)
        def _():
            _gather_rows_start(idx_ref, h_hbm, hbuf.at[0], sems.at[0], tm)

        @pl.when(r + 1 < n_used)
        def _():
            _gather_rows_start(idx_next_ref, h_hbm, hbuf.at[1 - slot], sems.at[1 - slot], tm)

        _gather_rows_wait(h_hbm, hbuf.at[slot], sems.at[slot], tm)
        hb[...] = hbuf[slot].astype(BF16)

    @pl.when(r < n_used)
    def _():
        h = hb[...]
        act = (_silu(_dot(h, wg_ref[0])) * _dot(h, wu_ref[0])).astype(BF16)
        y = _dot(act, wd_ref[0])

        @pl.when(c == 0)
        def _():
            ys_ref[...] = y

        @pl.when(c > 0)
        def _():
            ys_ref[...] += y

    @pl.when((c == 0) & (r >= n_used))
    def _():
        ys_ref[...] = jnp.zeros(ys_ref.shape, F32)


def _moe_experts(h, src_tiles, tile_expert, n_used, wgu_bf, wd_bf, tm):
    n_tiles = src_tiles.shape[0]
    d = h.shape[1]
    f = wd_bf.shape[1]
    fc = max(c for c in range(LANES, EXPERT_CHUNK_CAP + 1, LANES) if f % c == 0)
    n_fc = f // fc
    chunk = lambda r, c, nu: jnp.where(r < nu[0], c, 0)
    grid_spec = pltpu.PrefetchScalarGridSpec(
        num_scalar_prefetch=2,
        grid=(n_tiles, n_fc),
        in_specs=[pl.BlockSpec((1, 1, tm), lambda r, c, te, nu: (r, 0, 0), memory_space=pltpu.SMEM),
                  pl.BlockSpec((1, 1, tm), lambda r, c, te, nu: (jnp.minimum(r + 1, n_tiles - 1), 0, 0),
                               memory_space=pltpu.SMEM),
                  pl.BlockSpec(memory_space=pl.ANY),
                  pl.BlockSpec((1, d, fc), lambda r, c, te, nu: (te[r], 0, chunk(r, c, nu))),
                  pl.BlockSpec((1, d, fc), lambda r, c, te, nu: (te[r], 0, n_fc + chunk(r, c, nu))),
                  pl.BlockSpec((1, fc, d), lambda r, c, te, nu: (te[r], chunk(r, c, nu), 0))],
        out_specs=pl.BlockSpec((tm, d), lambda r, c, te, nu: (r, 0)),
        scratch_shapes=[pltpu.VMEM((2, tm, d), F32), pltpu.VMEM((tm, d), BF16),
                        pltpu.SemaphoreType.DMA((2,))])
    return pl.pallas_call(
        functools.partial(_experts_kernel, tm=tm),
        grid_spec=grid_spec,
        out_shape=jax.ShapeDtypeStruct((n_tiles * tm, d), F32),
        compiler_params=_params("arbitrary", "arbitrary"),
        name="moe_experts",
    )(tile_expert, n_used, src_tiles, src_tiles, h, wgu_bf, wgu_bf, wd_bf)


def _combine_kernel(idx1_ref, idx2_ref, x3_ref, route_ref, ys_hbm, gfin_ref, out_ref, ybuf, sems, *, tm):
    _gather_rows_start(idx1_ref, ys_hbm, ybuf.at[0], sems.at[0], tm)
    _gather_rows_start(idx2_ref, ys_hbm, ybuf.at[1], sems.at[1], tm)
    route = route_ref[...]
    lane = lax.broadcasted_iota(jnp.int32, route.shape, 1)
    gate = lambda n: jnp.sum(jnp.where(lane == n, route, 0.0), axis=-1, keepdims=True)
    g1, g2 = gate(2), gate(3)
    _gather_rows_wait(ys_hbm, ybuf.at[0], sems.at[0], tm)
    _gather_rows_wait(ys_hbm, ybuf.at[1], sems.at[1], tm)
    out_ref[...] = _rms(x3_ref[...] + (g1 * ybuf[0] + g2 * ybuf[1]), gfin_ref[...])


def _moe_combine(x3, route, ys, slot1, slot2, g_final, tm):
    n, d = x3.shape
    idx = pl.BlockSpec((1, 1, tm), lambda t: (t, 0, 0), memory_space=pltpu.SMEM)
    row = lambda w: pl.BlockSpec((tm, w), lambda t: (t, 0))
    return pl.pallas_call(
        functools.partial(_combine_kernel, tm=tm),
        grid=(n // tm,),
        in_specs=[idx, idx, row(d), row(LANES), pl.BlockSpec(memory_space=pl.ANY), _const_spec((1, d))],
        out_specs=row(d),
        out_shape=jax.ShapeDtypeStruct((n, d), F32),
        scratch_shapes=[pltpu.VMEM((2, tm, d), F32), pltpu.SemaphoreType.DMA((2,))],
        compiler_params=_params("arbitrary"),
        name="moe_combine",
    )(slot1, slot2, x3, route, ys, g_final)


def _routing_tables(route, n_exp, tm):
    n = route.shape[0]
    experts = route[:, :2].astype(jnp.int32).reshape(-1)
    onehot = (experts[:, None] == jnp.arange(n_exp, dtype=jnp.int32)[None, :]).astype(jnp.int32)
    csum = jnp.cumsum(onehot, axis=0)
    rank = jnp.sum(csum * onehot, axis=1) - 1
    counts = csum[-1]
    padded = ((counts + tm - 1) // tm) * tm
    ends = jnp.cumsum(padded)
    starts = ends - padded
    slots = jnp.sum(starts[None, :] * onehot, axis=1) + rank
    n_tiles = (2 * n) // tm + n_exp
    src = jnp.zeros((n_tiles * tm,), jnp.int32).at[slots].set(
        jnp.arange(2 * n, dtype=jnp.int32) // 2, unique_indices=True, mode="promise_in_bounds")
    tile_start = jnp.arange(n_tiles, dtype=jnp.int32) * tm
    tile_expert = jnp.minimum(jnp.sum((tile_start[:, None] >= ends[None, :]).astype(jnp.int32), axis=1),
                              n_exp - 1)
    n_used = (ends[-1] // tm).reshape(1)
    slots = slots.reshape(n, 2)
    return slots[:, 0], slots[:, 1], src, tile_expert, n_used


def _out_moe(x, o, wo_bf, g, router_pad, n_exp, wgu_bf, wd_bf, g_final):
    n, d = x.shape
    tm = _row_tile(n, 512)
    x3, h, route = _moe_route(x, o, wo_bf, g, router_pad, n_exp)
    slot1, slot2, src, tile_expert, n_used = _routing_tables(route, n_exp, tm)
    ys = _moe_experts(h, src.reshape(-1, 1, tm), tile_expert, n_used, wgu_bf, wd_bf, tm)
    return _moe_combine(x3, route, ys, slot1.reshape(-1, 1, tm), slot2.reshape(-1, 1, tm), g_final, tm)


def _lane_block(x, start):
    return jnp.pad(x, ((0, 0), (start, LANES - start - x.shape[1])))


def _swap_halves(x):
    half = x.shape[-1] // 2
    return jnp.concatenate([x[..., half:], x[..., :half]], axis=-1)


def _prep_mla_weights(w_a, g_q, g_kv, w_uq, w_ukv, q_lora, kv_lora, n_heads, nope, rope):
    w_pe = w_a[:, q_lora + kv_lora:]
    w_pe_sw = _swap_halves(w_pe)
    wpe = jnp.concatenate([_lane_block(w_pe, 0), _lane_block(w_pe_sw, 0),
                           _lane_block(w_pe, nope), _lane_block(w_pe_sw, nope)], axis=1)
    uq = w_uq.reshape(q_lora, n_heads, nope + rope)
    uq_nope, uq_pe = uq[..., :nope], uq[..., nope:]
    tail = jnp.zeros((q_lora, n_heads, LANES - nope - rope), w_uq.dtype)
    wuq = jnp.concatenate([uq_nope, uq_pe, tail], axis=-1).reshape(q_lora, n_heads * LANES)
    wuqs = jnp.concatenate([jnp.zeros_like(uq_nope), _swap_halves(uq_pe), tail],
                           axis=-1).reshape(q_lora, n_heads * LANES)
    ukv = w_ukv.reshape(kv_lora, n_heads, -1)
    uk = ukv[..., :nope]
    wuk = jnp.concatenate([uk, jnp.zeros((kv_lora, n_heads, LANES - nope), uk.dtype)],
                          axis=-1).reshape(kv_lora, n_heads * LANES)
    wuv = ukv[..., nope:].reshape(kv_lora, -1)
    bf = lambda a: a.astype(BF16)
    return {"waq": bf(w_a[:, :q_lora]), "wakv": bf(w_a[:, q_lora:q_lora + kv_lora]), "wpe": bf(wpe),
            "gq": g_q[None, :], "gkv": g_kv[None, :], "wuq": bf(wuq), "wuqs": bf(wuqs),
            "wuk": bf(wuk), "wuv": bf(wuv)}


def _rope_tables(pos, nope, rope):
    half = rope // 2
    freqs = jnp.power(ROPE_THETA, -jnp.arange(half, dtype=F32) * 2.0 / rope)
    ang = pos.astype(F32)[:, None] * freqs[None, :]
    cos = jnp.concatenate([jnp.cos(ang)] * 2, axis=1)
    sin = jnp.concatenate([-jnp.sin(ang), jnp.sin(ang)], axis=1)
    c64 = _lane_block(cos, nope) + _lane_block(jnp.ones((pos.shape[0], nope), F32), 0)
    return _lane_block(cos, 0), _lane_block(sin, 0), c64, _lane_block(sin, nope)


def _bf16_split3(x):
    hi = x.astype(BF16).astype(F32)
    mid = (x - hi).astype(BF16).astype(F32)
    lo = (x - hi - mid).astype(BF16).astype(F32)
    return hi, mid, lo


def kernel(x_prompt, x_sample, cache_diff_k, cache_diff_v, cache_mla_ckv, cache_mla_kpe, norm_mix, norm_ffn, norm_final, diff_w_qkv, diff_lambda, diff_subln, diff_w_o, mla_w_a, mla_norm_q, mla_norm_kv, mla_w_uq, mla_w_ukv, mla_w_o, ffn_w_gu, ffn_w_down, moe_router, moe_w_gu, moe_w_down):
    b, t, d = x_prompt.shape
    db, nq, _ = x_sample.shape
    past = cache_diff_k.shape[2]
    n_da_heads, dv = cache_diff_k.shape[3], diff_subln.shape[-1]
    da_width = n_da_heads * dv
    q_lora, kv_lora = mla_norm_q.shape[-1], mla_norm_kv.shape[-1]
    rope = mla_w_a.shape[-1] - q_lora - kv_lora
    mla_width = mla_w_o.shape[1]
    n_exp = moe_router.shape[-1]
    assert norm_mix.shape[0] == 2, "one differential-attention layer followed by one MLA layer"

    uq_cols, ukv_cols = mla_w_uq.shape[-1], mla_w_ukv.shape[-1]
    n_mla_heads = (uq_cols - (ukv_cols - mla_width)) // rope
    nope = uq_cols // n_mla_heads - rope
    mla_v = mla_width // n_mla_heads
    assert n_mla_heads * (nope + mla_v) == ukv_cols and n_mla_heads % 2 == 0
    assert nope + rope <= LANES and 2 * mla_v == LANES

    bf = lambda a: a.astype(BF16)
    lam_init = 0.8 - 0.6 * math.exp(-0.3 * 0)
    da_scale = (dv // 2) ** -0.5 * LOG2E
    mla_scale = (nope + rope) ** -0.5 * LOG2E
    slopes2 = jnp.exp2(-8.0 * jnp.arange(1, n_da_heads + 1, dtype=F32) / n_da_heads) * LOG2E
    slope_parts = jnp.stack(_bf16_split3(slopes2) + (slopes2,), axis=1).reshape(-1)
    slope_rows = jnp.repeat(slopes2, 2 * nq)[:, None]

    w_qkv, w_o0 = bf(diff_w_qkv[0]), bf(diff_w_o[0])
    w_gu0, w_d0 = bf(ffn_w_gu[0]), bf(ffn_w_down[0])
    mla_w = _prep_mla_weights(mla_w_a[0], mla_norm_q[0], mla_norm_kv[0], mla_w_uq[0], mla_w_ukv[0],
                              q_lora, kv_lora, n_mla_heads, nope, rope)
    w_o1 = bf(mla_w_o[0])
    router_pad = jnp.pad(moe_router[0], ((0, 0), (0, LANES - n_exp)))
    moe_gu, moe_d = bf(moe_w_gu.reshape(moe_w_gu.shape[1:])), bf(moe_w_down.reshape(moe_w_down.shape[1:]))
    g_mix, g_ffn, g_fin = norm_mix[:, None, :], norm_ffn[:, None, :], norm_final[None, :]
    lam, subln = diff_lambda[0], diff_subln[0][None, :]

    def trunk(x, pos, attn0, attn1):
        bx, tx, _ = x.shape
        x2d = x.reshape(bx * tx, d)
        q, k, v, kx, vtx = _qkv_proj(x, g_mix[0], w_qkv, dv, da_scale)
        o = attn0(q, k, v, kx, vtx)
        x2 = _out_ffn(x2d, o.reshape(bx * tx, da_width), w_o0, g_ffn[0], w_gu0, w_d0)
        ckv, kpe, pe64, qcat, kcat, vtx1 = _mla_proj(x2.reshape(bx, tx, d), g_mix[1], mla_w,
                                                    _rope_tables(pos, nope, rope), rope, mla_v,
                                                    mla_scale)
        o = attn1(ckv, pe64, qcat, kcat, vtx1)
        y = _out_moe(x2, o.reshape(bx * tx, mla_width), w_o1, g_ffn[1], router_pad, n_exp,
                     moe_gu, moe_d, g_fin)
        return y, k, v, ckv, kpe

    y_p, k_p, v_p, ckv_p, kpe_p = trunk(
        x_prompt, jnp.arange(t, dtype=jnp.int32),
        lambda q, k, v, kx, vtx: _diff_attn_prompt(q, kx, vtx, lam, subln, slope_parts, lam_init),
        lambda ckv, pe64, qcat, kcat, vtx1: _mla_attn_prompt(qcat, kcat, vtx1, mla_v))

    n_s = db * nq
    cache_k = cache_diff_k.reshape(cache_diff_k.shape[1:])
    cache_v = cache_diff_v.reshape(cache_diff_v.shape[1:])
    pos_s = past + jnp.tile(jnp.arange(nq, dtype=jnp.int32), db)

    def attn0_s(q, k, v, kx, vtx):
        r = lambda a: a.reshape(db, nq, da_width)
        return _diff_attn_decode(r(q), cache_k, cache_v, r(k), r(v), lam, subln, slope_rows, lam_init)

    def attn1_s(ckv, pe64, qcat, kcat, vtx1):
        return _mla_attn_decode(qcat.reshape(db, nq, -1), cache_mla_ckv[0], cache_mla_kpe[0],
                                ckv.reshape(db, nq, kv_lora), pe64.reshape(db, nq, LANES),
                                mla_w["wuk"], mla_w["wuv"], nope)

    y_s, k_s, v_s, ckv_s, kpe_s = trunk(x_sample.reshape(1, n_s, d), pos_s, attn0_s, attn1_s)

    return (y_p.reshape(b, t, d), y_s.reshape(db, nq, d),
            k_p.reshape(1, b, t, n_da_heads, dv), v_p.reshape(1, b, t, n_da_heads, dv),
            ckv_p.reshape(1, b, t, kv_lora), kpe_p.reshape(1, b, t, rope),
            k_s.reshape(1, db, nq, n_da_heads, dv), v_s.reshape(1, db, nq, n_da_heads, dv),
            ckv_s.reshape(1, db, nq, kv_lora), kpe_s.reshape(1, db, nq, rope))
```
